```python
import math
import jax, jax.numpy as jnp
from jax import lax
import numpy as np

D_MODEL = 1024
BATCH = 8
SEQ = 2048
DEPTH = 4

RMS_EPS = 1e-6
D_FF = 2816

SSD_WIDTH = D_MODEL
SSD_HEADDIM = 64
SSD_HEADS = SSD_WIDTH // SSD_HEADDIM
SSD_GROUPS = 2
SSD_HPG = SSD_HEADS // SSD_GROUPS
SSD_STATE = 128
SSD_CONV = 5
SSD_CHUNK = 128
SSD_XBC = SSD_WIDTH + 2 * SSD_GROUPS * SSD_STATE

MLA_HEADS = 8
MLA_Q_LORA = D_MODEL // 4
MLA_KV_LORA = D_MODEL // 8
MLA_NOPE = 64
MLA_ROPE = 32
MLA_QK = MLA_NOPE + MLA_ROPE
MLA_V = 64
MLA_WIDTH = MLA_HEADS * MLA_V
ROPE_BASE = 10000.0
ATTN_BLOCK = 128

CONV_WIDTH = D_MODEL // 2
CONV_GROUPS = 8
CONV_K = 3

D_MIX = SSD_WIDTH + MLA_WIDTH + CONV_WIDTH
IN_SIZES = (SSD_WIDTH, SSD_XBC, 2 * SSD_HEADS, MLA_Q_LORA, MLA_KV_LORA, MLA_ROPE, CONV_WIDTH, CONV_WIDTH, CONV_WIDTH)
D_IN = SSD_WIDTH + SSD_XBC + 2 * SSD_HEADS + MLA_Q_LORA + MLA_KV_LORA + MLA_ROPE + 3 * CONV_WIDTH

kernel_name = 'hybrid_ssd_mla_conv_macaron_encoder'


def rms_norm(x, g):
    xf = x.astype(jnp.float32)
    y = xf * lax.rsqrt(jnp.mean(xf * xf, axis=-1, keepdims=True) + RMS_EPS)
    return (y * g.astype(jnp.float32)).astype(x.dtype)


def group_rms_norm(x, g, n_groups):
    shp = x.shape
    xg = x.reshape(shp[:-1] + (n_groups, shp[-1] // n_groups))
    return rms_norm(xg, g.reshape(n_groups, -1)).reshape(shp)


def swiglu(x, w_gate, w_up, w_down):
    return (jax.nn.silu(x @ w_gate) * (x @ w_up)) @ w_down


def depthwise_conv(x, w):
    k, c = w.shape
    return lax.conv_general_dilated(x, w[:, None, :].astype(x.dtype), window_strides=(1,),
                                    padding=[(k // 2, k // 2)],
                                    dimension_numbers=('NWC', 'WIO', 'NWC'),
                                    feature_group_count=c)


def rope(x, pos):
    half = x.shape[-1] // 2
    inv = ROPE_BASE ** (-jnp.arange(half, dtype=jnp.float32) / half)
    ang = pos.astype(jnp.float32)[..., None] * inv
    cos = jnp.cos(ang)[:, :, None, :]
    sin = jnp.sin(ang)[:, :, None, :]
    xf = x.astype(jnp.float32)
    x1, x2 = xf[..., :half], xf[..., half:]
    return jnp.concatenate([x1 * cos - x2 * sin, x1 * sin + x2 * cos], axis=-1).astype(x.dtype)


def split_points(sizes):
    pts, acc = [], 0
    for n in sizes[:-1]:
        acc += n
        pts.append(acc)
    return pts


def ssd_chunked(x, dt, a, bm, cm):
    bsz, s = x.shape[0], x.shape[1]
    nc, L = s // SSD_CHUNK, SSD_CHUNK
    xc = x.reshape(bsz, nc, L, SSD_GROUPS, SSD_HPG, SSD_HEADDIM)
    dtc = dt.reshape(bsz, nc, L, SSD_GROUPS, SSD_HPG)
    bc = bm.reshape(bsz, nc, L, SSD_GROUPS, SSD_STATE)
    cc = cm.reshape(bsz, nc, L, SSD_GROUPS, SSD_STATE)
    xd = xc * dtc[..., None]
    a_cs = jnp.cumsum(dtc * a, axis=2)
    seg = a_cs[:, :, :, None] - a_cs[:, :, None, :]
    lower = jnp.tril(jnp.ones((L, L), dtype=bool))[None, None, :, :, None, None]
    decay = jnp.exp(jnp.where(lower, seg, -jnp.inf))
    cb = jnp.einsum('bclgn,bcsgn->bclsg', cc, bc)
    y_diag = jnp.einsum('bclsg,bclsgr,bcsgrp->bclgrp', cb, decay, xd)
    decay_states = jnp.exp(a_cs[:, :, -1:] - a_cs)
    states = jnp.einsum('bclgn,bclgr,bclgrp->bcgrpn', bc, decay_states, xd)
    chunk_decay = jnp.exp(a_cs[:, :, -1])

    def step(carry, inp):
        st, dec = inp
        return carry * dec[..., None, None] + st, carry

    init = jnp.zeros_like(states[:, 0])
    _, prev = lax.scan(step, init, (jnp.moveaxis(states, 1, 0), jnp.moveaxis(chunk_decay, 1, 0)))
    prev = jnp.moveaxis(prev, 0, 1)
    y_off = jnp.einsum('bclgn,bcgrpn,bclgr->bclgrp', cc, prev, jnp.exp(a_cs))
    return (y_diag + y_off).reshape(bsz, s, SSD_GROUPS, SSD_HPG, SSD_HEADDIM)


def ssd_mixer(z, xbc, dt_raw, conv_w, conv_b, dt_bias, a_log, d_skip, norm_g):
    bsz, s = z.shape[0], z.shape[1]
    xbc = jax.nn.silu(depthwise_conv(xbc, conv_w) + conv_b)
    xs = xbc[..., :SSD_WIDTH].reshape(bsz, s, SSD_GROUPS, SSD_HPG, SSD_HEADDIM).astype(jnp.float32)
    bm = xbc[..., SSD_WIDTH:SSD_WIDTH + SSD_GROUPS * SSD_STATE].reshape(bsz, s, SSD_GROUPS, SSD_STATE).astype(jnp.float32)
    cm = xbc[..., SSD_WIDTH + SSD_GROUPS * SSD_STATE:].reshape(bsz, s, SSD_GROUPS, SSD_STATE).astype(jnp.float32)
    dt = jax.nn.softplus(dt_raw.astype(jnp.float32).reshape(bsz, s, 2, SSD_GROUPS, SSD_HPG)
                         + dt_bias.astype(jnp.float32).reshape(2, SSD_GROUPS, SSD_HPG))
    a = -jnp.exp(a_log.astype(jnp.float32)).reshape(2, SSD_GROUPS, SSD_HPG)
    y_fwd = ssd_chunked(xs, dt[:, :, 0], a[0], bm, cm)
    y_bwd = jnp.flip(ssd_chunked(jnp.flip(xs, 1), jnp.flip(dt[:, :, 1], 1), a[1],
                                 jnp.flip(bm, 1), jnp.flip(cm, 1)), 1)
    y = y_fwd + y_bwd + xs * d_skip.astype(jnp.float32).reshape(SSD_GROUPS, SSD_HPG)[..., None]
    y = y.reshape(bsz, s, SSD_WIDTH) * jax.nn.silu(z.astype(jnp.float32))
    return group_rms_norm(y, norm_g, SSD_GROUPS).astype(z.dtype)


def mla_mixer(q_lat, kv_lat, k_pe, positions, q_norm, w_uq, kv_norm, w_ukv,
              q_head_norm, k_head_norm, out_norm):
    bsz, s = q_lat.shape[0], q_lat.shape[1]
    q = (rms_norm(q_lat, q_norm) @ w_uq).reshape(bsz, s, MLA_HEADS, MLA_QK)
    kv = (rms_norm(kv_lat, kv_norm) @ w_ukv).reshape(bsz, s, MLA_HEADS, MLA_NOPE + MLA_V)
    k_nope, v = kv[..., :MLA_NOPE], kv[..., MLA_NOPE:]
    k = jnp.concatenate([k_nope, jnp.broadcast_to(k_pe[:, :, None, :], (bsz, s, MLA_HEADS, MLA_ROPE))], axis=-1)
    q = rms_norm(q, q_head_norm)
    k = rms_norm(k, k_head_norm)
    q = jnp.concatenate([q[..., :MLA_NOPE], rope(q[..., MLA_NOPE:], positions)], axis=-1)
    k = jnp.concatenate([k[..., :MLA_NOPE], rope(k[..., MLA_NOPE:], positions)], axis=-1)
    scale = MLA_QK ** -0.5
    nb = s // ATTN_BLOCK
    q_blocks = q.reshape(bsz, nb, ATTN_BLOCK, MLA_HEADS, MLA_QK).swapaxes(0, 1)

    def attend(qb):
        sc = jnp.einsum('bqhd,bkhd->bhqk', qb, k).astype(jnp.float32) * scale
        p = jax.nn.softmax(sc, axis=-1).astype(v.dtype)
        return jnp.einsum('bhqk,bkhd->bqhd', p, v)

    o = lax.map(attend, q_blocks).swapaxes(0, 1).reshape(bsz, s, MLA_HEADS, MLA_V)
    return rms_norm(o, out_norm.reshape(MLA_HEADS, MLA_V)).reshape(bsz, s, MLA_WIDTH)


def conv_mixer(h_in, b_gate, c_gate, conv_w, out_norm):
    y = b_gate * depthwise_conv(c_gate * h_in, conv_w)
    return group_rms_norm(y, out_norm, CONV_GROUPS)


def setup_inputs(seed: int = 0) -> dict:
    key = jax.random.key(seed)
    k = jax.random.split(key, 32)
    f32 = jnp.float32

    def w(i, shape, fan_in):
        return jax.random.normal(k[i], shape, f32) * (fan_in ** -0.5)

    def gain(i, shape):
        return 1.0 + 0.05 * jax.random.normal(k[i], shape, f32)

    u = jax.random.uniform(k[9], (DEPTH, 2, SSD_HEADS), f32)
    dt0 = jnp.exp(u * (math.log(0.1) - math.log(1e-3)) + math.log(1e-3))
    ssd_dt_bias = dt0 + jnp.log(-jnp.expm1(-dt0))
    ssd_a_log = jnp.log(jax.random.uniform(k[10], (DEPTH, 2, SSD_HEADS), f32, 1.0, 16.0))
    positions = jnp.broadcast_to(jnp.arange(SEQ, dtype=jnp.int32)[None, :], (BATCH, SEQ))
    return {
        'x': jax.random.normal(k[0], (BATCH, SEQ, D_MODEL), f32),
        'positions': positions,
        'ffn1_norm': gain(1, (DEPTH, D_MODEL)),
        'ffn1_w_gate': w(2, (DEPTH, D_MODEL, D_FF), D_MODEL),
        'ffn1_w_up': w(3, (DEPTH, D_MODEL, D_FF), D_MODEL),
        'ffn1_w_down': w(4, (DEPTH, D_FF, D_MODEL), D_FF),
        'mix_norm': gain(5, (DEPTH, D_MODEL)),
        'w_in': w(6, (DEPTH, D_MODEL, D_IN), D_MODEL),
        'ssd_conv_w': w(7, (DEPTH, SSD_CONV, SSD_XBC), SSD_CONV),
        'ssd_conv_b': 0.02 * jax.random.normal(k[8], (DEPTH, SSD_XBC), f32),
        'ssd_dt_bias': ssd_dt_bias,
        'ssd_a_log': ssd_a_log,
        'ssd_d': 1.0 + 0.1 * jax.random.normal(k[11], (DEPTH, SSD_HEADS), f32),
        'ssd_norm': gain(12, (DEPTH, SSD_WIDTH)),
        'mla_q_norm': gain(13, (DEPTH, MLA_Q_LORA)),
        'mla_w_uq': w(14, (DEPTH, MLA_Q_LORA, MLA_HEADS * MLA_QK), MLA_Q_LORA),
        'mla_kv_norm': gain(15, (DEPTH, MLA_KV_LORA)),
        'mla_w_ukv': w(16, (DEPTH, MLA_KV_LORA, MLA_HEADS * (MLA_NOPE + MLA_V)), MLA_KV_LORA),
        'mla_q_head_norm': gain(17, (DEPTH, MLA_QK)),
        'mla_k_head_norm': gain(18, (DEPTH, MLA_QK)),
        'mla_out_norm': gain(19, (DEPTH, MLA_WIDTH)),
        'conv_w': w(20, (DEPTH, CONV_K, CONV_WIDTH), CONV_K),
        'conv_out_norm': gain(21, (DEPTH, CONV_WIDTH)),
        'w_out': w(22, (DEPTH, D_MIX, D_MODEL), D_MIX),
        'ffn2_norm': gain(23, (DEPTH, D_MODEL)),
        'ffn2_w_gate': w(24, (DEPTH, D_MODEL, D_FF), D_MODEL),
        'ffn2_w_up': w(25, (DEPTH, D_MODEL, D_FF), D_MODEL),
        'ffn2_w_down': w(26, (DEPTH, D_FF, D_MODEL), D_FF),
    }


def reference(x, positions, ffn1_norm, ffn1_w_gate, ffn1_w_up, ffn1_w_down, mix_norm, w_in,
              ssd_conv_w, ssd_conv_b, ssd_dt_bias, ssd_a_log, ssd_d, ssd_norm,
              mla_q_norm, mla_w_uq, mla_kv_norm, mla_w_ukv, mla_q_head_norm, mla_k_head_norm,
              mla_out_norm, conv_w, conv_out_norm, w_out,
              ffn2_norm, ffn2_w_gate, ffn2_w_up, ffn2_w_down):
    pts = split_points(IN_SIZES)
    for l in range(DEPTH):
        x = x + 0.5 * swiglu(rms_norm(x, ffn1_norm[l]), ffn1_w_gate[l], ffn1_w_up[l], ffn1_w_down[l])
        h = rms_norm(x, mix_norm[l])
        u = h @ w_in[l]
        z, xbc, dt_raw, q_lat, kv_lat, k_pe, c_h, c_b, c_c = jnp.split(u, pts, axis=-1)
        y_ssd = ssd_mixer(z, xbc, dt_raw, ssd_conv_w[l], ssd_conv_b[l], ssd_dt_bias[l],
                          ssd_a_log[l], ssd_d[l], ssd_norm[l])
        y_mla = mla_mixer(q_lat, kv_lat, k_pe, positions, mla_q_norm[l], mla_w_uq[l], mla_kv_norm[l],
                          mla_w_ukv[l], mla_q_head_norm[l], mla_k_head_norm[l], mla_out_norm[l])
        y_conv = conv_mixer(c_h, c_b, c_c, conv_w[l], conv_out_norm[l])
        x = x + jnp.concatenate([y_ssd, y_mla, y_conv], axis=-1) @ w_out[l]
        x = x + 0.5 * swiglu(rms_norm(x, ffn2_norm[l]), ffn2_w_gate[l], ffn2_w_up[l], ffn2_w_down[l])
    return x
```

```python
import functools
import math

import jax
import jax.numpy as jnp
from jax import lax
from jax.experimental import pallas as pl
from jax.experimental.pallas import tpu as pltpu

F32 = jnp.float32
BF16 = jnp.bfloat16

D_MODEL = 1024
D_FF = 2816
RMS_EPS = 1e-6

SSD_WIDTH = 1024
SSD_HEADDIM = 64
SSD_HEADS = 16
SSD_GROUPS = 2
SSD_HPG = 8
SSD_STATE = 128
SSD_CONV = 5
SSD_CHUNK = 128
SSD_GW = SSD_HPG * SSD_HEADDIM

MLA_HEADS = 8
MLA_Q_LORA = 256
MLA_KV_LORA = 128
MLA_NOPE = 64
MLA_ROPE = 32
MLA_QK = 96
MLA_V = 64
MLA_WIDTH = 512
ROPE_BASE = 10000.0

CONV_WIDTH = 512
CONV_GROUPS = 8
CONV_K = 3

D_MIX = 2048
LANES = 128
SUBLANES = 8

U_Z = 0
U_XS = 1024
U_B = 2048
U_C = 2304
U_QLAT = 2560
U_KVLAT = 2816
U_MISC = 2944
U_CH = 3072
U_CB = 3584
U_CC = 4096
U_WIDTH = 4608
MISC_KPE = 0
MISC_DT = 32

NEG_BIG = -1e30
VMEM_LIMIT = 56 * 1024 * 1024


def _cparams(sem):
    return pltpu.CompilerParams(dimension_semantics=sem, vmem_limit_bytes=VMEM_LIMIT)


def _dot(a, b):
    return jnp.dot(a, b, preferred_element_type=F32)


def _dot_nt(a, b):
    return lax.dot_general(a, b, (((1,), (1,)), ((), ())), preferred_element_type=F32)


def _split3(v):
    hi = v.astype(BF16)
    r1 = v - hi.astype(F32)
    mid = r1.astype(BF16)
    lo = (r1 - mid.astype(F32)).astype(BF16)
    return hi, mid, lo


def _dot_x01(v, m01):
    hi, mid, lo = _split3(v)
    return _dot(hi, m01) + _dot(mid, m01) + _dot(lo, m01)


def _dot_01x(m01, v):
    hi, mid, lo = _split3(v)
    return _dot(m01, hi) + _dot(m01, mid) + _dot(m01, lo)


def _silu(x):
    return x * jax.nn.sigmoid(x)


def _softplus(x):
    return jnp.maximum(x, 0.0) + jnp.log1p(jnp.exp(-jnp.abs(x)))


def _rms(x, g, n=None):
    n = x.shape[-1] if n is None else n
    ms = jnp.sum(x * x, axis=-1, keepdims=True) * (1.0 / n)
    return x * lax.rsqrt(ms + RMS_EPS) * g


def _ffn_kernel(x_ref, g_ref, wg_ref, wu_ref, wd_ref, o_ref, xn_ref, acc_ref):
    f = pl.program_id(1)

    @pl.when(f == 0)
    def _():
        xn_ref[...] = _rms(x_ref[...], g_ref[...]).astype(BF16)
        acc_ref[...] = jnp.zeros_like(acc_ref)

    xn = xn_ref[...]
    h = _silu(_dot(xn, wg_ref[...])) * _dot(xn, wu_ref[...])
    acc_ref[...] += _dot(h.astype(BF16), wd_ref[...])

    @pl.when(f == pl.num_programs(1) - 1)
    def _():
        o_ref[...] = x_ref[...] + 0.5 * acc_ref[...]


def _ffn(x, norm, wg, wu, wd, layer, tm, tf):
    t = x.shape[0]
    return pl.pallas_call(
        _ffn_kernel,
        grid=(t // tm, D_FF // tf),
        in_specs=[
            pl.BlockSpec((tm, D_MODEL), lambda i, f: (i, 0)),
            pl.BlockSpec((None, 1, D_MODEL), lambda i, f: (layer, 0, 0)),
            pl.BlockSpec((None, D_MODEL, tf), lambda i, f: (layer, 0, f)),
            pl.BlockSpec((None, D_MODEL, tf), lambda i, f: (layer, 0, f)),
            pl.BlockSpec((None, tf, D_MODEL), lambda i, f: (layer, f, 0)),
        ],
        out_specs=pl.BlockSpec((tm, D_MODEL), lambda i, f: (i, 0)),
        out_shape=jax.ShapeDtypeStruct((t, D_MODEL), F32),
        scratch_shapes=[pltpu.VMEM((tm, D_MODEL), BF16), pltpu.VMEM((tm, D_MODEL), F32)],
        compiler_params=_cparams(("parallel", "arbitrary")),
        name="ffn",
    )(x, norm, wg, wu, wd)


def _inproj_kernel(x_ref, g_ref, w_ref, o_ref, xn_ref):
    @pl.when(pl.program_id(1) == 0)
    def _():
        xn_ref[...] = _rms(x_ref[...], g_ref[...]).astype(BF16)

    o_ref[...] = _dot(xn_ref[...], w_ref[...])


def _inproj(x, norm, w, layer, tm, tn):
    t = x.shape[0]
    return pl.pallas_call(
        _inproj_kernel,
        grid=(t // tm, U_WIDTH // tn),
        in_specs=[
            pl.BlockSpec((tm, D_MODEL), lambda i, j: (i, 0)),
            pl.BlockSpec((None, 1, D_MODEL), lambda i, j: (layer, 0, 0)),
            pl.BlockSpec((None, D_MODEL, tn), lambda i, j: (layer, 0, j)),
        ],
        out_specs=pl.BlockSpec((tm, tn), lambda i, j: (i, j)),
        out_shape=jax.ShapeDtypeStruct((t, U_WIDTH), F32),
        scratch_shapes=[pltpu.VMEM((tm, D_MODEL), BF16)],
        compiler_params=_cparams(("parallel", "arbitrary")),
        name="inproj",
    )(x, norm, w)


def _outproj_kernel(x_ref, y0_ref, y1_ref, y2_ref, y3_ref, w0_ref, w1_ref, w2_ref, w3_ref, o_ref):
    acc = _dot(y0_ref[...], w0_ref[...])
    acc += _dot(y1_ref[...], w1_ref[...])
    acc += _dot(y2_ref[...], w2_ref[...])
    acc += _dot(y3_ref[...], w3_ref[...])
    o_ref[...] = x_ref[...] + acc


def _outproj(x, ys, w, layer, tm):
    t = x.shape[0]
    wd = D_MIX // len(ys)
    y_spec = pl.BlockSpec((tm, wd), lambda i: (i, 0))
    w_specs = [pl.BlockSpec((None, wd, D_MODEL), functools.partial(lambda k, i: (layer, k, 0), k))
               for k in range(len(ys))]
    return pl.pallas_call(
        _outproj_kernel,
        grid=(t // tm,),
        in_specs=[pl.BlockSpec((tm, D_MODEL), lambda i: (i, 0))] + [y_spec] * len(ys) + w_specs,
        out_specs=pl.BlockSpec((tm, D_MODEL), lambda i: (i, 0)),
        out_shape=jax.ShapeDtypeStruct((t, D_MODEL), F32),
        compiler_params=_cparams(("parallel",)),
        name="outproj",
    )(x, *ys, *([w] * len(ys)))


def _conv_silu_into(src_ref, pad_ref, w, dst_ref, seq):
    half = SSD_CONV // 2
    width = src_ref.shape[1]
    zeros = jnp.zeros((SUBLANES, width), F32)
    pad_ref[0:SUBLANES, :] = zeros
    pad_ref[SUBLANES + seq:2 * SUBLANES + seq, :] = zeros
    pad_ref[SUBLANES:SUBLANES + seq, :] = src_ref[...]
    rows = SSD_CHUNK
    for c in range(seq // rows):
        for j in range(width // LANES):
            ls = slice(j * LANES, (j + 1) * LANES)
            acc = jnp.broadcast_to(w[SSD_CONV:SSD_CONV + 1, ls], (rows, LANES))
            for k in range(SSD_CONV):
                r0 = SUBLANES - half + k + c * rows
                acc = acc + w[k:k + 1, ls] * pad_ref[r0:r0 + rows, ls]
            dst_ref[c * rows:(c + 1) * rows, ls] = _silu(acc).astype(dst_ref.dtype)


def _ssd_kernel(dt_lane, z_ref, xs_ref, b_ref, c_ref, misc_ref, dtt_ref, cwx_ref, cwb_ref, cwc_ref,
                hp_ref, hpt_ref, dx_ref, ng_ref, ef_ref, eb_ref, o_ref,
                xpad, bpad, cpad, xc, bc, cc, bt, yacc, dts, dtat, st):
    seq = o_ref.shape[0]
    ch = SSD_CHUNK
    nc = seq // ch
    fl = dt_lane
    bl = dt_lane + SSD_HPG

    _conv_silu_into(xs_ref, xpad, cwx_ref[...], xc, seq)
    _conv_silu_into(b_ref, bpad, cwb_ref[...], bc, seq)
    _conv_silu_into(c_ref, cpad, cwc_ref[...], cc, seq)
    for c in range(nc):
        rs = slice(c * ch, (c + 1) * ch)
        bt[:, rs] = bc[rs, :].T.astype(BF16)

    hp = hp_ref[...]
    bias_row = hp[0:1, :]
    a_row = -jnp.exp(hp[1:2, :]) * hp[2:3, :]
    hpt = hpt_ref[...]
    dts[...] = _softplus(misc_ref[...] + bias_row)
    dtat[...] = _softplus(dtt_ref[...] + hpt[:, 0:1]) * (-jnp.exp(hpt[:, 1:2]))
    st[...] = jnp.zeros_like(st)

    row_i = lax.broadcasted_iota(jnp.int32, (ch, ch), 0)
    col_i = lax.broadcasted_iota(jnp.int32, (ch, ch), 1)
    tri_le = col_i <= row_i
    tri_ge = col_i >= row_i
    t_le = jnp.where(tri_le, 1.0, 0.0).astype(BF16)
    t_ge = jnp.where(tri_ge, 1.0, 0.0).astype(BF16)
    lane_lo = lax.broadcasted_iota(jnp.int32, (ch, LANES), 1) < SSD_HEADDIM
    ef = ef_ref[...]
    eb = eb_ref[...]

    def fwd_body(c, carry):
        r0 = pl.multiple_of(c * ch, ch)
        rows = pl.ds(r0, ch)
        x = xc[rows, :]
        bmat = bc[rows, :].astype(BF16)
        cmat = cc[rows, :].astype(BF16)
        dt = dts[rows, :]
        dta = dt * a_row
        cum_f = _dot_01x(t_le, dta)
        cum_b = _dot_01x(t_ge, dta)
        dtat_c = dtat[:, rows]
        cumt_f = _dot_x01(dtat_c, t_ge)
        cumt_b = _dot_x01(dtat_c, t_le)
        cb = _dot_nt(cmat, bmat)
        xdf = x * _dot_x01(dt, ef)
        xdb = x * _dot_x01(dt, eb)
        zero = jnp.zeros((ch, LANES), F32)
        for p in range(SSD_HPG // 2):
            ls = slice(p * LANES, (p + 1) * LANES)
            xf, xb = xdf[:, ls], xdb[:, ls]
            rhs = jnp.concatenate([
                jnp.where(lane_lo, xf, zero), jnp.where(lane_lo, xb, zero),
                jnp.where(lane_lo, zero, xf), jnp.where(lane_lo, zero, xb)], axis=0).astype(BF16)
            ms = []
            for r in (2 * p, 2 * p + 1):
                seg_f = cum_f[:, fl + r:fl + r + 1] - cumt_f[r:r + 1, :]
                seg_b = cum_b[:, bl + r:bl + r + 1] - cumt_b[SSD_HPG + r:SSD_HPG + r + 1, :]
                ms.append(cb * jnp.exp(jnp.where(tri_le, seg_f, NEG_BIG)))
                ms.append(cb * jnp.exp(jnp.where(tri_ge, seg_b, NEG_BIG)))
            m = jnp.concatenate(ms, axis=1).astype(BF16)
            yacc[rows, ls] = _dot(m, rhs)
        prev = st[0]
        y_off = _dot(cmat, prev.astype(BF16)) * _dot_x01(jnp.exp(cum_f), ef)
        yacc[rows, :] += y_off
        last = cum_f[ch - 1:ch, :]
        w = dt * jnp.exp(last - cum_f)
        s_new = _dot(bt[:, rows], (x * _dot_x01(w, ef)).astype(BF16))
        dec = _dot_x01(jnp.exp(jnp.broadcast_to(last, (SUBLANES, LANES))), ef)[0:1, :]
        st[0] = prev * dec + s_new
        return carry

    lax.fori_loop(0, nc, fwd_body, 0)

    d_x = dx_ref[...]
    gain = ng_ref[...]

    def bwd_body(i, carry):
        c = nc - 1 - i
        r0 = pl.multiple_of(c * ch, ch)
        rows = pl.ds(r0, ch)
        x = xc[rows, :]
        cmat = cc[rows, :].astype(BF16)
        dt = dts[rows, :]
        cum_b = _dot_01x(t_ge, dt * a_row)
        prev = st[1]
        y_off = _dot(cmat, prev.astype(BF16)) * _dot_x01(jnp.exp(cum_b), eb)
        first = cum_b[0:1, :]
        w = dt * jnp.exp(first - cum_b)
        s_new = _dot(bt[:, rows], (x * _dot_x01(w, eb)).astype(BF16))
        dec = _dot_x01(jnp.exp(jnp.broadcast_to(first, (SUBLANES, LANES))), eb)[0:1, :]
        st[1] = prev * dec + s_new
        y = (yacc[rows, :] + y_off + x * d_x) * _silu(z_ref[rows, :])
        o_ref[rows, :] = _rms(y, gain).astype(o_ref.dtype)
        return carry

    lax.fori_loop(0, nc, bwd_body, 0)


def _ssd_group(u3, dtt, cw_ext, hp, hpt, dx, ng, ef, eb, group):
    bsz, seq, _ = u3.shape
    gw = SSD_GW
    n = SSD_STATE
    dt_lane = MISC_DT + group * 2 * SSD_HPG
    blk = lambda w, idx: pl.BlockSpec((None, seq, w), lambda b: (b, 0, idx))
    full2 = lambda a: pl.BlockSpec(a.shape, lambda b: (0, 0))
    return pl.pallas_call(
        functools.partial(_ssd_kernel, dt_lane),
        grid=(bsz,),
        in_specs=[
            blk(gw, U_Z // gw + group),
            blk(gw, U_XS // gw + group),
            blk(n, U_B // n + group),
            blk(n, U_C // n + group),
            blk(LANES, U_MISC // LANES),
            pl.BlockSpec((None, 2 * SSD_HPG, seq), lambda b: (b, group, 0)),
            pl.BlockSpec((SUBLANES, gw), lambda b: (0, group)),
            pl.BlockSpec((SUBLANES, n), lambda b: (0, SSD_WIDTH // n + group)),
            pl.BlockSpec((SUBLANES, n), lambda b: (0, SSD_WIDTH // n + SSD_GROUPS + group)),
            full2(hp), full2(hpt), full2(dx),
            pl.BlockSpec((1, gw), lambda b: (0, group)),
            full2(ef), full2(eb),
        ],
        out_specs=pl.BlockSpec((None, seq, gw), lambda b: (b, 0, 0)),
        out_shape=jax.ShapeDtypeStruct((bsz, seq, gw), BF16),
        scratch_shapes=[
            pltpu.VMEM((seq + 2 * SUBLANES, gw), F32),
            pltpu.VMEM((seq + 2 * SUBLANES, n), F32),
            pltpu.VMEM((seq + 2 * SUBLANES, n), F32),
            pltpu.VMEM((seq, gw), F32),
            pltpu.VMEM((seq, n), F32),
            pltpu.VMEM((seq, n), F32),
            pltpu.VMEM((n, seq), BF16),
            pltpu.VMEM((seq, gw), F32),
            pltpu.VMEM((seq, LANES), F32),
            pltpu.VMEM((2 * SSD_HPG, seq), F32),
            pltpu.VMEM((2, n, gw), F32),
        ],
        compiler_params=_cparams(("parallel",)),
        name=f"ssd_g{group}",
    )(u3, u3, u3, u3, u3, dtt, cw_ext, cw_ext, cw_ext, hp, hpt, dx, ng, ef, eb)


def _mla_kernel(tq, qlat_ref, kvlat_ref, misc_ref, pos_ref, qn_ref, wq_ref, wqr_ref, kvn_ref,
                wk_ref, wv_ref, gq_ref, gk_ref, go_ref, invf_ref, pp_ref, pr_ref, o_ref,
                k_scr, v_scr, cos_scr, sin_scr):
    qi = pl.program_id(1)
    hs = LANES

    @pl.when(qi == 0)
    def _():
        ang = pos_ref[...].astype(F32) * invf_ref[...]
        cos = jnp.cos(ang)
        sin = jnp.sin(ang)
        cos_scr[...] = cos
        sin_scr[...] = sin
        kvn = _rms(kvlat_ref[...], kvn_ref[...]).astype(BF16)
        hi, mid, _ = _split3(misc_ref[...])
        pe = _dot(hi, pp_ref[...]) + _dot(mid, pp_ref[...])
        pe_rot = _dot(hi, pr_ref[...]) + _dot(mid, pr_ref[...])
        gk = gk_ref[...]
        kr = pe * gk[0:1, :] * cos + pe_rot * gk[1:2, :] * sin
        ss_pe = jnp.sum(pe * pe, axis=-1, keepdims=True)
        for h in range(MLA_HEADS):
            ls = slice(h * hs, (h + 1) * hs)
            kp = _dot(kvn, wk_ref[:, ls])
            ss = jnp.sum(kp * kp, axis=-1, keepdims=True) + ss_pe
            rinv = lax.rsqrt(ss * (1.0 / MLA_QK) + RMS_EPS)
            k_scr[h] = ((kp * gk[0:1, :] + kr) * rinv).astype(BF16)
            v_scr[h] = _dot(kvn, wv_ref[:, ls]).astype(BF16)

    q0 = pl.multiple_of(qi * tq, tq)
    cosq = cos_scr[pl.ds(q0, tq), :]
    sinq = sin_scr[pl.ds(q0, tq), :]
    qn = _rms(qlat_ref[...], qn_ref[...]).astype(BF16)
    gq = gq_ref[...]
    gcos = gq[0:1, :] * cosq
    gsin = gq[1:2, :] * sinq
    scale = MLA_QK ** -0.5
    pair = None
    for h in range(MLA_HEADS):
        ls = slice(h * hs, (h + 1) * hs)
        qp = _dot(qn, wq_ref[:, ls])
        qr = _dot(qn, wqr_ref[:, ls])
        ss = jnp.sum(qp * qp, axis=-1, keepdims=True)
        rinv = lax.rsqrt(ss * (1.0 / MLA_QK) + RMS_EPS) * scale
        q = ((qp * gcos + qr * gsin) * rinv).astype(BF16)
        s = _dot_nt(q, k_scr[h])
        m = jnp.max(s, axis=-1, keepdims=True)
        p = jnp.exp(s - m)
        l = jnp.sum(p, axis=-1, keepdims=True)
        o = _dot(p.astype(BF16), v_scr[h]) / l
        ms = jnp.sum(o * o, axis=-1, keepdims=True) * (1.0 / MLA_V)
        on = o * lax.rsqrt(ms + RMS_EPS)
        if h % 2 == 0:
            pair = on
        else:
            ps = slice((h // 2) * hs, (h // 2 + 1) * hs)
            o_ref[:, ps] = ((pair + on) * go_ref[:, ps]).astype(o_ref.dtype)


def _mla(u3, pos3, qn, wq, wqr, kvn, wk, wv, gq, gk, go, invf, pp, pr, tq):
    bsz, seq, _ = u3.shape
    full2 = lambda a: pl.BlockSpec(a.shape, lambda b, i: (0, 0))
    return pl.pallas_call(
        functools.partial(_mla_kernel, tq),
        grid=(bsz, seq // tq),
        in_specs=[
            pl.BlockSpec((None, tq, MLA_Q_LORA), lambda b, i: (b, i, U_QLAT // MLA_Q_LORA)),
            pl.BlockSpec((None, seq, MLA_KV_LORA), lambda b, i: (b, 0, U_KVLAT // MLA_KV_LORA)),
            pl.BlockSpec((None, seq, LANES), lambda b, i: (b, 0, U_MISC // LANES)),
            pl.BlockSpec((None, seq, 1), lambda b, i: (b, 0, 0)),
            full2(qn), full2(wq), full2(wqr), full2(kvn), full2(wk), full2(wv),
            full2(gq), full2(gk), full2(go), full2(invf), full2(pp), full2(pr),
        ],
        out_specs=pl.BlockSpec((None, tq, MLA_WIDTH), lambda b, i: (b, i, 0)),
        out_shape=jax.ShapeDtypeStruct((bsz, seq, MLA_WIDTH), BF16),
        scratch_shapes=[
            pltpu.VMEM((MLA_HEADS, seq, LANES), BF16),
            pltpu.VMEM((MLA_HEADS, seq, LANES), BF16),
            pltpu.VMEM((seq, LANES), F32),
            pltpu.VMEM((seq, LANES), F32),
        ],
        compiler_params=_cparams(("parallel", "arbitrary")),
        name="mla",
    )(u3, u3, u3, pos3, qn, wq, wqr, kvn, wk, wv, gq, gk, go, invf, pp, pr)


def _gconv_kernel(h_ref, b_ref, c_ref, w_ref, g_ref, gm_ref, o_ref, pad):
    seq = o_ref.shape[0]
    half = CONV_K // 2
    zeros = jnp.zeros((SUBLANES, LANES), F32)
    pad[0:SUBLANES, :] = zeros
    pad[SUBLANES + seq:2 * SUBLANES + seq, :] = zeros
    pad[SUBLANES:SUBLANES + seq, :] = c_ref[...] * h_ref[...]
    w = w_ref[...]
    gm = gm_ref[...]
    rows = 256
    for c in range(seq // rows):
        acc = jnp.zeros((rows, LANES), F32)
        for k in range(CONV_K):
            r0 = SUBLANES - half + k + c * rows
            acc = acc + w[k:k + 1, :] * pad[r0:r0 + rows, :]
        rs = slice(c * rows, (c + 1) * rows)
        y = acc * b_ref[rs, :]
        hi, mid, _ = _split3(y * y)
        ss = _dot(hi, gm) + _dot(mid, gm)
        gsz = CONV_WIDTH // CONV_GROUPS
        o_ref[rs, :] = (y * lax.rsqrt(ss * (1.0 / gsz) + RMS_EPS) * g_ref[...]).astype(o_ref.dtype)


def _gconv(u3, w_ext, g, gm):
    bsz, seq, _ = u3.shape
    nj = CONV_WIDTH // LANES
    blk = lambda off: pl.BlockSpec((None, seq, LANES), lambda b, j: (b, 0, off // LANES + j))
    return pl.pallas_call(
        _gconv_kernel,
        grid=(bsz, nj),
        in_specs=[
            blk(U_CH), blk(U_CB), blk(U_CC),
            pl.BlockSpec((SUBLANES, LANES), lambda b, j: (0, j)),
            pl.BlockSpec((1, LANES), lambda b, j: (0, j)),
            pl.BlockSpec(gm.shape, lambda b, j: (0, 0)),
        ],
        out_specs=pl.BlockSpec((None, seq, LANES), lambda b, j: (b, 0, j)),
        out_shape=jax.ShapeDtypeStruct((bsz, seq, CONV_WIDTH), BF16),
        scratch_shapes=[pltpu.VMEM((seq + 2 * SUBLANES, LANES), F32)],
        compiler_params=_cparams(("parallel", "parallel")),
        name="gconv",
    )(u3, u3, u3, w_ext, g, gm)


def _rot_half_perm(n):
    h = n // 2
    idx = [(j + h) % n for j in range(n)]
    sign = [-1.0 if j < h else 1.0 for j in range(n)]
    return idx, sign


def _prep_w_in(w_in):
    depth = w_in.shape[0]
    o_xbc = SSD_WIDTH
    o_dt = o_xbc + SSD_WIDTH + 2 * SSD_GROUPS * SSD_STATE
    o_q = o_dt + 2 * SSD_HEADS
    o_kv = o_q + MLA_Q_LORA
    o_pe = o_kv + MLA_KV_LORA
    o_ch = o_pe + MLA_ROPE
    dt_idx = [o_dt + d * SSD_HEADS + g * SSD_HPG + r
              for g in range(SSD_GROUPS) for d in range(2) for r in range(SSD_HPG)]
    pad = jnp.zeros((depth, D_MODEL, U_CH - U_MISC - MLA_ROPE - 2 * SSD_HEADS), w_in.dtype)
    cols = [w_in[:, :, :o_dt], w_in[:, :, o_q:o_pe], w_in[:, :, o_pe:o_ch],
            w_in[:, :, jnp.array(dt_idx)], pad, w_in[:, :, o_ch:]]
    return jnp.concatenate(cols, axis=-1).astype(BF16)


def _prep_mla(w_uq, w_ukv, q_head_norm, k_head_norm):
    depth = w_uq.shape[0]
    ridx, rsign = _rot_half_perm(MLA_ROPE)
    rsign = jnp.array(rsign, F32)
    wq3 = w_uq.reshape(depth, MLA_Q_LORA, MLA_HEADS, MLA_QK)
    zq = jnp.zeros((depth, MLA_Q_LORA, MLA_HEADS, LANES - MLA_QK), w_uq.dtype)
    wq = jnp.concatenate([wq3, zq], axis=-1).reshape(depth, MLA_Q_LORA, MLA_HEADS * LANES)
    rot = wq3[..., MLA_NOPE:][..., jnp.array(ridx)] * rsign
    zn = jnp.zeros((depth, MLA_Q_LORA, MLA_HEADS, MLA_NOPE), w_uq.dtype)
    wqr = jnp.concatenate([zn, rot, zq], axis=-1).reshape(depth, MLA_Q_LORA, MLA_HEADS * LANES)
    wkv3 = w_ukv.reshape(depth, MLA_KV_LORA, MLA_HEADS, MLA_NOPE + MLA_V)
    zk = jnp.zeros((depth, MLA_KV_LORA, MLA_HEADS, LANES - MLA_NOPE), w_ukv.dtype)
    wk = jnp.concatenate([wkv3[..., :MLA_NOPE], zk], axis=-1).reshape(depth, MLA_KV_LORA, MLA_HEADS * LANES)
    v4 = wkv3[..., MLA_NOPE:].reshape(depth, MLA_KV_LORA, MLA_HEADS // 2, 2, MLA_V)
    zv = jnp.zeros_like(v4[:, :, :, 0])
    wv = jnp.stack([jnp.concatenate([v4[:, :, :, 0], zv], axis=-1),
                    jnp.concatenate([zv, v4[:, :, :, 1]], axis=-1)], axis=3)
    wv = wv.reshape(depth, MLA_KV_LORA, MLA_HEADS * LANES)

    def gains(g):
        zt = jnp.zeros((depth, LANES - MLA_QK), F32)
        plain = jnp.concatenate([g, zt], axis=-1)
        rot_g = jnp.concatenate([jnp.zeros((depth, MLA_NOPE), F32), g[:, MLA_NOPE:][:, jnp.array(ridx)], zt], axis=-1)
        rest = jnp.zeros((depth, SUBLANES - 2, LANES), F32)
        return jnp.concatenate([plain[:, None], rot_g[:, None], rest], axis=1)

    return (wq.astype(BF16), wqr.astype(BF16), wk.astype(BF16), wv.astype(BF16),
            gains(q_head_norm), gains(k_head_norm))


def _mla_consts():
    ridx, rsign = _rot_half_perm(MLA_ROPE)
    half = MLA_ROPE // 2
    inv = ROPE_BASE ** (-jnp.arange(half, dtype=F32) / half)
    invf = jnp.zeros((1, LANES), F32).at[0, MLA_NOPE:MLA_QK].set(jnp.concatenate([inv, inv]))
    src = jnp.arange(MLA_ROPE)
    pp = jnp.zeros((LANES, LANES), F32).at[MISC_KPE + src, MLA_NOPE + src].set(1.0)
    pr = jnp.zeros((LANES, LANES), F32).at[MISC_KPE + jnp.array(ridx), MLA_NOPE + src].set(jnp.array(rsign, F32))
    return invf, pp.astype(BF16), pr.astype(BF16)


def _prep_ssd(ssd_conv_w, ssd_conv_b, ssd_dt_bias, ssd_a_log, ssd_d):
    depth = ssd_conv_w.shape[0]
    cw_ext = jnp.concatenate([ssd_conv_w, ssd_conv_b[:, None, :],
                              jnp.zeros((depth, SUBLANES - SSD_CONV - 1, ssd_conv_w.shape[-1]), F32)], axis=1)

    def per_group(p):
        return p.reshape(depth, 2, SSD_GROUPS, SSD_HPG).transpose(0, 2, 1, 3).reshape(depth, SSD_GROUPS, 2 * SSD_HPG)

    bias_g, alog_g = per_group(ssd_dt_bias), per_group(ssd_a_log)
    hp = jnp.zeros((depth, SSD_GROUPS, SUBLANES, LANES), F32)
    hpt = jnp.zeros((depth, SSD_GROUPS, 2 * SSD_HPG, SUBLANES), F32)
    for g in range(SSD_GROUPS):
        lo = MISC_DT + g * 2 * SSD_HPG
        hp = hp.at[:, g, 0, lo:lo + 2 * SSD_HPG].set(bias_g[:, g])
        hp = hp.at[:, g, 1, lo:lo + 2 * SSD_HPG].set(alog_g[:, g])
        hp = hp.at[:, g, 2, lo:lo + 2 * SSD_HPG].set(1.0)
    hpt = hpt.at[:, :, :, 0].set(bias_g).at[:, :, :, 1].set(alog_g)
    dx = jnp.repeat(ssd_d.reshape(depth, SSD_GROUPS, SSD_HPG), SSD_HEADDIM, axis=-1)[:, :, None, :]
    return cw_ext, hp, hpt, dx


def _ssd_expanders():
    out = []
    for g in range(SSD_GROUPS):
        per_dir = []
        for d in range(2):
            lo = MISC_DT + g * 2 * SSD_HPG + d * SSD_HPG
            rows = jnp.repeat(lo + jnp.arange(SSD_HPG), SSD_HEADDIM)
            e = jnp.zeros((LANES, SSD_GW), F32).at[rows, jnp.arange(SSD_GW)].set(1.0)
            per_dir.append(e.astype(BF16))
        out.append(per_dir)
    return out


def _prep_mixers(positions, ssd_conv_w, ssd_conv_b, ssd_dt_bias, ssd_a_log, ssd_d, ssd_norm,
                 mla_q_norm, mla_w_uq, mla_kv_norm, mla_w_ukv, mla_q_head_norm, mla_k_head_norm,
                 mla_out_norm, conv_w, conv_out_norm):
    depth = conv_w.shape[0]
    bsz, seq = positions.shape
    gsz = CONV_WIDTH // CONV_GROUPS
    li = jnp.arange(LANES)
    return dict(
        mla_w=_prep_mla(mla_w_uq, mla_w_ukv, mla_q_head_norm, mla_k_head_norm),
        mla_c=_mla_consts(),
        mla_n=(mla_q_norm, mla_kv_norm, mla_out_norm),
        ssd=_prep_ssd(ssd_conv_w, ssd_conv_b, ssd_dt_bias, ssd_a_log, ssd_d),
        ssd_norm=ssd_norm,
        expanders=_ssd_expanders(),
        gc_w=jnp.concatenate([conv_w, jnp.zeros((depth, SUBLANES - CONV_K, CONV_WIDTH), F32)], axis=1),
        gc_g=conv_out_norm,
        gm=(li[:, None] // gsz == li[None, :] // gsz).astype(BF16),
        pos3=positions.reshape(bsz, seq, 1),
    )


def _mixers(u3, mp, l):
    bsz, seq, _ = u3.shape
    t = bsz * seq
    tq = 512 if seq % 512 == 0 else SSD_CHUNK
    cw_ext, hp, hpt, dx = mp["ssd"]
    dtt = jnp.swapaxes(u3[:, :, U_MISC + MISC_DT:U_MISC + MISC_DT + 2 * SSD_HEADS], 1, 2)
    ys = [_ssd_group(u3, dtt, cw_ext[l], hp[l, g], hpt[l, g], dx[l, g], mp["ssd_norm"][l][None, :],
                     mp["expanders"][g][0], mp["expanders"][g][1], g).reshape(t, SSD_GW)
          for g in range(SSD_GROUPS)]
    wq, wqr, wk, wv, gq, gk = mp["mla_w"]
    invf, pp, pr = mp["mla_c"]
    qn, kvn, go = mp["mla_n"]
    y_mla = _mla(u3, mp["pos3"], qn[l][None, :], wq[l], wqr[l], kvn[l][None, :], wk[l], wv[l],
                 gq[l], gk[l], go[l][None, :], invf, pp, pr, tq).reshape(t, MLA_WIDTH)
    y_conv = _gconv(u3, mp["gc_w"][l], mp["gc_g"][l][None, :], mp["gm"]).reshape(t, CONV_WIDTH)
    return ys + [y_mla, y_conv]


def kernel(x, positions, ffn1_norm, ffn1_w_gate, ffn1_w_up, ffn1_w_down, mix_norm, w_in, ssd_conv_w, ssd_conv_b, ssd_dt_bias, ssd_a_log, ssd_d, ssd_norm, mla_q_norm, mla_w_uq, mla_kv_norm, mla_w_ukv, mla_q_head_norm, mla_k_head_norm, mla_out_norm, conv_w, conv_out_norm, w_out, ffn2_norm, ffn2_w_gate, ffn2_w_up, ffn2_w_down):
    bsz, seq, _ = x.shape
    depth = w_in.shape[0]
    t = bsz * seq
    assert seq % SSD_CHUNK == 0 and t % 256 == 0
    tm = 1024 if t % 1024 == 0 else 256

    row = lambda a: a[:, None, :]
    f1 = (row(ffn1_norm), ffn1_w_gate.astype(BF16), ffn1_w_up.astype(BF16), ffn1_w_down.astype(BF16))
    f2 = (row(ffn2_norm), ffn2_w_gate.astype(BF16), ffn2_w_up.astype(BF16), ffn2_w_down.astype(BF16))
    w_in_p = _prep_w_in(w_in)
    w_out_b = w_out.astype(BF16)
    mp = _prep_mixers(positions, ssd_conv_w, ssd_conv_b, ssd_dt_bias, ssd_a_log, ssd_d, ssd_norm,
                      mla_q_norm, mla_w_uq, mla_kv_norm, mla_w_ukv, mla_q_head_norm, mla_k_head_norm,
                      mla_out_norm, conv_w, conv_out_norm)

    xt = x.reshape(t, D_MODEL)
    for l in range(depth):
        xt = _ffn(xt, *f1, l, tm, 256)
        u = _inproj(xt, row(mix_norm), w_in_p, l, tm, 512)
        ys = _mixers(u.reshape(bsz, seq, U_WIDTH), mp, l)
        xt = _outproj(xt, ys, w_out_b, l, tm)
        xt = _ffn(xt, *f2, l, tm, 256)
    return xt.reshape(bsz, seq, D_MODEL)
```

```python
import functools
import math

import numpy as np
import jax
import jax.numpy as jnp
from jax import lax
from jax.experimental import pallas as pl
from jax.experimental.pallas import tpu as pltpu

F32 = jnp.float32
BF16 = jnp.bfloat16

D_MODEL = 1024
D_FF = 2816
RMS_EPS = 1e-6

SSD_WIDTH = 1024
SSD_HEADDIM = 64
SSD_HEADS = 16
SSD_GROUPS = 2
SSD_HPG = 8
SSD_STATE = 128
SSD_CONV = 5
SSD_CHUNK = 128
SSD_GW = SSD_HPG * SSD_HEADDIM

MLA_HEADS = 8
MLA_Q_LORA = 256
MLA_KV_LORA = 128
MLA_NOPE = 64
MLA_ROPE = 32
MLA_QK = 96
MLA_V = 64
MLA_WIDTH = 512
ROPE_BASE = 10000.0

CONV_WIDTH = 512
CONV_GROUPS = 8
CONV_K = 3

D_MIX = 2048
LANES = 128
SUBLANES = 8

U_Z = 0
U_XS = 1024
U_B = 2048
U_C = 2304
U_QLAT = 2560
U_KVLAT = 2816
U_MISC = 2944
U_CH = 3072
U_CB = 3584
U_CC = 4096
U_WIDTH = 4608
MISC_KPE = 0
MISC_DT = 32

NEG_BIG = -1e30
LOG2E = math.log2(math.e)
VMEM_LIMIT = 56 * 1024 * 1024
FFN_TF = 256


def _cparams(sem):
    return pltpu.CompilerParams(dimension_semantics=sem, vmem_limit_bytes=VMEM_LIMIT)


def _resident(block_shape, index_map):
    return pl.BlockSpec(block_shape, index_map, pipeline_mode=pl.Buffered(1))


def _dot(a, b):
    return jnp.dot(a, b, preferred_element_type=F32)


def _dot_nt(a, b):
    return lax.dot_general(a, b, (((1,), (1,)), ((), ())), preferred_element_type=F32)


def _split(v, parts):
    out = []
    r = v
    for i in range(parts):
        t = r.astype(BF16)
        out.append(t)
        if i + 1 < parts:
            r = r - t.astype(F32)
    return out


def _dot_x01(v, m01, parts=3):
    acc = None
    for t in _split(v, parts):
        d = _dot(t, m01)
        acc = d if acc is None else acc + d
    return acc


def _dot_01x(m01, v, parts=3):
    acc = None
    for t in _split(v, parts):
        d = _dot(m01, t)
        acc = d if acc is None else acc + d
    return acc


def _silu(x):
    return x * jax.nn.sigmoid(x)


def _softplus(x):
    return jnp.maximum(x, 0.0) + jnp.log1p(jnp.exp(-jnp.abs(x)))


def _rms(x, g, n=None):
    n = x.shape[-1] if n is None else n
    ms = jnp.sum(x * x, axis=-1, keepdims=True) * (1.0 / n)
    return x * lax.rsqrt(ms + RMS_EPS) * g


def _ffn_kernel(x_ref, g_ref, wg_ref, wu_ref, wd_ref, o_ref, xn_ref):
    x = x_ref[...]
    xn_ref[...] = _rms(x, g_ref[...]).astype(BF16)
    o_ref[...] = x

    def body(f, carry):
        cs = pl.ds(pl.multiple_of(f * FFN_TF, FFN_TF), FFN_TF)
        xn = xn_ref[...]
        h = _silu(_dot(xn, wg_ref[:, cs])) * _dot(xn, wu_ref[:, cs])
        o_ref[...] += _dot(h.astype(BF16), wd_ref[cs, :])
        return carry

    lax.fori_loop(0, D_FF // FFN_TF, body, 0)


def _ffn(x, norm, wg, wu, wd_half, layer, tm):
    t = x.shape[0]
    return pl.pallas_call(
        _ffn_kernel,
        grid=(t // tm,),
        in_specs=[
            pl.BlockSpec((tm, D_MODEL), lambda i: (i, 0)),
            _resident((None, 1, D_MODEL), lambda i: (layer, 0, 0)),
            _resident((None, D_MODEL, D_FF), lambda i: (layer, 0, 0)),
            _resident((None, D_MODEL, D_FF), lambda i: (layer, 0, 0)),
            _resident((None, D_FF, D_MODEL), lambda i: (layer, 0, 0)),
        ],
        out_specs=pl.BlockSpec((tm, D_MODEL), lambda i: (i, 0)),
        out_shape=jax.ShapeDtypeStruct((t, D_MODEL), F32),
        scratch_shapes=[pltpu.VMEM((tm, D_MODEL), BF16)],
        compiler_params=_cparams(("parallel",)),
        name="ffn",
    )(x, norm, wg, wu, wd_half)


def _inproj_kernel(x_ref, g_ref, w_ref, u_ref, dtt_ref):
    xn = _rms(x_ref[...], g_ref[...]).astype(BF16)
    u_ref[...] = _dot(xn, w_ref[...])
    misc_t = u_ref[:, U_MISC:U_MISC + LANES].T
    dtt_ref[...] = misc_t[MISC_DT:MISC_DT + 2 * SSD_HEADS, :]


def _inproj(x3, norm, w, layer, ts):
    bsz, seq, _ = x3.shape
    return pl.pallas_call(
        _inproj_kernel,
        grid=(bsz, seq // ts),
        in_specs=[
            pl.BlockSpec((None, ts, D_MODEL), lambda b, i: (b, i, 0)),
            _resident((None, 1, D_MODEL), lambda b, i: (layer, 0, 0)),
            _resident((None, D_MODEL, U_WIDTH), lambda b, i: (layer, 0, 0)),
        ],
        out_specs=[
            pl.BlockSpec((None, ts, U_WIDTH), lambda b, i: (b, i, 0)),
            pl.BlockSpec((None, 2 * SSD_HEADS, ts), lambda b, i: (b, 0, i)),
        ],
        out_shape=[
            jax.ShapeDtypeStruct((bsz, seq, U_WIDTH), F32),
            jax.ShapeDtypeStruct((bsz, 2 * SSD_HEADS, seq), F32),
        ],
        compiler_params=_cparams(("parallel", "parallel")),
        name="inproj",
    )(x3, norm, w)


def _outproj_kernel(x_ref, y0_ref, y1_ref, y2_ref, y3_ref, w0_ref, w1_ref, w2_ref, w3_ref, o_ref):
    acc = _dot(y0_ref[...], w0_ref[...])
    acc += _dot(y1_ref[...], w1_ref[...])
    acc += _dot(y2_ref[...], w2_ref[...])
    acc += _dot(y3_ref[...], w3_ref[...])
    o_ref[...] = x_ref[...] + acc


def _outproj(x, ys, w, layer, tm):
    t = x.shape[0]
    wd = D_MIX // len(ys)
    y_spec = pl.BlockSpec((tm, wd), lambda i: (i, 0))
    w_specs = [_resident((None, wd, D_MODEL), functools.partial(lambda k, i: (layer, k, 0), k))
               for k in range(len(ys))]
    return pl.pallas_call(
        _outproj_kernel,
        grid=(t // tm,),
        in_specs=[pl.BlockSpec((tm, D_MODEL), lambda i: (i, 0))] + [y_spec] * len(ys) + w_specs,
        out_specs=pl.BlockSpec((tm, D_MODEL), lambda i: (i, 0)),
        out_shape=jax.ShapeDtypeStruct((t, D_MODEL), F32),
        compiler_params=_cparams(("parallel",)),
        name="outproj",
    )(x, *ys, *([w] * len(ys)))


def _conv_silu_into(src_ref, pad_ref, w, dst_ref, seq):
    half = SSD_CONV // 2
    width = src_ref.shape[1]
    zeros = jnp.zeros((SUBLANES, width), F32)
    pad_ref[0:SUBLANES, :] = zeros
    pad_ref[SUBLANES + seq:2 * SUBLANES + seq, :] = zeros
    pad_ref[SUBLANES:SUBLANES + seq, :] = src_ref[...]
    rows = SSD_CHUNK
    for c in range(seq // rows):
        for j in range(width // LANES):
            ls = slice(j * LANES, (j + 1) * LANES)
            acc = jnp.broadcast_to(w[SSD_CONV:SSD_CONV + 1, ls], (rows, LANES))
            for k in range(SSD_CONV):
                r0 = SUBLANES - half + k + c * rows
                acc = acc + w[k:k + 1, ls] * pad_ref[r0:r0 + rows, ls]
            dst_ref[c * rows:(c + 1) * rows, ls] = _silu(acc).astype(dst_ref.dtype)


def _ssd_kernel(dt_lane, z_ref, xs_ref, b_ref, c_ref, misc_ref, dtt_ref, cwx_ref, cwb_ref, cwc_ref,
                hp_ref, hpt_ref, dx_ref, ng_ref, ef_ref, eb_ref, o_ref,
                xpad, bpad, cpad, xc, x2, bc, cc, bt, yacc, dts, dtat, ldtt, st):
    seq = o_ref.shape[0]
    ch = SSD_CHUNK
    nc = seq // ch
    hpg = SSD_HPG
    fl = dt_lane
    bl = dt_lane + hpg

    _conv_silu_into(xs_ref, xpad, cwx_ref[...], xc, seq)
    _conv_silu_into(b_ref, bpad, cwb_ref[...], bc, seq)
    _conv_silu_into(c_ref, cpad, cwc_ref[...], cc, seq)
    lane_lo = lax.broadcasted_iota(jnp.int32, (ch, SSD_GW), 1) % LANES < SSD_HEADDIM
    for c in range(nc):
        rs = slice(c * ch, (c + 1) * ch)
        bt[:, rs] = bc[rs, :].T.astype(BF16)
        xv = xc[rs, :]
        zero = jnp.zeros_like(xv)
        x2[c] = jnp.concatenate([jnp.where(lane_lo, xv, zero), jnp.where(lane_lo, zero, xv)],
                                axis=0).astype(BF16)

    hp = hp_ref[...]
    bias_row = hp[0:1, :]
    a_row = -jnp.exp(hp[1:2, :]) * hp[2:3, :]
    hpt = hpt_ref[...]
    dts[...] = _softplus(misc_ref[...] + bias_row)
    dtt = _softplus(dtt_ref[...] + hpt[:, 0:1])
    dtat[...] = dtt * (-jnp.exp(hpt[:, 1:2]))
    ldtt[...] = jnp.log(dtt)
    st[...] = jnp.zeros_like(st)

    row_i = lax.broadcasted_iota(jnp.int32, (ch, ch), 0)
    col_i = lax.broadcasted_iota(jnp.int32, (ch, ch), 1)
    tri_le = col_i <= row_i
    tri_ge = col_i >= row_i
    t_le = jnp.where(tri_le, 1.0, 0.0).astype(BF16)
    t_ge = jnp.where(tri_ge, 1.0, 0.0).astype(BF16)
    ef = ef_ref[...]
    eb = eb_ref[...]

    def expand3(a, b, tot, e):
        stacked = jnp.concatenate([a, b, jnp.broadcast_to(tot, (SUBLANES, LANES))], axis=0)
        out = _dot_x01(stacked, e, parts=2)
        return out[0:ch], out[ch:2 * ch], out[2 * ch:2 * ch + 1]

    def fwd_body(c, carry):
        r0 = pl.multiple_of(c * ch, ch)
        rows = pl.ds(r0, ch)
        cmat = cc[rows, :].astype(BF16)
        dt = dts[rows, :]
        dta = dt * a_row
        cum_f = _dot_01x(t_le, dta)
        tot = cum_f[ch - 1:ch, :]
        cum_b = tot - cum_f + dta
        dtat_c = dtat[:, rows]
        cumt_f = _dot_x01(dtat_c, t_ge)
        cumt_b = cumt_f[:, ch - 1:ch] - cumt_f + dtat_c
        ldt = ldtt[:, rows]
        rf = cumt_f - ldt
        rb = cumt_b - ldt
        cb = _dot_nt(cmat, bc[rows, :].astype(BF16))
        x2c = x2[c]
        for p in range(hpg // 2):
            ls = slice(p * LANES, (p + 1) * LANES)
            ms = []
            for r in (2 * p, 2 * p + 1):
                seg_f = cum_f[:, fl + r:fl + r + 1] - rf[r:r + 1, :]
                seg_b = cum_b[:, bl + r:bl + r + 1] - rb[hpg + r:hpg + r + 1, :]
                ms.append(cb * (jnp.exp(jnp.where(tri_le, seg_f, NEG_BIG))
                                + jnp.exp(jnp.where(tri_ge, seg_b, NEG_BIG))))
            m = jnp.concatenate(ms, axis=1).astype(BF16)
            yacc[rows, ls] = _dot(m, x2c[:, ls])
        prev = st[0]
        e_cum, e_w, e_dec = expand3(jnp.exp(cum_f), dt * jnp.exp(tot - cum_f), jnp.exp(tot), ef)
        yacc[rows, :] += _dot(cmat, prev.astype(BF16)) * e_cum
        s_new = _dot(bt[:, rows], (xc[rows, :] * e_w).astype(BF16))
        st[0] = prev * e_dec + s_new
        return carry

    lax.fori_loop(0, nc, fwd_body, 0, unroll=2)

    d_x = dx_ref[...]
    gain = ng_ref[...]

    def bwd_body(i, carry):
        c = nc - 1 - i
        r0 = pl.multiple_of(c * ch, ch)
        rows = pl.ds(r0, ch)
        x = xc[rows, :]
        cmat = cc[rows, :].astype(BF16)
        dt = dts[rows, :]
        dta = dt * a_row
        cum_f = _dot_01x(t_le, dta)
        tot = cum_f[ch - 1:ch, :]
        cum_b = tot - cum_f + dta
        prev = st[1]
        e_cum, e_w, e_dec = expand3(jnp.exp(cum_b), dt * jnp.exp(tot - cum_b), jnp.exp(tot), eb)
        y_off = _dot(cmat, prev.astype(BF16)) * e_cum
        s_new = _dot(bt[:, rows], (x * e_w).astype(BF16))
        st[1] = prev * e_dec + s_new
        y = (yacc[rows, :] + y_off + x * d_x) * _silu(z_ref[rows, :])
        o_ref[rows, :] = _rms(y, gain).astype(o_ref.dtype)
        return carry

    lax.fori_loop(0, nc, bwd_body, 0, unroll=2)


def _ssd_group(u3, dtt, cw_ext, hp, hpt, dx, ng, ef, eb, group):
    bsz, seq, _ = u3.shape
    gw = SSD_GW
    n = SSD_STATE
    dt_lane = MISC_DT + group * 2 * SSD_HPG
    blk = lambda w, idx: pl.BlockSpec((None, seq, w), lambda b: (b, 0, idx))
    full2 = lambda a: _resident(a.shape, lambda b: (0, 0))
    return pl.pallas_call(
        functools.partial(_ssd_kernel, dt_lane),
        grid=(bsz,),
        in_specs=[
            blk(gw, U_Z // gw + group),
            blk(gw, U_XS // gw + group),
            blk(n, U_B // n + group),
            blk(n, U_C // n + group),
            blk(LANES, U_MISC // LANES),
            pl.BlockSpec((None, 2 * SSD_HPG, seq), lambda b: (b, group, 0)),
            _resident((SUBLANES, gw), lambda b: (0, group)),
            _resident((SUBLANES, n), lambda b: (0, SSD_WIDTH // n + group)),
            _resident((SUBLANES, n), lambda b: (0, SSD_WIDTH // n + SSD_GROUPS + group)),
            full2(hp), full2(hpt), full2(dx),
            _resident((1, gw), lambda b: (0, group)),
            full2(ef), full2(eb),
        ],
        out_specs=pl.BlockSpec((None, seq, gw), lambda b: (b, 0, 0)),
        out_shape=jax.ShapeDtypeStruct((bsz, seq, gw), BF16),
        scratch_shapes=[
            pltpu.VMEM((seq + 2 * SUBLANES, gw), F32),
            pltpu.VMEM((seq + 2 * SUBLANES, n), F32),
            pltpu.VMEM((seq + 2 * SUBLANES, n), F32),
            pltpu.VMEM((seq, gw), F32),
            pltpu.VMEM((seq // SSD_CHUNK, 2 * SSD_CHUNK, gw), BF16),
            pltpu.VMEM((seq, n), F32),
            pltpu.VMEM((seq, n), F32),
            pltpu.VMEM((n, seq), BF16),
            pltpu.VMEM((seq, gw), F32),
            pltpu.VMEM((seq, LANES), F32),
            pltpu.VMEM((2 * SSD_HPG, seq), F32),
            pltpu.VMEM((2 * SSD_HPG, seq), F32),
            pltpu.VMEM((2, n, gw), F32),
        ],
        compiler_params=_cparams(("parallel",)),
        name=f"ssd_g{group}",
    )(u3, u3, u3, u3, u3, dtt, cw_ext, cw_ext, cw_ext, hp, hpt, dx, ng, ef, eb)


def _rope_kernel(pos_ref, invf_ref, cos_ref, sin_ref):
    ang = pos_ref[...].astype(F32) * invf_ref[...]
    cos_ref[...] = jnp.cos(ang)
    sin_ref[...] = jnp.sin(ang)


def _rope_tables(pos3, invf):
    bsz, seq, _ = pos3.shape
    out = jax.ShapeDtypeStruct((bsz, seq, LANES), F32)
    spec = pl.BlockSpec((None, seq, LANES), lambda b: (b, 0, 0))
    return pl.pallas_call(
        _rope_kernel,
        grid=(bsz,),
        in_specs=[pl.BlockSpec((None, seq, 1), lambda b: (b, 0, 0)),
                  pl.BlockSpec(invf.shape, lambda b: (0, 0))],
        out_specs=[spec, spec],
        out_shape=[out, out],
        compiler_params=_cparams(("parallel",)),
        name="rope_tables",
    )(pos3, invf)


def _mla_kernel(qlat_ref, kvlat_ref, misc_ref, cosk_ref, sink_ref, cosq_ref, sinq_ref, qn_ref, wq_ref,
                wqr_ref, kvn_ref, wk_ref, wv_ref, gq_ref, gk_ref, go_ref, pp_ref, pr_ref, o_ref,
                k_scr, v_scr):
    hs = LANES
    lane = lax.broadcasted_iota(jnp.int32, (1, hs), 1)
    ones_lane = (MLA_V, 0)
    v_mask = (lane < MLA_V, lane >= MLA_V)

    @pl.when(pl.program_id(1) == 0)
    def _():
        kvn = _rms(kvlat_ref[...], kvn_ref[...]).astype(BF16)
        hi, mid = _split(misc_ref[...], 2)
        pe = _dot(hi, pp_ref[...]) + _dot(mid, pp_ref[...])
        pe_rot = _dot(hi, pr_ref[...]) + _dot(mid, pr_ref[...])
        gk = gk_ref[...]
        kr = pe * gk[0:1, :] * cosk_ref[...] + pe_rot * gk[1:2, :] * sink_ref[...]
        ss_pe = jnp.sum(pe * pe, axis=-1, keepdims=True)
        for h in range(MLA_HEADS):
            ls = slice(h * hs, (h + 1) * hs)
            kp = _dot(kvn, wk_ref[:, ls])
            ss = jnp.sum(kp * kp, axis=-1, keepdims=True) + ss_pe
            rinv = lax.rsqrt(ss * (1.0 / MLA_QK) + RMS_EPS)
            k_scr[h] = ((kp * gk[0:1, :] + kr) * rinv).astype(BF16)
            ones_row = jnp.where(lane == ones_lane[h % 2], 1.0, 0.0)
            v_scr[h] = (_dot(kvn, wv_ref[:, ls]) + ones_row).astype(BF16)

    qn = _rms(qlat_ref[...], qn_ref[...]).astype(BF16)
    gq = gq_ref[...]
    gcos = gq[0:1, :] * cosq_ref[...]
    gsin = gq[1:2, :] * sinq_ref[...]
    scale = MLA_QK ** -0.5 * LOG2E
    pair = None
    for h in range(MLA_HEADS):
        ls = slice(h * hs, (h + 1) * hs)
        qp = _dot(qn, wq_ref[:, ls])
        qr = _dot(qn, wqr_ref[:, ls])
        ss = jnp.sum(qp * qp, axis=-1, keepdims=True)
        rinv = lax.rsqrt(ss * (1.0 / MLA_QK) + RMS_EPS) * scale
        q = ((qp * gcos + qr * gsin) * rinv).astype(BF16)
        s = _dot_nt(q, k_scr[h])
        p = jnp.exp2(s - jnp.max(s, axis=-1, keepdims=True))
        o = _dot(p.astype(BF16), v_scr[h])
        ol = ones_lane[h % 2]
        o = jnp.where(v_mask[h % 2], o, 0.0) / o[:, ol:ol + 1]
        ms = jnp.sum(o * o, axis=-1, keepdims=True) * (1.0 / MLA_V)
        on = o * lax.rsqrt(ms + RMS_EPS)
        if h % 2 == 0:
            pair = on
        else:
            ps = slice((h // 2) * hs, (h // 2 + 1) * hs)
            o_ref[:, ps] = ((pair + on) * go_ref[:, ps]).astype(o_ref.dtype)


def _mla(u3, cos, sin, qn, wq, wqr, kvn, wk, wv, gq, gk, go, pp, pr, tq):
    bsz, seq, _ = u3.shape
    full2 = lambda a: _resident(a.shape, lambda b, i: (0, 0))
    seq_blk = lambda w, idx: pl.BlockSpec((None, seq, w), lambda b, i: (b, 0, idx))
    q_blk = lambda w, idx: pl.BlockSpec((None, tq, w), lambda b, i: (b, i, idx))
    return pl.pallas_call(
        _mla_kernel,
        grid=(bsz, seq // tq),
        in_specs=[
            q_blk(MLA_Q_LORA, U_QLAT // MLA_Q_LORA),
            seq_blk(MLA_KV_LORA, U_KVLAT // MLA_KV_LORA),
            seq_blk(LANES, U_MISC // LANES),
            seq_blk(LANES, 0), seq_blk(LANES, 0),
            q_blk(LANES, 0), q_blk(LANES, 0),
            full2(qn), full2(wq), full2(wqr), full2(kvn), full2(wk), full2(wv),
            full2(gq), full2(gk), full2(go), full2(pp), full2(pr),
        ],
        out_specs=pl.BlockSpec((None, tq, MLA_WIDTH), lambda b, i: (b, i, 0)),
        out_shape=jax.ShapeDtypeStruct((bsz, seq, MLA_WIDTH), BF16),
        scratch_shapes=[
            pltpu.VMEM((MLA_HEADS, seq, LANES), BF16),
            pltpu.VMEM((MLA_HEADS, seq, LANES), BF16),
        ],
        compiler_params=_cparams(("parallel", "arbitrary")),
        name="mla",
    )(u3, u3, u3, cos, sin, cos, sin, qn, wq, wqr, kvn, wk, wv, gq, gk, go, pp, pr)


def _gconv_kernel(h_ref, b_ref, c_ref, w_ref, g_ref, gm_ref, o_ref, pad):
    seq = o_ref.shape[0]
    half = CONV_K // 2
    zeros = jnp.zeros((SUBLANES, LANES), F32)
    pad[0:SUBLANES, :] = zeros
    pad[SUBLANES + seq:2 * SUBLANES + seq, :] = zeros
    pad[SUBLANES:SUBLANES + seq, :] = c_ref[...] * h_ref[...]
    w = w_ref[...]
    gm = gm_ref[...]
    rows = 256
    gsz = CONV_WIDTH // CONV_GROUPS
    for c in range(seq // rows):
        acc = jnp.zeros((rows, LANES), F32)
        for k in range(CONV_K):
            r0 = SUBLANES - half + k + c * rows
            acc = acc + w[k:k + 1, :] * pad[r0:r0 + rows, :]
        rs = slice(c * rows, (c + 1) * rows)
        y = acc * b_ref[rs, :]
        ss = _dot_x01(y * y, gm, parts=2)
        o_ref[rs, :] = (y * lax.rsqrt(ss * (1.0 / gsz) + RMS_EPS) * g_ref[...]).astype(o_ref.dtype)


def _gconv(u3, w_ext, g, gm):
    bsz, seq, _ = u3.shape
    nj = CONV_WIDTH // LANES
    blk = lambda off: pl.BlockSpec((None, seq, LANES), lambda b, j: (b, 0, off // LANES + j))
    return pl.pallas_call(
        _gconv_kernel,
        grid=(bsz, nj),
        in_specs=[
            blk(U_CH), blk(U_CB), blk(U_CC),
            pl.BlockSpec((SUBLANES, LANES), lambda b, j: (0, j)),
            pl.BlockSpec((1, LANES), lambda b, j: (0, j)),
            pl.BlockSpec(gm.shape, lambda b, j: (0, 0)),
        ],
        out_specs=pl.BlockSpec((None, seq, LANES), lambda b, j: (b, 0, j)),
        out_shape=jax.ShapeDtypeStruct((bsz, seq, CONV_WIDTH), BF16),
        scratch_shapes=[pltpu.VMEM((seq + 2 * SUBLANES, LANES), F32)],
        compiler_params=_cparams(("parallel", "parallel")),
        name="gconv",
    )(u3, u3, u3, w_ext, g, gm)


def _rot_half_perm(n):
    h = n // 2
    idx = np.array([(j + h) % n for j in range(n)])
    sign = np.array([-1.0 if j < h else 1.0 for j in range(n)], np.float32)
    return idx, sign


def _prep_w_in(w_in):
    depth = w_in.shape[0]
    o_dt = 2 * SSD_WIDTH + 2 * SSD_GROUPS * SSD_STATE
    o_q = o_dt + 2 * SSD_HEADS
    o_pe = o_q + MLA_Q_LORA + MLA_KV_LORA
    o_ch = o_pe + MLA_ROPE
    w = w_in.astype(BF16)
    dt_cols = [w[:, :, o_dt + d * SSD_HEADS + g * SSD_HPG:o_dt + d * SSD_HEADS + (g + 1) * SSD_HPG]
               for g in range(SSD_GROUPS) for d in range(2)]
    pad = jnp.zeros((depth, D_MODEL, U_CH - U_MISC - MLA_ROPE - 2 * SSD_HEADS), BF16)
    return jnp.concatenate([w[:, :, :o_dt], w[:, :, o_q:o_pe], w[:, :, o_pe:o_ch]] + dt_cols
                           + [pad, w[:, :, o_ch:]], axis=-1)


def _prep_mla(w_uq, w_ukv, q_head_norm, k_head_norm):
    depth = w_uq.shape[0]
    ridx, rsign = _rot_half_perm(MLA_ROPE)
    wq3 = w_uq.astype(BF16).reshape(depth, MLA_Q_LORA, MLA_HEADS, MLA_QK)
    zq = jnp.zeros((depth, MLA_Q_LORA, MLA_HEADS, LANES - MLA_QK), BF16)
    wq = jnp.concatenate([wq3, zq], axis=-1).reshape(depth, MLA_Q_LORA, MLA_HEADS * LANES)
    half = MLA_ROPE // 2
    rot = jnp.concatenate([-wq3[..., MLA_NOPE + half:], wq3[..., MLA_NOPE:MLA_NOPE + half]], axis=-1)
    zn = jnp.zeros((depth, MLA_Q_LORA, MLA_HEADS, MLA_NOPE), BF16)
    wqr = jnp.concatenate([zn, rot, zq], axis=-1).reshape(depth, MLA_Q_LORA, MLA_HEADS * LANES)
    wkv3 = w_ukv.astype(BF16).reshape(depth, MLA_KV_LORA, MLA_HEADS, MLA_NOPE + MLA_V)
    zk = jnp.zeros((depth, MLA_KV_LORA, MLA_HEADS, LANES - MLA_NOPE), BF16)
    wk = jnp.concatenate([wkv3[..., :MLA_NOPE], zk], axis=-1).reshape(depth, MLA_KV_LORA, MLA_HEADS * LANES)
    v4 = wkv3[..., MLA_NOPE:].reshape(depth, MLA_KV_LORA, MLA_HEADS // 2, 2, MLA_V)
    zv = jnp.zeros_like(v4[:, :, :, 0])
    wv = jnp.stack([jnp.concatenate([v4[:, :, :, 0], zv], axis=-1),
                    jnp.concatenate([zv, v4[:, :, :, 1]], axis=-1)], axis=3)
    wv = wv.reshape(depth, MLA_KV_LORA, MLA_HEADS * LANES)

    def gains(g):
        zt = jnp.zeros((depth, LANES - MLA_QK), F32)
        plain = jnp.concatenate([g, zt], axis=-1)
        g_rope = g[:, MLA_NOPE:]
        rot_g = jnp.concatenate([jnp.zeros((depth, MLA_NOPE), F32), g_rope[:, half:], g_rope[:, :half], zt],
                                axis=-1)
        rest = jnp.zeros((depth, SUBLANES - 2, LANES), F32)
        return jnp.concatenate([plain[:, None], rot_g[:, None], rest], axis=1)

    return wq, wqr, wk, wv, gains(q_head_norm), gains(k_head_norm)


def _mla_consts():
    ridx, rsign = _rot_half_perm(MLA_ROPE)
    half = MLA_ROPE // 2
    inv = ROPE_BASE ** (-jnp.arange(half, dtype=F32) / half)
    invf = jnp.concatenate([jnp.zeros((MLA_NOPE,), F32), inv, inv, jnp.zeros((LANES - MLA_QK,), F32)])[None, :]
    src = np.arange(MLA_ROPE)
    pp = np.zeros((LANES, LANES), np.float32)
    pp[MISC_KPE + src, MLA_NOPE + src] = 1.0
    pr = np.zeros((LANES, LANES), np.float32)
    pr[MISC_KPE + ridx, MLA_NOPE + src] = rsign
    return invf, jnp.asarray(pp, BF16), jnp.asarray(pr, BF16)


def _prep_ssd(ssd_conv_w, ssd_conv_b, ssd_dt_bias, ssd_a_log, ssd_d):
    depth = ssd_conv_w.shape[0]
    cw_ext = jnp.concatenate([ssd_conv_w, ssd_conv_b[:, None, :],
                              jnp.zeros((depth, SUBLANES - SSD_CONV - 1, ssd_conv_w.shape[-1]), F32)], axis=1)

    def per_group(p):
        return p.reshape(depth, 2, SSD_GROUPS, SSD_HPG).transpose(0, 2, 1, 3).reshape(depth, SSD_GROUPS, 2 * SSD_HPG)

    bias_g, alog_g = per_group(ssd_dt_bias), per_group(ssd_a_log)
    n16 = 2 * SSD_HPG
    hps = []
    for g in range(SSD_GROUPS):
        lo = MISC_DT + g * n16
        lane_pad = ((0, 0), (lo, LANES - lo - n16))
        rows = [jnp.pad(bias_g[:, g], lane_pad), jnp.pad(alog_g[:, g], lane_pad),
                jnp.pad(jnp.ones((depth, n16), F32), lane_pad)]
        rows += [jnp.zeros((depth, LANES), F32)] * (SUBLANES - len(rows))
        hps.append(jnp.stack(rows, axis=1))
    hp = jnp.stack(hps, axis=1)
    hpt = jnp.stack([bias_g, alog_g] + [jnp.zeros_like(bias_g)] * (SUBLANES - 2), axis=-1)
    dx = jnp.repeat(ssd_d.reshape(depth, SSD_GROUPS, SSD_HPG), SSD_HEADDIM, axis=-1)[:, :, None, :]
    return cw_ext, hp, hpt, dx


def _ssd_expanders():
    out = []
    for g in range(SSD_GROUPS):
        per_dir = []
        for d in range(2):
            lo = MISC_DT + g * 2 * SSD_HPG + d * SSD_HPG
            e = np.zeros((LANES, SSD_GW), np.float32)
            e[lo + np.arange(SSD_GW) // SSD_HEADDIM, np.arange(SSD_GW)] = 1.0
            per_dir.append(jnp.asarray(e, BF16))
        out.append(per_dir)
    return out


def _prep_mixers(positions, ssd_conv_w, ssd_conv_b, ssd_dt_bias, ssd_a_log, ssd_d, ssd_norm,
                 mla_q_norm, mla_w_uq, mla_kv_norm, mla_w_ukv, mla_q_head_norm, mla_k_head_norm,
                 mla_out_norm, conv_w, conv_out_norm):
    depth = conv_w.shape[0]
    bsz, seq = positions.shape
    gsz = CONV_WIDTH // CONV_GROUPS
    li = np.arange(LANES)
    invf, pp, pr = _mla_consts()
    cos, sin = _rope_tables(positions.reshape(bsz, seq, 1), invf)
    return dict(
        mla_w=_prep_mla(mla_w_uq, mla_w_ukv, mla_q_head_norm, mla_k_head_norm),
        mla_c=(cos, sin, pp, pr),
        mla_n=(mla_q_norm, mla_kv_norm, mla_out_norm),
        ssd=_prep_ssd(ssd_conv_w, ssd_conv_b, ssd_dt_bias, ssd_a_log, ssd_d),
        ssd_norm=ssd_norm,
        expanders=_ssd_expanders(),
        gc_w=jnp.concatenate([conv_w, jnp.zeros((depth, SUBLANES - CONV_K, CONV_WIDTH), F32)], axis=1),
        gc_g=conv_out_norm,
        gm=jnp.asarray((li[:, None] // gsz == li[None, :] // gsz).astype(np.float32), BF16),
    )


def _mixers(u3, dtt, mp, l):
    bsz, seq, _ = u3.shape
    t = bsz * seq
    tq = 512 if seq % 512 == 0 else SSD_CHUNK
    cw_ext, hp, hpt, dx = mp["ssd"]
    ys = [_ssd_group(u3, dtt, cw_ext[l], hp[l, g], hpt[l, g], dx[l, g], mp["ssd_norm"][l][None, :],
                     mp["expanders"][g][0], mp["expanders"][g][1], g).reshape(t, SSD_GW)
          for g in range(SSD_GROUPS)]
    wq, wqr, wk, wv, gq, gk = mp["mla_w"]
    cos, sin, pp, pr = mp["mla_c"]
    qn, kvn, go = mp["mla_n"]
    y_mla = _mla(u3, cos, sin, qn[l][None, :], wq[l], wqr[l], kvn[l][None, :], wk[l], wv[l],
                 gq[l], gk[l], go[l][None, :], pp, pr, tq).reshape(t, MLA_WIDTH)
    y_conv = _gconv(u3, mp["gc_w"][l], mp["gc_g"][l][None, :], mp["gm"]).reshape(t, CONV_WIDTH)
    return ys + [y_mla, y_conv]


def kernel(x, positions, ffn1_norm, ffn1_w_gate, ffn1_w_up, ffn1_w_down, mix_norm, w_in, ssd_conv_w, ssd_conv_b, ssd_dt_bias, ssd_a_log, ssd_d, ssd_norm, mla_q_norm, mla_w_uq, mla_kv_norm, mla_w_ukv, mla_q_head_norm, mla_k_head_norm, mla_out_norm, conv_w, conv_out_norm, w_out, ffn2_norm, ffn2_w_gate, ffn2_w_up, ffn2_w_down):
    bsz, seq, _ = x.shape
    depth = w_in.shape[0]
    t = bsz * seq
    assert seq % 256 == 0
    tm = 1024 if t % 1024 == 0 else 256
    ts = 512 if seq % 512 == 0 else 256

    row = lambda a: a[:, None, :]
    f1 = (row(ffn1_norm), ffn1_w_gate.astype(BF16), ffn1_w_up.astype(BF16), (0.5 * ffn1_w_down).astype(BF16))
    f2 = (row(ffn2_norm), ffn2_w_gate.astype(BF16), ffn2_w_up.astype(BF16), (0.5 * ffn2_w_down).astype(BF16))
    w_in_p = _prep_w_in(w_in)
    w_out_b = w_out.astype(BF16)
    mp = _prep_mixers(positions, ssd_conv_w, ssd_conv_b, ssd_dt_bias, ssd_a_log, ssd_d, ssd_norm,
                      mla_q_norm, mla_w_uq, mla_kv_norm, mla_w_ukv, mla_q_head_norm, mla_k_head_norm,
                      mla_out_norm, conv_w, conv_out_norm)

    xt = x.reshape(t, D_MODEL)
    for l in range(depth):
        xt = _ffn(xt, *f1, l, tm)
        u3, dtt = _inproj(xt.reshape(bsz, seq, D_MODEL), row(mix_norm), w_in_p, l, ts)
        ys = _mixers(u3, dtt, mp, l)
        xt = _outproj(xt, ys, w_out_b, l, tm)
        xt = _ffn(xt, *f2, l, tm)
    return xt.reshape(bsz, seq, D_MODEL)
```

```python
import functools
import math

import numpy as np
import jax
import jax.numpy as jnp
from jax import lax
from jax.experimental import pallas as pl
from jax.experimental.pallas import tpu as pltpu

F32 = jnp.float32
BF16 = jnp.bfloat16

D_MODEL = 1024
D_FF = 2816
RMS_EPS = 1e-6

SSD_WIDTH = 1024
SSD_HEADDIM = 64
SSD_HEADS = 16
SSD_GROUPS = 2
SSD_HPG = 8
SSD_STATE = 128
SSD_CONV = 5
SSD_CHUNK = 128
SSD_GW = SSD_HPG * SSD_HEADDIM

MLA_HEADS = 8
MLA_Q_LORA = 256
MLA_KV_LORA = 128
MLA_NOPE = 64
MLA_ROPE = 32
MLA_QK = 96
MLA_V = 64
MLA_WIDTH = 512
MLA_VROWS = 80
ROPE_BASE = 10000.0

CONV_WIDTH = 512
CONV_GROUPS = 8
CONV_K = 3

D_MIX = 2048
LANES = 128
SUBLANES = 8

U_Z = 0
U_XS = 1024
U_B = 2048
U_C = 2304
U_QLAT = 2560
U_KVLAT = 2816
U_MISC = 2944
U_CH = 3072
U_CB = 3584
U_CC = 4096
U_WIDTH = 4608
MISC_KPE = 0
MISC_DT = 32

NEG_BIG = -1e30
LOG2E = math.log2(math.e)
VMEM_LIMIT = 56 * 1024 * 1024
FFN_TF = 256
P1_GROUP = 2


def _cparams(sem):
    return pltpu.CompilerParams(dimension_semantics=sem, vmem_limit_bytes=VMEM_LIMIT)


def _resident(block_shape, index_map):
    return pl.BlockSpec(block_shape, index_map, pipeline_mode=pl.Buffered(1))


def _dot(a, b):
    return jnp.dot(a, b, preferred_element_type=F32)


def _dot_nt(a, b):
    return lax.dot_general(a, b, (((1,), (1,)), ((), ())), preferred_element_type=F32)


def _split(v, parts):
    out = []
    r = v
    for i in range(parts):
        t = r.astype(BF16)
        out.append(t)
        if i + 1 < parts:
            r = r - t.astype(F32)
    return out


def _dot_x01(v, m01, parts=3):
    acc = None
    for t in _split(v, parts):
        d = _dot(t, m01)
        acc = d if acc is None else acc + d
    return acc


def _dot_01x(m01, v, parts=3):
    acc = None
    for t in _split(v, parts):
        d = _dot(m01, t)
        acc = d if acc is None else acc + d
    return acc


def _silu(x):
    return x * jax.nn.sigmoid(x)


def _softplus(x):
    return jnp.maximum(x, 0.0) + jnp.log1p(jnp.exp(-jnp.abs(x)))


def _rms(x, g, n=None):
    n = x.shape[-1] if n is None else n
    ms = jnp.sum(x * x, axis=-1, keepdims=True) * (1.0 / n)
    return x * lax.rsqrt(ms + RMS_EPS) * g


def _ffn_kernel(x_ref, g_ref, wg_ref, wu_ref, wd_ref, o_ref, xn_ref):
    x = x_ref[...]
    xn_ref[...] = _rms(x, g_ref[...]).astype(BF16)
    o_ref[...] = x

    def body(f, carry):
        cs = pl.ds(pl.multiple_of(f * FFN_TF, FFN_TF), FFN_TF)
        xn = xn_ref[...]
        h = _silu(_dot(xn, wg_ref[:, cs])) * _dot(xn, wu_ref[:, cs])
        o_ref[...] += _dot(h.astype(BF16), wd_ref[cs, :])
        return carry

    lax.fori_loop(0, D_FF // FFN_TF, body, 0, unroll=True)


def _ffn(x, norm, wg, wu, wd_half, layer, tm):
    t = x.shape[0]
    return pl.pallas_call(
        _ffn_kernel,
        grid=(t // tm,),
        in_specs=[
            pl.BlockSpec((tm, D_MODEL), lambda i: (i, 0)),
            _resident((None, 1, D_MODEL), lambda i: (layer, 0, 0)),
            _resident((None, D_MODEL, D_FF), lambda i: (layer, 0, 0)),
            _resident((None, D_MODEL, D_FF), lambda i: (layer, 0, 0)),
            _resident((None, D_FF, D_MODEL), lambda i: (layer, 0, 0)),
        ],
        out_specs=pl.BlockSpec((tm, D_MODEL), lambda i: (i, 0)),
        out_shape=jax.ShapeDtypeStruct((t, D_MODEL), F32),
        scratch_shapes=[pltpu.VMEM((tm, D_MODEL), BF16)],
        compiler_params=_cparams(("parallel",)),
        name="ffn",
    )(x, norm, wg, wu, wd_half)


def _inproj_kernel(x_ref, g_ref, wa_ref, wm_ref, wc_ref, u_ref, dtt_ref):
    xn = _rms(x_ref[...], g_ref[...]).astype(BF16)
    u_ref[:, :U_QLAT] = _dot(xn, wa_ref[...])
    u_ref[:, U_QLAT:U_CH] = _dot(xn, wm_ref[...])
    u_ref[:, U_CH:] = _dot(xn, wc_ref[...])
    misc_t = u_ref[:, U_MISC:U_MISC + LANES].T
    dtt_ref[...] = misc_t[MISC_DT:MISC_DT + 2 * SSD_HEADS, :]


def _inproj(x3, norm, w_parts, layer, ts):
    bsz, seq, _ = x3.shape
    w_specs = [_resident((None, D_MODEL, w.shape[-1]), lambda b, i: (layer, 0, 0)) for w in w_parts]
    return pl.pallas_call(
        _inproj_kernel,
        grid=(bsz, seq // ts),
        in_specs=[
            pl.BlockSpec((None, ts, D_MODEL), lambda b, i: (b, i, 0)),
            _resident((None, 1, D_MODEL), lambda b, i: (layer, 0, 0)),
        ] + w_specs,
        out_specs=[
            pl.BlockSpec((None, ts, U_WIDTH), lambda b, i: (b, i, 0)),
            pl.BlockSpec((None, 2 * SSD_HEADS, ts), lambda b, i: (b, 0, i)),
        ],
        out_shape=[
            jax.ShapeDtypeStruct((bsz, seq, U_WIDTH), F32),
            jax.ShapeDtypeStruct((bsz, 2 * SSD_HEADS, seq), F32),
        ],
        compiler_params=_cparams(("parallel", "parallel")),
        name="inproj",
    )(x3, norm, *w_parts)


def _outproj_kernel(x_ref, y0_ref, y1_ref, y2_ref, y3_ref, w0_ref, w1_ref, w2_ref, w3_ref, o_ref):
    acc = _dot(y0_ref[...], w0_ref[...])
    acc += _dot(y1_ref[...], w1_ref[...])
    acc += _dot(y2_ref[...], w2_ref[...])
    acc += _dot(y3_ref[...], w3_ref[...])
    o_ref[...] = x_ref[...] + acc


def _outproj(x, ys, w, layer, tm):
    t = x.shape[0]
    wd = D_MIX // len(ys)
    y_spec = pl.BlockSpec((tm, wd), lambda i: (i, 0))
    w_specs = [_resident((None, wd, D_MODEL), functools.partial(lambda k, i: (layer, k, 0), k))
               for k in range(len(ys))]
    return pl.pallas_call(
        _outproj_kernel,
        grid=(t // tm,),
        in_specs=[pl.BlockSpec((tm, D_MODEL), lambda i: (i, 0))] + [y_spec] * len(ys) + w_specs,
        out_specs=pl.BlockSpec((tm, D_MODEL), lambda i: (i, 0)),
        out_shape=jax.ShapeDtypeStruct((t, D_MODEL), F32),
        compiler_params=_cparams(("parallel",)),
        name="outproj",
    )(x, *ys, *([w] * len(ys)))


def _conv_silu_block(src_ref, w, c, ls, seq):
    rows = SSD_CHUNK
    half = SSD_CONV // 2
    row_id = lax.broadcasted_iota(jnp.int32, (rows, LANES), 0)
    acc = jnp.broadcast_to(w[SSD_CONV:SSD_CONV + 1, ls], (rows, LANES))
    for k in range(SSD_CONV):
        off = k - half
        lo = c * rows + off
        if lo >= 0 and lo + rows <= seq:
            v = src_ref[lo:lo + rows, ls]
        else:
            v = pltpu.roll(src_ref[c * rows:(c + 1) * rows, ls], (-off) % rows, axis=0)
            v = jnp.where(row_id < -off if off < 0 else row_id >= rows - off, 0.0, v)
        acc = acc + w[k:k + 1, ls] * v
    return _silu(acc)


def _ssd_kernel(dt_lane, z_ref, xs_ref, b_ref, c_ref, misc_ref, dtt_ref, cwx_ref, cwb_ref, cwc_ref,
                hp_ref, hpt_ref, dx_ref, ng_ref, ef_ref, eb_ref, sel_ref, o_ref,
                xc, bc, cc, bt, yacc, dts, dtat, ldtt, sf, sb, ecf, ecb, decf, decb, st):
    seq = o_ref.shape[0]
    ch = SSD_CHUNK
    nc = seq // ch
    hpg = SSD_HPG
    fl = dt_lane
    bl = dt_lane + hpg

    wx, wb, wc = cwx_ref[...], cwb_ref[...], cwc_ref[...]
    one_block = slice(0, LANES)
    for c in range(nc):
        rs = slice(c * ch, (c + 1) * ch)
        for j in range(SSD_GW // LANES):
            ls = slice(j * LANES, (j + 1) * LANES)
            xc[rs, ls] = _conv_silu_block(xs_ref, wx, c, ls, seq)
        bv = _conv_silu_block(b_ref, wb, c, one_block, seq)
        bc[rs, :] = bv.astype(BF16)
        bt[:, rs] = bv.T.astype(BF16)
        cc[rs, :] = _conv_silu_block(c_ref, wc, c, one_block, seq).astype(BF16)

    hp = hp_ref[...]
    bias_row = hp[0:1, :]
    a_row = -jnp.exp(hp[1:2, :]) * hp[2:3, :] * LOG2E
    hpt = hpt_ref[...]
    dts[...] = _softplus(misc_ref[...] + bias_row)
    dtt = _softplus(dtt_ref[...] + hpt[:, 0:1])
    dtat[...] = dtt * (-jnp.exp(hpt[:, 1:2]) * LOG2E)
    ldtt[...] = jnp.log(dtt) * LOG2E
    st[...] = jnp.zeros_like(st)
    lane_lo = lax.broadcasted_iota(jnp.int32, (ch, SSD_GW), 1) % LANES < SSD_HEADDIM

    row_i = lax.broadcasted_iota(jnp.int32, (ch, ch), 0)
    col_i = lax.broadcasted_iota(jnp.int32, (ch, ch), 1)
    tri_le = col_i <= row_i
    tri_ge = col_i >= row_i
    t_le = jnp.where(tri_le, 1.0, 0.0).astype(BF16)
    t_ge = jnp.where(tri_ge, 1.0, 0.0).astype(BF16)
    ef = ef_ref[...]
    eb = eb_ref[...]
    sel = sel_ref[...]
    lane_i = lax.broadcasted_iota(jnp.int32, (1, LANES), 1)
    fwd_lane = (lane_i >= fl) & (lane_i < bl)

    def chunk_rows(c):
        return pl.ds(pl.multiple_of(c * ch, ch), ch)

    def pass1(i, carry):
        cs = [i * P1_GROUP + k for k in range(P1_GROUP)]
        rws = [chunk_rows(c) for c in cs]
        dts_ = [dts[r, :] for r in rws]
        dtas = [dt * a_row for dt in dts_]
        cum_fs = [_dot_01x(t_le, dta) for dta in dtas]
        cumt_fs = [_dot_x01(dtat[:, r], t_ge) for r in rws]
        cbs = [_dot_nt(cc[r, :], bc[r, :]) for r in rws]
        tots = [cf[ch - 1:ch, :] for cf in cum_fs]
        cum_bs = [tot - cf + dta for tot, cf, dta in zip(tots, cum_fs, dtas)]
        colbs = [_dot_x01(jnp.where(fwd_lane, cf, cbw), sel, parts=2) for cf, cbw in zip(cum_fs, cum_bs)]
        spreads = []
        for dt, tot, cf, cbw in zip(dts_, tots, cum_fs, cum_bs):
            spreads.append([_dot(jnp.concatenate([jnp.exp2(cum), dt * jnp.exp2(tot - cum)], axis=0).astype(BF16), e)
                            for cum, e in ((cf, ef), (cbw, eb))])
        for k, (c, r) in enumerate(zip(cs, rws)):
            x = xc[r, :]
            bt_c = bt[:, r]
            dec8 = jnp.broadcast_to(jnp.exp2(tots[k]), (SUBLANES, LANES))
            for spread, e, ec_ref, s_ref, dec_ref in ((spreads[k][0], ef, ecf, sf, decf),
                                                      (spreads[k][1], eb, ecb, sb, decb)):
                ec_ref[c] = spread[0:ch].astype(BF16)
                s_ref[c] = _dot(bt_c, (x * spread[ch:]).astype(BF16)).astype(BF16)
                dec_ref[c] = _dot_x01(dec8, e, parts=2)
        for k, (c, r) in enumerate(zip(cs, rws)):
            x = xc[r, :]
            dtat_c = dtat[:, r]
            ldt = ldtt[:, r]
            rf = cumt_fs[k] - ldt
            rb = cumt_fs[k][:, ch - 1:ch] - cumt_fs[k] + dtat_c - ldt
            zero = jnp.zeros_like(x)
            x2 = jnp.concatenate([jnp.where(lane_lo, x, zero), jnp.where(lane_lo, zero, x)],
                                 axis=0).astype(BF16)
            cb, colb = cbs[k], colbs[k]
            for p in range(hpg // 2):
                ls = slice(p * LANES, (p + 1) * LANES)
                ms = []
                for h in (2 * p, 2 * p + 1):
                    seg_f = colb[:, h * LANES:(h + 1) * LANES] - rf[h:h + 1, :]
                    seg_b = colb[:, (hpg + h) * LANES:(hpg + h + 1) * LANES] - rb[hpg + h:hpg + h + 1, :]
                    ms.append(cb * (jnp.exp2(jnp.where(tri_le, seg_f, NEG_BIG))
                                    + jnp.exp2(jnp.where(tri_ge, seg_b, NEG_BIG))))
                m = jnp.concatenate(ms, axis=1).astype(BF16)
                yacc[r, ls] = _dot(m, x2[:, ls])
        return carry

    lax.fori_loop(0, nc // P1_GROUP, pass1, 0)

    def pass2(i, carry):
        for d, (c, ec_ref, s_ref, dec_ref) in enumerate(((i, ecf, sf, decf), (nc - 1 - i, ecb, sb, decb))):
            rows = chunk_rows(c)
            prev = st[d]
            yacc[rows, :] += _dot(cc[rows, :], prev.astype(BF16)) * ec_ref[c].astype(F32)
            st[d] = prev * dec_ref[c][0:1, :] + s_ref[c].astype(F32)
        return carry

    lax.fori_loop(0, nc, pass2, 0)

    d_x = dx_ref[...]
    gain = ng_ref[...]

    def pass3(c, carry):
        rows = chunk_rows(c)
        y = (yacc[rows, :] + xc[rows, :] * d_x) * _silu(z_ref[rows, :])
        o_ref[rows, :] = _rms(y, gain).astype(o_ref.dtype)
        return carry

    lax.fori_loop(0, nc, pass3, 0, unroll=2)


def _ssd_group(u3, dtt, cw_ext, hp, hpt, dx, ng, ef, eb, sel, group):
    bsz, seq, _ = u3.shape
    gw = SSD_GW
    n = SSD_STATE
    dt_lane = MISC_DT + group * 2 * SSD_HPG
    nchunk = seq // SSD_CHUNK
    blk = lambda w, idx: pl.BlockSpec((None, seq, w), lambda b: (b, 0, idx))
    full2 = lambda a: _resident(a.shape, lambda b: (0, 0))
    return pl.pallas_call(
        functools.partial(_ssd_kernel, dt_lane),
        grid=(bsz,),
        in_specs=[
            blk(gw, U_Z // gw + group),
            blk(gw, U_XS // gw + group),
            blk(n, U_B // n + group),
            blk(n, U_C // n + group),
            blk(LANES, U_MISC // LANES),
            pl.BlockSpec((None, 2 * SSD_HPG, seq), lambda b: (b, group, 0)),
            _resident((SUBLANES, gw), lambda b: (0, group)),
            _resident((SUBLANES, n), lambda b: (0, SSD_WIDTH // n + group)),
            _resident((SUBLANES, n), lambda b: (0, SSD_WIDTH // n + SSD_GROUPS + group)),
            full2(hp), full2(hpt), full2(dx),
            _resident((1, gw), lambda b: (0, group)),
            full2(ef), full2(eb), full2(sel),
        ],
        out_specs=pl.BlockSpec((None, seq, gw), lambda b: (b, 0, 0)),
        out_shape=jax.ShapeDtypeStruct((bsz, seq, gw), BF16),
        scratch_shapes=[
            pltpu.VMEM((seq, gw), F32),
            pltpu.VMEM((seq, n), BF16),
            pltpu.VMEM((seq, n), BF16),
            pltpu.VMEM((n, seq), BF16),
            pltpu.VMEM((seq, gw), F32),
            pltpu.VMEM((seq, LANES), F32),
            pltpu.VMEM((2 * SSD_HPG, seq), F32),
            pltpu.VMEM((2 * SSD_HPG, seq), F32),
            pltpu.VMEM((nchunk, n, gw), BF16),
            pltpu.VMEM((nchunk, n, gw), BF16),
            pltpu.VMEM((nchunk, SSD_CHUNK, gw), BF16),
            pltpu.VMEM((nchunk, SSD_CHUNK, gw), BF16),
            pltpu.VMEM((nchunk, SUBLANES, gw), F32),
            pltpu.VMEM((nchunk, SUBLANES, gw), F32),
            pltpu.VMEM((2, n, gw), F32),
        ],
        compiler_params=_cparams(("parallel",)),
        name=f"ssd_g{group}",
    )(u3, u3, u3, u3, u3, dtt, cw_ext, cw_ext, cw_ext, hp, hpt, dx, ng, ef, eb, sel)


def _rope_kernel(pos_ref, invf_ref, cos_ref, sin_ref):
    ang = pos_ref[...].astype(F32) * invf_ref[...]
    cos_ref[...] = jnp.cos(ang)
    sin_ref[...] = jnp.sin(ang)


def _rope_tables(pos3, invf):
    bsz, seq, _ = pos3.shape
    out = jax.ShapeDtypeStruct((bsz, seq, LANES), F32)
    spec = pl.BlockSpec((None, seq, LANES), lambda b: (b, 0, 0))
    return pl.pallas_call(
        _rope_kernel,
        grid=(bsz,),
        in_specs=[pl.BlockSpec((None, seq, 1), lambda b: (b, 0, 0)),
                  pl.BlockSpec(invf.shape, lambda b: (0, 0))],
        out_specs=[spec, spec],
        out_shape=[out, out],
        compiler_params=_cparams(("parallel",)),
        name="rope_tables",
    )(pos3, invf)


def _mla_kernel(qlat_ref, kvlat_ref, misc_ref, cosk_ref, sink_ref, cosq_ref, sinq_ref, qn_ref, wq_ref,
                wqr_ref, kvn_ref, wk_ref, wvt_ref, gq_ref, gk_ref, go_ref, pp_ref, pr_ref, o_ref,
                k_scr, vt_scr):
    hs = LANES
    vr = MLA_VROWS

    @pl.when(pl.program_id(1) == 0)
    def _():
        kvn_f = _rms(kvlat_ref[...], kvn_ref[...])
        kvn = kvn_f.astype(BF16)
        kvn_t = kvn_f.T.astype(BF16)
        vrow = lax.broadcasted_iota(jnp.int32, (vr, 1), 0)
        hi, mid = _split(misc_ref[...], 2)
        pe = _dot(hi, pp_ref[...]) + _dot(mid, pp_ref[...])
        pe_rot = _dot(hi, pr_ref[...]) + _dot(mid, pr_ref[...])
        gk = gk_ref[...]
        kr = pe * gk[0:1, :] * cosk_ref[...] + pe_rot * gk[1:2, :] * sink_ref[...]
        ss_pe = jnp.sum(pe * pe, axis=-1, keepdims=True)
        for h in range(MLA_HEADS):
            ls = slice(h * hs, (h + 1) * hs)
            kp = _dot(kvn, wk_ref[:, ls])
            ss = jnp.sum(kp * kp, axis=-1, keepdims=True) + ss_pe
            rinv = lax.rsqrt(ss * (1.0 / MLA_QK) + RMS_EPS)
            k_scr[h] = ((kp * gk[0:1, :] + kr) * rinv).astype(BF16)
            vt = _dot(wvt_ref[h * vr:(h + 1) * vr, :], kvn_t)
            vt_scr[h] = jnp.where(vrow == MLA_V, 1.0, vt).astype(BF16)

    qn = _rms(qlat_ref[...], qn_ref[...]).astype(BF16)
    gq = gq_ref[...]
    gcos = gq[0:1, :] * cosq_ref[...]
    gsin = gq[1:2, :] * sinq_ref[...]
    scale = MLA_QK ** -0.5 * LOG2E
    def scores(h):
        ls = slice(h * hs, (h + 1) * hs)
        qp = _dot(qn, wq_ref[:, ls])
        qr = _dot(qn, wqr_ref[:, ls])
        ss = jnp.sum(qp * qp, axis=-1, keepdims=True)
        rinv = lax.rsqrt(ss * (1.0 / MLA_QK) + RMS_EPS) * scale
        q = ((qp * gcos + qr * gsin) * rinv).astype(BF16)
        return _dot_nt(k_scr[h], q)

    outs = []
    s_next = scores(0)
    for h in range(MLA_HEADS):
        s_t = s_next
        if h + 1 < MLA_HEADS:
            s_next = scores(h + 1)
        p_t = jnp.exp2(s_t - jnp.max(s_t, axis=0, keepdims=True)).astype(BF16)
        o_t = _dot(vt_scr[h], p_t)
        o = o_t[0:MLA_V] / o_t[MLA_V:MLA_V + 1]
        ms = jnp.sum(o * o, axis=0, keepdims=True) * (1.0 / MLA_V)
        outs.append(o * lax.rsqrt(ms + RMS_EPS))
    y_t = jnp.concatenate(outs, axis=0)
    o_ref[...] = (y_t.T * go_ref[...]).astype(o_ref.dtype)


def _mla(u3, cos, sin, qn, wq, wqr, kvn, wk, wv, gq, gk, go, pp, pr, tq):
    bsz, seq, _ = u3.shape
    full2 = lambda a: _resident(a.shape, lambda b, i: (0, 0))
    seq_blk = lambda w, idx: pl.BlockSpec((None, seq, w), lambda b, i: (b, 0, idx))
    q_blk = lambda w, idx: pl.BlockSpec((None, tq, w), lambda b, i: (b, i, idx))
    return pl.pallas_call(
        _mla_kernel,
        grid=(bsz, seq // tq),
        in_specs=[
            q_blk(MLA_Q_LORA, U_QLAT // MLA_Q_LORA),
            seq_blk(MLA_KV_LORA, U_KVLAT // MLA_KV_LORA),
            seq_blk(LANES, U_MISC // LANES),
            seq_blk(LANES, 0), seq_blk(LANES, 0),
            q_blk(LANES, 0), q_blk(LANES, 0),
            full2(qn), full2(wq), full2(wqr), full2(kvn), full2(wk), full2(wv),
            full2(gq), full2(gk), full2(go), full2(pp), full2(pr),
        ],
        out_specs=pl.BlockSpec((None, tq, MLA_WIDTH), lambda b, i: (b, i, 0)),
        out_shape=jax.ShapeDtypeStruct((bsz, seq, MLA_WIDTH), BF16),
        scratch_shapes=[
            pltpu.VMEM((MLA_HEADS, seq, LANES), BF16),
            pltpu.VMEM((MLA_HEADS, MLA_VROWS, seq), BF16),
        ],
        compiler_params=_cparams(("parallel", "arbitrary")),
        name="mla",
    )(u3, u3, u3, cos, sin, cos, sin, qn, wq, wqr, kvn, wk, wv, gq, gk, go, pp, pr)


def _gconv_kernel(h_ref, b_ref, c_ref, w_ref, g_ref, gm_ref, o_ref, pad):
    seq = o_ref.shape[0]
    half = CONV_K // 2
    zeros = jnp.zeros((SUBLANES, LANES), F32)
    pad[0:SUBLANES, :] = zeros
    pad[SUBLANES + seq:2 * SUBLANES + seq, :] = zeros
    pad[SUBLANES:SUBLANES + seq, :] = c_ref[...] * h_ref[...]
    w = w_ref[...]
    gm = gm_ref[...]
    rows = 256
    gsz = CONV_WIDTH // CONV_GROUPS
    for c in range(seq // rows):
        acc = jnp.zeros((rows, LANES), F32)
        for k in range(CONV_K):
            r0 = SUBLANES - half + k + c * rows
            acc = acc + w[k:k + 1, :] * pad[r0:r0 + rows, :]
        rs = slice(c * rows, (c + 1) * rows)
        y = acc * b_ref[rs, :]
        ss = _dot_x01(y * y, gm, parts=2)
        o_ref[rs, :] = (y * lax.rsqrt(ss * (1.0 / gsz) + RMS_EPS) * g_ref[...]).astype(o_ref.dtype)


def _gconv(u3, w_ext, g, gm):
    bsz, seq, _ = u3.shape
    nj = CONV_WIDTH // LANES
    blk = lambda off: pl.BlockSpec((None, seq, LANES), lambda b, j: (b, 0, off // LANES + j))
    return pl.pallas_call(
        _gconv_kernel,
        grid=(bsz, nj),
        in_specs=[
            blk(U_CH), blk(U_CB), blk(U_CC),
            pl.BlockSpec((SUBLANES, LANES), lambda b, j: (0, j)),
            pl.BlockSpec((1, LANES), lambda b, j: (0, j)),
            pl.BlockSpec(gm.shape, lambda b, j: (0, 0)),
        ],
        out_specs=pl.BlockSpec((None, seq, LANES), lambda b, j: (b, 0, j)),
        out_shape=jax.ShapeDtypeStruct((bsz, seq, CONV_WIDTH), BF16),
        scratch_shapes=[pltpu.VMEM((seq + 2 * SUBLANES, LANES), F32)],
        compiler_params=_cparams(("parallel", "parallel")),
        name="gconv",
    )(u3, u3, u3, w_ext, g, gm)


def _rot_half_perm(n):
    h = n // 2
    idx = np.array([(j + h) % n for j in range(n)])
    sign = np.array([-1.0 if j < h else 1.0 for j in range(n)], np.float32)
    return idx, sign


def _prep_w_in(w_in):
    depth = w_in.shape[0]
    o_dt = 2 * SSD_WIDTH + 2 * SSD_GROUPS * SSD_STATE
    o_q = o_dt + 2 * SSD_HEADS
    o_ch = o_q + MLA_Q_LORA + MLA_KV_LORA + MLA_ROPE
    mid = w_in[:, :, o_dt:o_ch].astype(BF16)
    dt_cols = [mid[:, :, d * SSD_HEADS + g * SSD_HPG:d * SSD_HEADS + (g + 1) * SSD_HPG]
               for g in range(SSD_GROUPS) for d in range(2)]
    pad = jnp.zeros((depth, D_MODEL, U_CH - U_MISC - MLA_ROPE - 2 * SSD_HEADS), BF16)
    w_mid = jnp.concatenate([mid[:, :, o_q - o_dt:]] + dt_cols + [pad], axis=-1)
    return w_in[:, :, :o_dt].astype(BF16), w_mid, w_in[:, :, o_ch:].astype(BF16)


def _prep_mla(w_uq, w_ukv, q_head_norm, k_head_norm):
    depth = w_uq.shape[0]
    ridx, rsign = _rot_half_perm(MLA_ROPE)
    wq3 = w_uq.astype(BF16).reshape(depth, MLA_Q_LORA, MLA_HEADS, MLA_QK)
    zq = jnp.zeros((depth, MLA_Q_LORA, MLA_HEADS, LANES - MLA_QK), BF16)
    wq = jnp.concatenate([wq3, zq], axis=-1).reshape(depth, MLA_Q_LORA, MLA_HEADS * LANES)
    half = MLA_ROPE // 2
    rot = jnp.concatenate([-wq3[..., MLA_NOPE + half:], wq3[..., MLA_NOPE:MLA_NOPE + half]], axis=-1)
    zn = jnp.zeros((depth, MLA_Q_LORA, MLA_HEADS, MLA_NOPE), BF16)
    wqr = jnp.concatenate([zn, rot, zq], axis=-1).reshape(depth, MLA_Q_LORA, MLA_HEADS * LANES)
    wkv3 = w_ukv.astype(BF16).reshape(depth, MLA_KV_LORA, MLA_HEADS, MLA_NOPE + MLA_V)
    zk = jnp.zeros((depth, MLA_KV_LORA, MLA_HEADS, LANES - MLA_NOPE), BF16)
    wk = jnp.concatenate([wkv3[..., :MLA_NOPE], zk], axis=-1).reshape(depth, MLA_KV_LORA, MLA_HEADS * LANES)
    wv_t = jnp.swapaxes(wkv3[..., MLA_NOPE:], 1, 3)
    wv_t = jnp.swapaxes(wv_t, 1, 2)
    zv = jnp.zeros((depth, MLA_HEADS, MLA_VROWS - MLA_V, MLA_KV_LORA), BF16)
    wv = jnp.concatenate([wv_t, zv], axis=2).reshape(depth, MLA_HEADS * MLA_VROWS, MLA_KV_LORA)

    def gains(g):
        zt = jnp.zeros((depth, LANES - MLA_QK), F32)
        plain = jnp.concatenate([g, zt], axis=-1)
        g_rope = g[:, MLA_NOPE:]
        rot_g = jnp.concatenate([jnp.zeros((depth, MLA_NOPE), F32), g_rope[:, half:], g_rope[:, :half], zt],
                                axis=-1)
        rest = jnp.zeros((depth, SUBLANES - 2, LANES), F32)
        return jnp.concatenate([plain[:, None], rot_g[:, None], rest], axis=1)

    return wq, wqr, wk, wv, gains(q_head_norm), gains(k_head_norm)


def _mla_consts():
    ridx, rsign = _rot_half_perm(MLA_ROPE)
    half = MLA_ROPE // 2
    inv = ROPE_BASE ** (-jnp.arange(half, dtype=F32) / half)
    invf = jnp.concatenate([jnp.zeros((MLA_NOPE,), F32), inv, inv, jnp.zeros((LANES - MLA_QK,), F32)])[None, :]
    src = np.arange(MLA_ROPE)
    pp = np.zeros((LANES, LANES), np.float32)
    pp[MISC_KPE + src, MLA_NOPE + src] = 1.0
    pr = np.zeros((LANES, LANES), np.float32)
    pr[MISC_KPE + ridx, MLA_NOPE + src] = rsign
    return invf, jnp.asarray(pp, BF16), jnp.asarray(pr, BF16)


def _prep_ssd(ssd_conv_w, ssd_conv_b, ssd_dt_bias, ssd_a_log, ssd_d):
    depth = ssd_conv_w.shape[0]
    cw_ext = jnp.concatenate([ssd_conv_w, ssd_conv_b[:, None, :],
                              jnp.zeros((depth, SUBLANES - SSD_CONV - 1, ssd_conv_w.shape[-1]), F32)], axis=1)

    def per_group(p):
        return p.reshape(depth, 2, SSD_GROUPS, SSD_HPG).transpose(0, 2, 1, 3).reshape(depth, SSD_GROUPS, 2 * SSD_HPG)

    bias_g, alog_g = per_group(ssd_dt_bias), per_group(ssd_a_log)
    n16 = 2 * SSD_HPG
    hps = []
    for g in range(SSD_GROUPS):
        lo = MISC_DT + g * n16
        lane_pad = ((0, 0), (lo, LANES - lo - n16))
        rows = [jnp.pad(bias_g[:, g], lane_pad), jnp.pad(alog_g[:, g], lane_pad),
                jnp.pad(jnp.ones((depth, n16), F32), lane_pad)]
        rows += [jnp.zeros((depth, LANES), F32)] * (SUBLANES - len(rows))
        hps.append(jnp.stack(rows, axis=1))
    hp = jnp.stack(hps, axis=1)
    hpt = jnp.stack([bias_g, alog_g] + [jnp.zeros_like(bias_g)] * (SUBLANES - 2), axis=-1)
    dx = jnp.repeat(ssd_d.reshape(depth, SSD_GROUPS, SSD_HPG), SSD_HEADDIM, axis=-1)[:, :, None, :]
    return cw_ext, hp, hpt, dx


def _ssd_expanders():
    out = []
    for g in range(SSD_GROUPS):
        per_dir = []
        for d in range(2):
            lo = MISC_DT + g * 2 * SSD_HPG + d * SSD_HPG
            e = np.zeros((LANES, SSD_GW), np.float32)
            e[lo + np.arange(SSD_GW) // SSD_HEADDIM, np.arange(SSD_GW)] = 1.0
            per_dir.append(jnp.asarray(e, BF16))
        lo = MISC_DT + g * 2 * SSD_HPG
        sel = np.zeros((LANES, 2 * SSD_HPG * LANES), np.float32)
        sel[lo + np.arange(2 * SSD_HPG * LANES) // LANES, np.arange(2 * SSD_HPG * LANES)] = 1.0
        per_dir.append(jnp.asarray(sel, BF16))
        out.append(per_dir)
    return out


def _prep_mixers(positions, ssd_conv_w, ssd_conv_b, ssd_dt_bias, ssd_a_log, ssd_d, ssd_norm,
                 mla_q_norm, mla_w_uq, mla_kv_norm, mla_w_ukv, mla_q_head_norm, mla_k_head_norm,
                 mla_out_norm, conv_w, conv_out_norm):
    depth = conv_w.shape[0]
    bsz, seq = positions.shape
    gsz = CONV_WIDTH // CONV_GROUPS
    li = np.arange(LANES)
    invf, pp, pr = _mla_consts()
    cos, sin = _rope_tables(positions.reshape(bsz, seq, 1), invf)
    return dict(
        mla_w=_prep_mla(mla_w_uq, mla_w_ukv, mla_q_head_norm, mla_k_head_norm),
        mla_c=(cos, sin, pp, pr),
        mla_n=(mla_q_norm, mla_kv_norm, mla_out_norm),
        ssd=_prep_ssd(ssd_conv_w, ssd_conv_b, ssd_dt_bias, ssd_a_log, ssd_d),
        ssd_norm=ssd_norm,
        expanders=_ssd_expanders(),
        gc_w=jnp.concatenate([conv_w, jnp.zeros((depth, SUBLANES - CONV_K, CONV_WIDTH), F32)], axis=1),
        gc_g=conv_out_norm,
        gm=jnp.asarray((li[:, None] // gsz == li[None, :] // gsz).astype(np.float32), BF16),
    )


def _mixers(u3, dtt, mp, l):
    bsz, seq, _ = u3.shape
    t = bsz * seq
    tq = 512 if seq % 512 == 0 else SSD_CHUNK
    cw_ext, hp, hpt, dx = mp["ssd"]
    ys = [_ssd_group(u3, dtt, cw_ext[l], hp[l, g], hpt[l, g], dx[l, g], mp["ssd_norm"][l][None, :],
                     *mp["expanders"][g], g).reshape(t, SSD_GW)
          for g in range(SSD_GROUPS)]
    wq, wqr, wk, wv, gq, gk = mp["mla_w"]
    cos, sin, pp, pr = mp["mla_c"]
    qn, kvn, go = mp["mla_n"]
    y_mla = _mla(u3, cos, sin, qn[l][None, :], wq[l], wqr[l], kvn[l][None, :], wk[l], wv[l],
                 gq[l], gk[l], go[l][None, :], pp, pr, tq).reshape(t, MLA_WIDTH)
    y_conv = _gconv(u3, mp["gc_w"][l], mp["gc_g"][l][None, :], mp["gm"]).reshape(t, CONV_WIDTH)
    return ys + [y_mla, y_conv]


def kernel(x, positions, ffn1_norm, ffn1_w_gate, ffn1_w_up, ffn1_w_down, mix_norm, w_in, ssd_conv_w, ssd_conv_b, ssd_dt_bias, ssd_a_log, ssd_d, ssd_norm, mla_q_norm, mla_w_uq, mla_kv_norm, mla_w_ukv, mla_q_head_norm, mla_k_head_norm, mla_out_norm, conv_w, conv_out_norm, w_out, ffn2_norm, ffn2_w_gate, ffn2_w_up, ffn2_w_down):
    bsz, seq, _ = x.shape
    depth = w_in.shape[0]
    t = bsz * seq
    assert seq % 256 == 0
    tm = 1024 if t % 1024 == 0 else 256
    ts = 512 if seq % 512 == 0 else 256

    row = lambda a: a[:, None, :]
    f1 = (row(ffn1_norm), ffn1_w_gate.astype(BF16), ffn1_w_up.astype(BF16), (0.5 * ffn1_w_down).astype(BF16))
    f2 = (row(ffn2_norm), ffn2_w_gate.astype(BF16), ffn2_w_up.astype(BF16), (0.5 * ffn2_w_down).astype(BF16))
    w_in_p = _prep_w_in(w_in)
    w_out_b = w_out.astype(BF16)
    mp = _prep_mixers(positions, ssd_conv_w, ssd_conv_b, ssd_dt_bias, ssd_a_log, ssd_d, ssd_norm,
                      mla_q_norm, mla_w_uq, mla_kv_norm, mla_w_ukv, mla_q_head_norm, mla_k_head_norm,
                      mla_out_norm, conv_w, conv_out_norm)

    xt = x.reshape(t, D_MODEL)
    for l in range(depth):
        xt = _ffn(xt, *f1, l, tm)
        u3, dtt = _inproj(xt.reshape(bsz, seq, D_MODEL), row(mix_norm), w_in_p, l, ts)
        ys = _mixers(u3, dtt, mp, l)
        xt = _outproj(xt, ys, w_out_b, l, tm)
        xt = _ffn(xt, *f2, l, tm)
    return xt.reshape(bsz, seq, D_MODEL)
```

```python
import functools
import math

import numpy as np
import jax
import jax.numpy as jnp
from jax import lax
from jax.experimental import pallas as pl
from jax.experimental.pallas import tpu as pltpu

F32 = jnp.float32
BF16 = jnp.bfloat16

D_MODEL = 1024
D_FF = 2816
RMS_EPS = 1e-6

SSD_WIDTH = 1024
SSD_HEADDIM = 64
SSD_HEADS = 16
SSD_GROUPS = 2
SSD_HPG = 8
SSD_STATE = 128
SSD_CONV = 5
SSD_CHUNK = 128
SSD_GW = SSD_HPG * SSD_HEADDIM

MLA_HEADS = 8
MLA_Q_LORA = 256
MLA_KV_LORA = 128
MLA_NOPE = 64
MLA_ROPE = 32
MLA_QK = 96
MLA_V = 64
MLA_WIDTH = 512
ROPE_BASE = 10000.0

CONV_WIDTH = 512
CONV_GROUPS = 8
CONV_K = 3

D_MIX = 2048
LANES = 128
SUBLANES = 8

U_Z = 0
U_XS = 1024
U_B = 2048
U_C = 2304
U_QLAT = 2560
U_KVLAT = 2816
U_MISC = 2944
U_CH = 3072
U_CB = 3584
U_CC = 4096
U_WIDTH = 4608
MISC_KPE = 0
MISC_DT = 32

NEG_BIG = -1e30
LOG2E = math.log2(math.e)
VMEM_LIMIT = 56 * 1024 * 1024
FFN_TF = 256
P1_GROUP = 2
P2_GROUP = 4


def _cparams(sem):
    return pltpu.CompilerParams(dimension_semantics=sem, vmem_limit_bytes=VMEM_LIMIT)


def _resident(block_shape, index_map):
    return pl.BlockSpec(block_shape, index_map, pipeline_mode=pl.Buffered(1))


def _dot(a, b):
    return jnp.dot(a, b, preferred_element_type=F32)


def _dot_nt(a, b):
    return lax.dot_general(a, b, (((1,), (1,)), ((), ())), preferred_element_type=F32)


def _split(v, parts):
    out = []
    r = v
    for i in range(parts):
        t = r.astype(BF16)
        out.append(t)
        if i + 1 < parts:
            r = r - t.astype(F32)
    return out


def _dot_x01(v, m01, parts=3):
    acc = None
    for t in _split(v, parts):
        d = _dot(t, m01)
        acc = d if acc is None else acc + d
    return acc


def _dot_01x(m01, v, parts=3):
    acc = None
    for t in _split(v, parts):
        d = _dot(m01, t)
        acc = d if acc is None else acc + d
    return acc


def _silu(x):
    return x * jax.nn.sigmoid(x)


def _softplus(x):
    return jnp.maximum(x, 0.0) + jnp.log1p(jnp.exp(-jnp.abs(x)))


def _rms(x, g, n=None):
    n = x.shape[-1] if n is None else n
    ms = jnp.sum(x * x, axis=-1, keepdims=True) * (1.0 / n)
    return x * lax.rsqrt(ms + RMS_EPS) * g


def _ffn_kernel(x_ref, g_ref, wg_ref, wu_ref, wd_ref, o_ref, xn_ref):
    x = x_ref[...]
    xn_ref[...] = _rms(x, g_ref[...]).astype(BF16)
    o_ref[...] = x

    def body(f, carry):
        cs = pl.ds(pl.multiple_of(f * FFN_TF, FFN_TF), FFN_TF)
        xn = xn_ref[...]
        h = _silu(_dot(xn, wg_ref[:, cs])) * _dot(xn, wu_ref[:, cs])
        o_ref[...] += _dot(h.astype(BF16), wd_ref[cs, :])
        return carry

    lax.fori_loop(0, D_FF // FFN_TF, body, 0, unroll=True)


def _ffn(x, norm, wg, wu, wd_half, layer, tm):
    t = x.shape[0]
    return pl.pallas_call(
        _ffn_kernel,
        grid=(t // tm,),
        in_specs=[
            pl.BlockSpec((tm, D_MODEL), lambda i: (i, 0)),
            _resident((None, 1, D_MODEL), lambda i: (layer, 0, 0)),
            _resident((None, D_MODEL, D_FF), lambda i: (layer, 0, 0)),
            _resident((None, D_MODEL, D_FF), lambda i: (layer, 0, 0)),
            _resident((None, D_FF, D_MODEL), lambda i: (layer, 0, 0)),
        ],
        out_specs=pl.BlockSpec((tm, D_MODEL), lambda i: (i, 0)),
        out_shape=jax.ShapeDtypeStruct((t, D_MODEL), F32),
        scratch_shapes=[pltpu.VMEM((tm, D_MODEL), BF16)],
        compiler_params=_cparams(("parallel",)),
        name="ffn",
    )(x, norm, wg, wu, wd_half)


def _inproj_kernel(x_ref, g_ref, wa_ref, wm_ref, wc_ref, u_ref, dtt_ref):
    xn = _rms(x_ref[...], g_ref[...]).astype(BF16)
    u_ref[:, :U_QLAT] = _dot(xn, wa_ref[...])
    u_ref[:, U_QLAT:U_CH] = _dot(xn, wm_ref[...])
    u_ref[:, U_CH:] = _dot(xn, wc_ref[...])
    misc_t = u_ref[:, U_MISC:U_MISC + LANES].T
    dtt_ref[...] = misc_t[MISC_DT:MISC_DT + 2 * SSD_HEADS, :]


def _inproj(x3, norm, w_parts, layer, ts):
    bsz, seq, _ = x3.shape
    w_specs = [_resident((None, D_MODEL, w.shape[-1]), lambda b, i: (layer, 0, 0)) for w in w_parts]
    return pl.pallas_call(
        _inproj_kernel,
        grid=(bsz, seq // ts),
        in_specs=[
            pl.BlockSpec((None, ts, D_MODEL), lambda b, i: (b, i, 0)),
            _resident((None, 1, D_MODEL), lambda b, i: (layer, 0, 0)),
        ] + w_specs,
        out_specs=[
            pl.BlockSpec((None, ts, U_WIDTH), lambda b, i: (b, i, 0)),
            pl.BlockSpec((None, 2 * SSD_HEADS, ts), lambda b, i: (b, 0, i)),
        ],
        out_shape=[
            jax.ShapeDtypeStruct((bsz, seq, U_WIDTH), F32),
            jax.ShapeDtypeStruct((bsz, 2 * SSD_HEADS, seq), F32),
        ],
        compiler_params=_cparams(("parallel", "parallel")),
        name="inproj",
    )(x3, norm, *w_parts)


def _outproj_kernel(x_ref, y0_ref, y1_ref, y2_ref, y3_ref, w0_ref, w1_ref, w2_ref, w3_ref, o_ref):
    acc = _dot(y0_ref[...], w0_ref[...])
    acc += _dot(y1_ref[...], w1_ref[...])
    acc += _dot(y2_ref[...], w2_ref[...])
    acc += _dot(y3_ref[...], w3_ref[...])
    o_ref[...] = x_ref[...] + acc


def _outproj(x, ys, w, layer, tm):
    t = x.shape[0]
    wd = D_MIX // len(ys)
    y_spec = pl.BlockSpec((tm, wd), lambda i: (i, 0))
    w_specs = [_resident((None, wd, D_MODEL), functools.partial(lambda k, i: (layer, k, 0), k))
               for k in range(len(ys))]
    return pl.pallas_call(
        _outproj_kernel,
        grid=(t // tm,),
        in_specs=[pl.BlockSpec((tm, D_MODEL), lambda i: (i, 0))] + [y_spec] * len(ys) + w_specs,
        out_specs=pl.BlockSpec((tm, D_MODEL), lambda i: (i, 0)),
        out_shape=jax.ShapeDtypeStruct((t, D_MODEL), F32),
        compiler_params=_cparams(("parallel",)),
        name="outproj",
    )(x, *ys, *([w] * len(ys)))


def _conv_silu_block(src_ref, w, c, ls, seq):
    rows = SSD_CHUNK
    half = SSD_CONV // 2
    r0 = pl.multiple_of(c * rows, rows)
    before = pl.multiple_of(jnp.maximum(r0 - SUBLANES, 0), SUBLANES)
    after = pl.multiple_of(jnp.minimum(r0 + rows, seq - SUBLANES), SUBLANES)
    prev8 = jnp.where(c > 0, src_ref[pl.ds(before, SUBLANES), ls], 0.0)
    next8 = jnp.where(c < seq // rows - 1, src_ref[pl.ds(after, SUBLANES), ls], 0.0)
    cat = jnp.concatenate([prev8, src_ref[pl.ds(r0, rows), ls], next8], axis=0)
    acc = jnp.broadcast_to(w[SSD_CONV:SSD_CONV + 1, ls], (rows, LANES))
    for k in range(SSD_CONV):
        lo = SUBLANES - half + k
        acc = acc + w[k:k + 1, ls] * cat[lo:lo + rows]
    return _silu(acc)


def _ssd_kernel(dt_lane, z_ref, xs_ref, b_ref, c_ref, misc_ref, dtt_ref, cwx_ref, cwb_ref, cwc_ref,
                hp_ref, hpt_ref, dx_ref, ng_ref, ef_ref, eb_ref, o_ref,
                xc, bc, cc, bt, yacc, dts, dtat, ldtt, sf, sb, ecf, ecb, decf, decb, st):
    seq = o_ref.shape[0]
    ch = SSD_CHUNK
    nc = seq // ch
    hpg = SSD_HPG
    fl = dt_lane
    bl = dt_lane + hpg

    wx, wb, wc = cwx_ref[...], cwb_ref[...], cwc_ref[...]
    one_block = slice(0, LANES)

    def chunk_rows(c):
        return pl.ds(pl.multiple_of(c * ch, ch), ch)

    def conv_chunk(c):
        rows = chunk_rows(c)
        for j in range(SSD_GW // LANES):
            ls = slice(j * LANES, (j + 1) * LANES)
            xc[rows, ls] = _conv_silu_block(xs_ref, wx, c, ls, seq)
        bv = _conv_silu_block(b_ref, wb, c, one_block, seq)
        bc[rows, :] = bv.astype(BF16)
        bt[:, rows] = bv.T.astype(BF16)
        cc[rows, :] = _conv_silu_block(c_ref, wc, c, one_block, seq).astype(BF16)

    def conv_first(c, carry):
        conv_chunk(c)
        return carry

    lax.fori_loop(0, nc, conv_first, 0)

    hp = hp_ref[...]
    bias_row = hp[0:1, :]
    a_row = -jnp.exp(hp[1:2, :]) * hp[2:3, :] * LOG2E
    hpt = hpt_ref[...]
    dts[...] = _softplus(misc_ref[...] + bias_row)
    dtt = _softplus(dtt_ref[...] + hpt[:, 0:1])
    dtat[...] = dtt * (-jnp.exp(hpt[:, 1:2]) * LOG2E)
    ldtt[...] = jnp.log(dtt) * LOG2E
    st[...] = jnp.zeros_like(st)
    lane_lo = lax.broadcasted_iota(jnp.int32, (ch, SSD_GW), 1) % LANES < SSD_HEADDIM

    row_i = lax.broadcasted_iota(jnp.int32, (ch, ch), 0)
    col_i = lax.broadcasted_iota(jnp.int32, (ch, ch), 1)
    tri_le = col_i <= row_i
    tri_ge = col_i >= row_i
    t_le = jnp.where(tri_le, 1.0, 0.0).astype(BF16)
    t_ge = jnp.where(tri_ge, 1.0, 0.0).astype(BF16)
    ef = ef_ref[...]
    eb = eb_ref[...]

    g1 = math.gcd(nc, P1_GROUP)
    g2 = math.gcd(nc, P2_GROUP)

    def pass1(i, carry):
        cs = [i * g1 + k for k in range(g1)]
        rws = [chunk_rows(c) for c in cs]
        xs_ = [xc[r, :] for r in rws]
        bts = [bt[:, r] for r in rws]
        dts_ = [dts[r, :] for r in rws]
        dtas = [dt * a_row for dt in dts_]
        cum_fs = [_dot_01x(t_le, dta) for dta in dtas]
        cumt_fs = [_dot_x01(dtat[:, r], t_ge) for r in rws]
        cbs = [_dot_nt(cc[r, :], bc[r, :]) for r in rws]
        tots = [cf[ch - 1:ch, :] for cf in cum_fs]
        cum_bs = [tot - cf + dta for tot, cf, dta in zip(tots, cum_fs, dtas)]
        spreads = []
        for dt, tot, cf, cbw in zip(dts_, tots, cum_fs, cum_bs):
            spreads.append([_dot(jnp.concatenate([jnp.exp2(cum), dt * jnp.exp2(tot - cum)], axis=0).astype(BF16), e)
                            for cum, e in ((cf, ef), (cbw, eb))])
        for k, (c, r) in enumerate(zip(cs, rws)):
            x = xs_[k]
            bt_c = bts[k]
            dec8 = jnp.broadcast_to(jnp.exp2(tots[k]), (SUBLANES, LANES))
            for spread, e, ec_ref, s_ref, dec_ref in ((spreads[k][0], ef, ecf, sf, decf),
                                                      (spreads[k][1], eb, ecb, sb, decb)):
                ec_ref[c] = spread[0:ch].astype(BF16)
                s_ref[c] = _dot(bt_c, (x * spread[ch:]).astype(BF16)).astype(BF16)
                dec_ref[c] = _dot_x01(dec8, e, parts=2)
        for k, (c, r) in enumerate(zip(cs, rws)):
            x = xs_[k]
            dtat_c = dtat[:, r]
            ldt = ldtt[:, r]
            rf = cumt_fs[k] - ldt
            rb = cumt_fs[k][:, ch - 1:ch] - cumt_fs[k] + dtat_c - ldt
            zero = jnp.zeros_like(x)
            x2 = jnp.concatenate([jnp.where(lane_lo, x, zero), jnp.where(lane_lo, zero, x)],
                                 axis=0).astype(BF16)
            cb, cum_f, cum_b = cbs[k], cum_fs[k], cum_bs[k]
            for p in range(hpg // 2):
                ls = slice(p * LANES, (p + 1) * LANES)
                ms = []
                for h in (2 * p, 2 * p + 1):
                    seg_f = cum_f[:, fl + h:fl + h + 1] - rf[h:h + 1, :]
                    seg_b = cum_b[:, bl + h:bl + h + 1] - rb[hpg + h:hpg + h + 1, :]
                    ms.append(cb * (jnp.exp2(jnp.where(tri_le, seg_f, NEG_BIG))
                                    + jnp.exp2(jnp.where(tri_ge, seg_b, NEG_BIG))))
                m = jnp.concatenate(ms, axis=1).astype(BF16)
                yacc[r, ls] = _dot(m, x2[:, ls])
        return carry

    lax.fori_loop(0, nc // g1, pass1, 0)

    def pass2(i, carry):
        work = []
        for d, ec_ref, s_ref, dec_ref in ((0, ecf, sf, decf), (1, ecb, sb, decb)):
            prev = st[d]
            for k in range(g2):
                step = i * g2 + k
                c = step if d == 0 else nc - 1 - step
                work.append((c, ec_ref, prev.astype(BF16)))
                prev = prev * dec_ref[c][0:1, :] + s_ref[c].astype(F32)
            st[d] = prev
        offs = [_dot(cc[chunk_rows(c), :], prev_b) for c, _, prev_b in work]
        for (c, ec_ref, _), off in zip(work, offs):
            yacc[chunk_rows(c), :] += off * ec_ref[c].astype(F32)
        return carry

    lax.fori_loop(0, nc // g2, pass2, 0)

    d_x = dx_ref[...]
    gain = ng_ref[...]

    def pass3(c, carry):
        rows = chunk_rows(c)
        y = (yacc[rows, :] + xc[rows, :] * d_x) * _silu(z_ref[rows, :])
        o_ref[rows, :] = _rms(y, gain).astype(o_ref.dtype)
        return carry

    lax.fori_loop(0, nc, pass3, 0, unroll=2)


def _ssd_group(u3, dtt, cw_ext, hp, hpt, dx, ng, ef, eb, group):
    bsz, seq, _ = u3.shape
    gw = SSD_GW
    n = SSD_STATE
    dt_lane = MISC_DT + group * 2 * SSD_HPG
    nchunk = seq // SSD_CHUNK
    blk = lambda w, idx: pl.BlockSpec((None, seq, w), lambda b: (b, 0, idx))
    full2 = lambda a: _resident(a.shape, lambda b: (0, 0))
    return pl.pallas_call(
        functools.partial(_ssd_kernel, dt_lane),
        grid=(bsz,),
        in_specs=[
            blk(gw, U_Z // gw + group),
            blk(gw, U_XS // gw + group),
            blk(n, U_B // n + group),
            blk(n, U_C // n + group),
            blk(LANES, U_MISC // LANES),
            pl.BlockSpec((None, 2 * SSD_HPG, seq), lambda b: (b, group, 0)),
            _resident((SUBLANES, gw), lambda b: (0, group)),
            _resident((SUBLANES, n), lambda b: (0, SSD_WIDTH // n + group)),
            _resident((SUBLANES, n), lambda b: (0, SSD_WIDTH // n + SSD_GROUPS + group)),
            full2(hp), full2(hpt), full2(dx),
            _resident((1, gw), lambda b: (0, group)),
            full2(ef), full2(eb),
        ],
        out_specs=pl.BlockSpec((None, seq, gw), lambda b: (b, 0, 0)),
        out_shape=jax.ShapeDtypeStruct((bsz, seq, gw), BF16),
        scratch_shapes=[
            pltpu.VMEM((seq, gw), F32),
            pltpu.VMEM((seq, n), BF16),
            pltpu.VMEM((seq, n), BF16),
            pltpu.VMEM((n, seq), BF16),
            pltpu.VMEM((seq, gw), F32),
            pltpu.VMEM((seq, LANES), F32),
            pltpu.VMEM((2 * SSD_HPG, seq), F32),
            pltpu.VMEM((2 * SSD_HPG, seq), F32),
            pltpu.VMEM((nchunk, n, gw), BF16),
            pltpu.VMEM((nchunk, n, gw), BF16),
            pltpu.VMEM((nchunk, SSD_CHUNK, gw), BF16),
            pltpu.VMEM((nchunk, SSD_CHUNK, gw), BF16),
            pltpu.VMEM((nchunk, SUBLANES, gw), F32),
            pltpu.VMEM((nchunk, SUBLANES, gw), F32),
            pltpu.VMEM((2, n, gw), F32),
        ],
        compiler_params=_cparams(("parallel",)),
        name=f"ssd_g{group}",
    )(u3, u3, u3, u3, u3, dtt, cw_ext, cw_ext, cw_ext, hp, hpt, dx, ng, ef, eb)


def _rope_kernel(pos_ref, invf_ref, cos_ref, sin_ref):
    ang = pos_ref[...].astype(F32) * invf_ref[...]
    cos_ref[...] = jnp.cos(ang)
    sin_ref[...] = jnp.sin(ang)


def _rope_tables(pos3, invf):
    bsz, seq, _ = pos3.shape
    out = jax.ShapeDtypeStruct((bsz, seq, LANES), F32)
    spec = pl.BlockSpec((None, seq, LANES), lambda b: (b, 0, 0))
    return pl.pallas_call(
        _rope_kernel,
        grid=(bsz,),
        in_specs=[pl.BlockSpec((None, seq, 1), lambda b: (b, 0, 0)),
                  pl.BlockSpec(invf.shape, lambda b: (0, 0))],
        out_specs=[spec, spec],
        out_shape=[out, out],
        compiler_params=_cparams(("parallel",)),
        name="rope_tables",
    )(pos3, invf)


def _mla_kernel(qlat_ref, kvlat_ref, misc_ref, cosk_ref, sink_ref, cosq_ref, sinq_ref, qn_ref, wq_ref,
                wqr_ref, kvn_ref, wk_ref, wv_ref, gq_ref, gk_ref, go_ref, pp_ref, pr_ref, o_ref,
                k_scr, v_scr):
    hs = LANES
    lane = lax.broadcasted_iota(jnp.int32, (1, hs), 1)
    ones_lane = (MLA_V, 0)
    v_mask = (lane < MLA_V, lane >= MLA_V)

    @pl.when(pl.program_id(1) == 0)
    def _():
        kvn = _rms(kvlat_ref[...], kvn_ref[...]).astype(BF16)
        hi, mid = _split(misc_ref[...], 2)
        pe = _dot(hi, pp_ref[...]) + _dot(mid, pp_ref[...])
        pe_rot = _dot(hi, pr_ref[...]) + _dot(mid, pr_ref[...])
        gk = gk_ref[...]
        kr = pe * gk[0:1, :] * cosk_ref[...] + pe_rot * gk[1:2, :] * sink_ref[...]
        ss_pe = jnp.sum(pe * pe, axis=-1, keepdims=True)
        for h in range(MLA_HEADS):
            ls = slice(h * hs, (h + 1) * hs)
            kp = _dot(kvn, wk_ref[:, ls])
            ss = jnp.sum(kp * kp, axis=-1, keepdims=True) + ss_pe
            rinv = lax.rsqrt(ss * (1.0 / MLA_QK) + RMS_EPS)
            k_scr[h] = ((kp * gk[0:1, :] + kr) * rinv).astype(BF16)
            ones_row = jnp.where(lane == ones_lane[h % 2], 1.0, 0.0)
            v_scr[h] = (_dot(kvn, wv_ref[:, ls]) + ones_row).astype(BF16)

    qn = _rms(qlat_ref[...], qn_ref[...]).astype(BF16)
    gq = gq_ref[...]
    gcos = gq[0:1, :] * cosq_ref[...]
    gsin = gq[1:2, :] * sinq_ref[...]
    scale = MLA_QK ** -0.5 * LOG2E
    def scores(h):
        ls = slice(h * hs, (h + 1) * hs)
        qp = _dot(qn, wq_ref[:, ls])
        qr = _dot(qn, wqr_ref[:, ls])
        ss = jnp.sum(qp * qp, axis=-1, keepdims=True)
        rinv = lax.rsqrt(ss * (1.0 / MLA_QK) + RMS_EPS) * scale
        q = ((qp * gcos + qr * gsin) * rinv).astype(BF16)
        return _dot_nt(q, k_scr[h])

    pair = None
    s_next = scores(0)
    for h in range(MLA_HEADS):
        s = s_next
        if h + 1 < MLA_HEADS:
            s_next = scores(h + 1)
        p = jnp.exp2(s - jnp.max(s, axis=-1, keepdims=True))
        o = _dot(p.astype(BF16), v_scr[h])
        ol = ones_lane[h % 2]
        o = jnp.where(v_mask[h % 2], o, 0.0) / o[:, ol:ol + 1]
        ms = jnp.sum(o * o, axis=-1, keepdims=True) * (1.0 / MLA_V)
        on = o * lax.rsqrt(ms + RMS_EPS)
        if h % 2 == 0:
            pair = on
        else:
            ps = slice((h // 2) * hs, (h // 2 + 1) * hs)
            o_ref[:, ps] = ((pair + on) * go_ref[:, ps]).astype(o_ref.dtype)


def _mla(u3, cos, sin, qn, wq, wqr, kvn, wk, wv, gq, gk, go, pp, pr, tq):
    bsz, seq, _ = u3.shape
    full2 = lambda a: _resident(a.shape, lambda b, i: (0, 0))
    seq_blk = lambda w, idx: pl.BlockSpec((None, seq, w), lambda b, i: (b, 0, idx))
    q_blk = lambda w, idx: pl.BlockSpec((None, tq, w), lambda b, i: (b, i, idx))
    return pl.pallas_call(
        _mla_kernel,
        grid=(bsz, seq // tq),
        in_specs=[
            q_blk(MLA_Q_LORA, U_QLAT // MLA_Q_LORA),
            seq_blk(MLA_KV_LORA, U_KVLAT // MLA_KV_LORA),
            seq_blk(LANES, U_MISC // LANES),
            seq_blk(LANES, 0), seq_blk(LANES, 0),
            q_blk(LANES, 0), q_blk(LANES, 0),
            full2(qn), full2(wq), full2(wqr), full2(kvn), full2(wk), full2(wv),
            full2(gq), full2(gk), full2(go), full2(pp), full2(pr),
        ],
        out_specs=pl.BlockSpec((None, tq, MLA_WIDTH), lambda b, i: (b, i, 0)),
        out_shape=jax.ShapeDtypeStruct((bsz, seq, MLA_WIDTH), BF16),
        scratch_shapes=[
            pltpu.VMEM((MLA_HEADS, seq, LANES), BF16),
            pltpu.VMEM((MLA_HEADS, seq, LANES), BF16),
        ],
        compiler_params=_cparams(("parallel", "arbitrary")),
        name="mla",
    )(u3, u3, u3, cos, sin, cos, sin, qn, wq, wqr, kvn, wk, wv, gq, gk, go, pp, pr)


def _gconv_kernel(h_ref, b_ref, c_ref, w_ref, g_ref, gm_ref, o_ref, pad):
    seq = o_ref.shape[0]
    half = CONV_K // 2
    zeros = jnp.zeros((SUBLANES, LANES), F32)
    pad[0:SUBLANES, :] = zeros
    pad[SUBLANES + seq:2 * SUBLANES + seq, :] = zeros
    pad[SUBLANES:SUBLANES + seq, :] = c_ref[...] * h_ref[...]
    w = w_ref[...]
    gm = gm_ref[...]
    rows = 256
    gsz = CONV_WIDTH // CONV_GROUPS
    for c in range(seq // rows):
        acc = jnp.zeros((rows, LANES), F32)
        for k in range(CONV_K):
            r0 = SUBLANES - half + k + c * rows
            acc = acc + w[k:k + 1, :] * pad[r0:r0 + rows, :]
        rs = slice(c * rows, (c + 1) * rows)
        y = acc * b_ref[rs, :]
        ss = _dot_x01(y * y, gm, parts=2)
        o_ref[rs, :] = (y * lax.rsqrt(ss * (1.0 / gsz) + RMS_EPS) * g_ref[...]).astype(o_ref.dtype)


def _gconv(u3, w_ext, g, gm):
    bsz, seq, _ = u3.shape
    nj = CONV_WIDTH // LANES
    blk = lambda off: pl.BlockSpec((None, seq, LANES), lambda b, j: (b, 0, off // LANES + j))
    return pl.pallas_call(
        _gconv_kernel,
        grid=(bsz, nj),
        in_specs=[
            blk(U_CH), blk(U_CB), blk(U_CC),
            pl.BlockSpec((SUBLANES, LANES), lambda b, j: (0, j)),
            pl.BlockSpec((1, LANES), lambda b, j: (0, j)),
            pl.BlockSpec(gm.shape, lambda b, j: (0, 0)),
        ],
        out_specs=pl.BlockSpec((None, seq, LANES), lambda b, j: (b, 0, j)),
        out_shape=jax.ShapeDtypeStruct((bsz, seq, CONV_WIDTH), BF16),
        scratch_shapes=[pltpu.VMEM((seq + 2 * SUBLANES, LANES), F32)],
        compiler_params=_cparams(("parallel", "parallel")),
        name="gconv",
    )(u3, u3, u3, w_ext, g, gm)


def _rot_half_perm(n):
    h = n // 2
    idx = np.array([(j + h) % n for j in range(n)])
    sign = np.array([-1.0 if j < h else 1.0 for j in range(n)], np.float32)
    return idx, sign


def _prep_w_in(w_in):
    depth = w_in.shape[0]
    o_dt = 2 * SSD_WIDTH + 2 * SSD_GROUPS * SSD_STATE
    o_q = o_dt + 2 * SSD_HEADS
    o_ch = o_q + MLA_Q_LORA + MLA_KV_LORA + MLA_ROPE
    mid = w_in[:, :, o_dt:o_ch].astype(BF16)
    dt_cols = [mid[:, :, d * SSD_HEADS + g * SSD_HPG:d * SSD_HEADS + (g + 1) * SSD_HPG]
               for g in range(SSD_GROUPS) for d in range(2)]
    pad = jnp.zeros((depth, D_MODEL, U_CH - U_MISC - MLA_ROPE - 2 * SSD_HEADS), BF16)
    w_mid = jnp.concatenate([mid[:, :, o_q - o_dt:]] + dt_cols + [pad], axis=-1)
    return w_in[:, :, :o_dt].astype(BF16), w_mid, w_in[:, :, o_ch:].astype(BF16)


def _prep_mla(w_uq, w_ukv, q_head_norm, k_head_norm):
    depth = w_uq.shape[0]
    ridx, rsign = _rot_half_perm(MLA_ROPE)
    wq3 = w_uq.astype(BF16).reshape(depth, MLA_Q_LORA, MLA_HEADS, MLA_QK)
    zq = jnp.zeros((depth, MLA_Q_LORA, MLA_HEADS, LANES - MLA_QK), BF16)
    wq = jnp.concatenate([wq3, zq], axis=-1).reshape(depth, MLA_Q_LORA, MLA_HEADS * LANES)
    half = MLA_ROPE // 2
    rot = jnp.concatenate([-wq3[..., MLA_NOPE + half:], wq3[..., MLA_NOPE:MLA_NOPE + half]], axis=-1)
    zn = jnp.zeros((depth, MLA_Q_LORA, MLA_HEADS, MLA_NOPE), BF16)
    wqr = jnp.concatenate([zn, rot, zq], axis=-1).reshape(depth, MLA_Q_LORA, MLA_HEADS * LANES)
    wkv3 = w_ukv.astype(BF16).reshape(depth, MLA_KV_LORA, MLA_HEADS, MLA_NOPE + MLA_V)
    zk = jnp.zeros((depth, MLA_KV_LORA, MLA_HEADS, LANES - MLA_NOPE), BF16)
    wk = jnp.concatenate([wkv3[..., :MLA_NOPE], zk], axis=-1).reshape(depth, MLA_KV_LORA, MLA_HEADS * LANES)
    v4 = wkv3[..., MLA_NOPE:].reshape(depth, MLA_KV_LORA, MLA_HEADS // 2, 2, MLA_V)
    zv = jnp.zeros_like(v4[:, :, :, 0])
    wv = jnp.stack([jnp.concatenate([v4[:, :, :, 0], zv], axis=-1),
                    jnp.concatenate([zv, v4[:, :, :, 1]], axis=-1)], axis=3)
    wv = wv.reshape(depth, MLA_KV_LORA, MLA_HEADS * LANES)

    def gains(g):
        zt = jnp.zeros((depth, LANES - MLA_QK), F32)
        plain = jnp.concatenate([g, zt], axis=-1)
        g_rope = g[:, MLA_NOPE:]
        rot_g = jnp.concatenate([jnp.zeros((depth, MLA_NOPE), F32), g_rope[:, half:], g_rope[:, :half], zt],
                                axis=-1)
        rest = jnp.zeros((depth, SUBLANES - 2, LANES), F32)
        return jnp.concatenate([plain[:, None], rot_g[:, None], rest], axis=1)

    return wq, wqr, wk, wv, gains(q_head_norm), gains(k_head_norm)


def _mla_consts():
    ridx, rsign = _rot_half_perm(MLA_ROPE)
    half = MLA_ROPE // 2
    inv = ROPE_BASE ** (-jnp.arange(half, dtype=F32) / half)
    invf = jnp.concatenate([jnp.zeros((MLA_NOPE,), F32), inv, inv, jnp.zeros((LANES - MLA_QK,), F32)])[None, :]
    src = np.arange(MLA_ROPE)
    pp = np.zeros((LANES, LANES), np.float32)
    pp[MISC_KPE + src, MLA_NOPE + src] = 1.0
    pr = np.zeros((LANES, LANES), np.float32)
    pr[MISC_KPE + ridx, MLA_NOPE + src] = rsign
    return invf, jnp.asarray(pp, BF16), jnp.asarray(pr, BF16)


def _prep_ssd(ssd_conv_w, ssd_conv_b, ssd_dt_bias, ssd_a_log, ssd_d):
    depth = ssd_conv_w.shape[0]
    cw_ext = jnp.concatenate([ssd_conv_w, ssd_conv_b[:, None, :],
                              jnp.zeros((depth, SUBLANES - SSD_CONV - 1, ssd_conv_w.shape[-1]), F32)], axis=1)

    def per_group(p):
        return p.reshape(depth, 2, SSD_GROUPS, SSD_HPG).transpose(0, 2, 1, 3).reshape(depth, SSD_GROUPS, 2 * SSD_HPG)

    bias_g, alog_g = per_group(ssd_dt_bias), per_group(ssd_a_log)
    n16 = 2 * SSD_HPG
    hps = []
    for g in range(SSD_GROUPS):
        lo = MISC_DT + g * n16
        lane_pad = ((0, 0), (lo, LANES - lo - n16))
        rows = [jnp.pad(bias_g[:, g], lane_pad), jnp.pad(alog_g[:, g], lane_pad),
                jnp.pad(jnp.ones((depth, n16), F32), lane_pad)]
        rows += [jnp.zeros((depth, LANES), F32)] * (SUBLANES - len(rows))
        hps.append(jnp.stack(rows, axis=1))
    hp = jnp.stack(hps, axis=1)
    hpt = jnp.stack([bias_g, alog_g] + [jnp.zeros_like(bias_g)] * (SUBLANES - 2), axis=-1)
    dx = jnp.repeat(ssd_d.reshape(depth, SSD_GROUPS, SSD_HPG), SSD_HEADDIM, axis=-1)[:, :, None, :]
    return cw_ext, hp, hpt, dx


def _ssd_expanders():
    out = []
    for g in range(SSD_GROUPS):
        per_dir = []
        for d in range(2):
            lo = MISC_DT + g * 2 * SSD_HPG + d * SSD_HPG
            e = np.zeros((LANES, SSD_GW), np.float32)
            e[lo + np.arange(SSD_GW) // SSD_HEADDIM, np.arange(SSD_GW)] = 1.0
            per_dir.append(jnp.asarray(e, BF16))
        out.append(per_dir)
    return out


def _prep_mixers(positions, ssd_conv_w, ssd_conv_b, ssd_dt_bias, ssd_a_log, ssd_d, ssd_norm,
                 mla_q_norm, mla_w_uq, mla_kv_norm, mla_w_ukv, mla_q_head_norm, mla_k_head_norm,
                 mla_out_norm, conv_w, conv_out_norm):
    depth = conv_w.shape[0]
    bsz, seq = positions.shape
    gsz = CONV_WIDTH // CONV_GROUPS
    li = np.arange(LANES)
    invf, pp, pr = _mla_consts()
    cos, sin = _rope_tables(positions.reshape(bsz, seq, 1), invf)
    return dict(
        mla_w=_prep_mla(mla_w_uq, mla_w_ukv, mla_q_head_norm, mla_k_head_norm),
        mla_c=(cos, sin, pp, pr),
        mla_n=(mla_q_norm, mla_kv_norm, mla_out_norm),
        ssd=_prep_ssd(ssd_conv_w, ssd_conv_b, ssd_dt_bias, ssd_a_log, ssd_d),
        ssd_norm=ssd_norm,
        expanders=_ssd_expanders(),
        gc_w=jnp.concatenate([conv_w, jnp.zeros((depth, SUBLANES - CONV_K, CONV_WIDTH), F32)], axis=1),
        gc_g=conv_out_norm,
        gm=jnp.asarray((li[:, None] // gsz == li[None, :] // gsz).astype(np.float32), BF16),
    )


def _mixers(u3, dtt, mp, l):
    bsz, seq, _ = u3.shape
    t = bsz * seq
    tq = 512 if seq % 512 == 0 else SSD_CHUNK
    cw_ext, hp, hpt, dx = mp["ssd"]
    ys = [_ssd_group(u3, dtt, cw_ext[l], hp[l, g], hpt[l, g], dx[l, g], mp["ssd_norm"][l][None, :],
                     *mp["expanders"][g], g).reshape(t, SSD_GW)
          for g in range(SSD_GROUPS)]
    wq, wqr, wk, wv, gq, gk = mp["mla_w"]
    cos, sin, pp, pr = mp["mla_c"]
    qn, kvn, go = mp["mla_n"]
    y_mla = _mla(u3, cos, sin, qn[l][None, :], wq[l], wqr[l], kvn[l][None, :], wk[l], wv[l],
                 gq[l], gk[l], go[l][None, :], pp, pr, tq).reshape(t, MLA_WIDTH)
    y_conv = _gconv(u3, mp["gc_w"][l], mp["gc_g"][l][None, :], mp["gm"]).reshape(t, CONV_WIDTH)
    return ys + [y_mla, y_conv]


def kernel(x, positions, ffn1_norm, ffn1_w_gate, ffn1_w_up, ffn1_w_down, mix_norm, w_in, ssd_conv_w, ssd_conv_b, ssd_dt_bias, ssd_a_log, ssd_d, ssd_norm, mla_q_norm, mla_w_uq, mla_kv_norm, mla_w_ukv, mla_q_head_norm, mla_k_head_norm, mla_out_norm, conv_w, conv_out_norm, w_out, ffn2_norm, ffn2_w_gate, ffn2_w_up, ffn2_w_down):
    bsz, seq, _ = x.shape
    depth = w_in.shape[0]
    t = bsz * seq
    assert seq % 256 == 0
    tm = 1024 if t % 1024 == 0 else 256
    ts = 512 if seq % 512 == 0 else 256

    row = lambda a: a[:, None, :]
    f1 = (row(ffn1_norm), ffn1_w_gate.astype(BF16), ffn1_w_up.astype(BF16), (0.5 * ffn1_w_down).astype(BF16))
    f2 = (row(ffn2_norm), ffn2_w_gate.astype(BF16), ffn2_w_up.astype(BF16), (0.5 * ffn2_w_down).astype(BF16))
    w_in_p = _prep_w_in(w_in)
    w_out_b = w_out.astype(BF16)
    mp = _prep_mixers(positions, ssd_conv_w, ssd_conv_b, ssd_dt_bias, ssd_a_log, ssd_d, ssd_norm,
                      mla_q_norm, mla_w_uq, mla_kv_norm, mla_w_ukv, mla_q_head_norm, mla_k_head_norm,
                      mla_out_norm, conv_w, conv_out_norm)

    xt = x.reshape(t, D_MODEL)
    for l in range(depth):
        xt = _ffn(xt, *f1, l, tm)
        u3, dtt = _inproj(xt.reshape(bsz, seq, D_MODEL), row(mix_norm), w_in_p, l, ts)
        ys = _mixers(u3, dtt, mp, l)
        xt = _outproj(xt, ys, w_out_b, l, tm)
        xt = _ffn(xt, *f2, l, tm)
    return xt.reshape(bsz, seq, D_MODEL)
```

```python
import functools
import math

import numpy as np
import jax
import jax.numpy as jnp
from jax import lax
from jax.experimental import pallas as pl
from jax.experimental.pallas import tpu as pltpu

F32 = jnp.float32
BF16 = jnp.bfloat16

D_MODEL = 1024
D_FF = 2816
RMS_EPS = 1e-6

SSD_WIDTH = 1024
SSD_HEADDIM = 64
SSD_HEADS = 16
SSD_GROUPS = 2
SSD_HPG = 8
SSD_STATE = 128
SSD_CONV = 5
SSD_CHUNK = 128
SSD_GW = SSD_HPG * SSD_HEADDIM

MLA_HEADS = 8
MLA_Q_LORA = 256
MLA_KV_LORA = 128
MLA_NOPE = 64
MLA_ROPE = 32
MLA_QK = 96
MLA_V = 64
MLA_WIDTH = 512
ROPE_BASE = 10000.0

CONV_WIDTH = 512
CONV_GROUPS = 8
CONV_K = 3

D_MIX = 2048
LANES = 128
SUBLANES = 8

U_Z = 0
U_XS = 1024
U_B = 2048
U_C = 2304
U_QLAT = 2560
U_KVLAT = 2816
U_MISC = 2944
U_CH = 3072
U_CB = 3584
U_CC = 4096
U_WIDTH = 4608
MISC_KPE = 0
MISC_DT = 32

NEG_BIG = -1e30
LOG2E = math.log2(math.e)
VMEM_LIMIT = 56 * 1024 * 1024
FFN_TF = 256
P1_GROUP = 4
P2_GROUP = 4


def _cparams(sem):
    return pltpu.CompilerParams(dimension_semantics=sem, vmem_limit_bytes=VMEM_LIMIT)


def _resident(block_shape, index_map):
    return pl.BlockSpec(block_shape, index_map, pipeline_mode=pl.Buffered(1))


def _dot(a, b):
    return jnp.dot(a, b, preferred_element_type=F32)


def _dot_nt(a, b):
    return lax.dot_general(a, b, (((1,), (1,)), ((), ())), preferred_element_type=F32)


def _split(v, parts):
    out = []
    r = v
    for i in range(parts):
        t = r.astype(BF16)
        out.append(t)
        if i + 1 < parts:
            r = r - t.astype(F32)
    return out


def _dot_x01(v, m01, parts=3):
    acc = None
    for t in _split(v, parts):
        d = _dot(t, m01)
        acc = d if acc is None else acc + d
    return acc


def _dot_01x(m01, v, parts=3):
    acc = None
    for t in _split(v, parts):
        d = _dot(m01, t)
        acc = d if acc is None else acc + d
    return acc


def _silu(x):
    return x * jax.nn.sigmoid(x)


def _softplus(x):
    return jnp.maximum(x, 0.0) + jnp.log1p(jnp.exp(-jnp.abs(x)))


def _rms(x, g, n=None):
    n = x.shape[-1] if n is None else n
    ms = jnp.sum(x * x, axis=-1, keepdims=True) * (1.0 / n)
    return x * lax.rsqrt(ms + RMS_EPS) * g


def _ffn_kernel(x_ref, g_ref, wg_ref, wu_ref, wd_ref, o_ref, xn_ref):
    x = x_ref[...]
    xn_ref[...] = _rms(x, g_ref[...]).astype(BF16)
    o_ref[...] = x

    def body(f, carry):
        cs = pl.ds(pl.multiple_of(f * FFN_TF, FFN_TF), FFN_TF)
        xn = xn_ref[...]
        h = _silu(_dot(xn, wg_ref[:, cs])) * _dot(xn, wu_ref[:, cs])
        o_ref[...] += _dot(h.astype(BF16), wd_ref[cs, :])
        return carry

    lax.fori_loop(0, D_FF // FFN_TF, body, 0, unroll=True)


def _ffn(x, norm, wg, wu, wd_half, layer, tm):
    t = x.shape[0]
    return pl.pallas_call(
        _ffn_kernel,
        grid=(t // tm,),
        in_specs=[
            pl.BlockSpec((tm, D_MODEL), lambda i: (i, 0)),
            _resident((None, 1, D_MODEL), lambda i: (layer, 0, 0)),
            _resident((None, D_MODEL, D_FF), lambda i: (layer, 0, 0)),
            _resident((None, D_MODEL, D_FF), lambda i: (layer, 0, 0)),
            _resident((None, D_FF, D_MODEL), lambda i: (layer, 0, 0)),
        ],
        out_specs=pl.BlockSpec((tm, D_MODEL), lambda i: (i, 0)),
        out_shape=jax.ShapeDtypeStruct((t, D_MODEL), F32),
        scratch_shapes=[pltpu.VMEM((tm, D_MODEL), BF16)],
        compiler_params=_cparams(("parallel",)),
        name="ffn",
    )(x, norm, wg, wu, wd_half)


def _inproj_kernel(x_ref, g_ref, wa_ref, wm_ref, wc_ref, u_ref, dtt_ref):
    xn = _rms(x_ref[...], g_ref[...]).astype(BF16)
    u_ref[:, :U_QLAT] = _dot(xn, wa_ref[...])
    u_ref[:, U_QLAT:U_CH] = _dot(xn, wm_ref[...])
    u_ref[:, U_CH:] = _dot(xn, wc_ref[...])
    misc_t = u_ref[:, U_MISC:U_MISC + LANES].T
    dtt_ref[...] = misc_t[MISC_DT:MISC_DT + 2 * SSD_HEADS, :]


def _inproj(x3, norm, w_parts, layer, ts):
    bsz, seq, _ = x3.shape
    w_specs = [_resident((None, D_MODEL, w.shape[-1]), lambda b, i: (layer, 0, 0)) for w in w_parts]
    return pl.pallas_call(
        _inproj_kernel,
        grid=(bsz, seq // ts),
        in_specs=[
            pl.BlockSpec((None, ts, D_MODEL), lambda b, i: (b, i, 0)),
            _resident((None, 1, D_MODEL), lambda b, i: (layer, 0, 0)),
        ] + w_specs,
        out_specs=[
            pl.BlockSpec((None, ts, U_WIDTH), lambda b, i: (b, i, 0)),
            pl.BlockSpec((None, 2 * SSD_HEADS, ts), lambda b, i: (b, 0, i)),
        ],
        out_shape=[
            jax.ShapeDtypeStruct((bsz, seq, U_WIDTH), F32),
            jax.ShapeDtypeStruct((bsz, 2 * SSD_HEADS, seq), F32),
        ],
        compiler_params=_cparams(("parallel", "parallel")),
        name="inproj",
    )(x3, norm, *w_parts)


def _outproj_kernel(x_ref, y0_ref, y1_ref, y2_ref, y3_ref, w0_ref, w1_ref, w2_ref, w3_ref, o_ref):
    acc = _dot(y0_ref[...], w0_ref[...])
    acc += _dot(y1_ref[...], w1_ref[...])
    acc += _dot(y2_ref[...], w2_ref[...])
    acc += _dot(y3_ref[...], w3_ref[...])
    o_ref[...] = x_ref[...] + acc


def _outproj(x, ys, w, layer, tm):
    t = x.shape[0]
    wd = D_MIX // len(ys)
    y_spec = pl.BlockSpec((tm, wd), lambda i: (i, 0))
    w_specs = [_resident((None, wd, D_MODEL), functools.partial(lambda k, i: (layer, k, 0), k))
               for k in range(len(ys))]
    return pl.pallas_call(
        _outproj_kernel,
        grid=(t // tm,),
        in_specs=[pl.BlockSpec((tm, D_MODEL), lambda i: (i, 0))] + [y_spec] * len(ys) + w_specs,
        out_specs=pl.BlockSpec((tm, D_MODEL), lambda i: (i, 0)),
        out_shape=jax.ShapeDtypeStruct((t, D_MODEL), F32),
        compiler_params=_cparams(("parallel",)),
        name="outproj",
    )(x, *ys, *([w] * len(ys)))


def _conv_silu_block(src_ref, w, c, ls, seq):
    rows = SSD_CHUNK
    half = SSD_CONV // 2
    r0 = pl.multiple_of(c * rows, rows)
    before = pl.multiple_of(jnp.maximum(r0 - SUBLANES, 0), SUBLANES)
    after = pl.multiple_of(jnp.minimum(r0 + rows, seq - SUBLANES), SUBLANES)
    prev8 = jnp.where(c > 0, src_ref[pl.ds(before, SUBLANES), ls], 0.0)
    next8 = jnp.where(c < seq // rows - 1, src_ref[pl.ds(after, SUBLANES), ls], 0.0)
    cat = jnp.concatenate([prev8, src_ref[pl.ds(r0, rows), ls], next8], axis=0)
    acc = jnp.broadcast_to(w[SSD_CONV:SSD_CONV + 1, ls], (rows, LANES))
    for k in range(SSD_CONV):
        shifted = cat if k == half else pltpu.roll(cat, (half - k) % cat.shape[0], axis=0)
        acc = acc + w[k:k + 1, ls] * shifted[SUBLANES:SUBLANES + rows]
    return _silu(acc)


def _ssd_kernel(dt_lane, z_ref, xs_ref, b_ref, c_ref, misc_ref, dtt_ref, cwx_ref, cwb_ref, cwc_ref,
                hp_ref, hpt_ref, dx_ref, ng_ref, ef_ref, eb_ref, o_ref,
                xc, bc, cc, bt, yacc, dts, dtat, ldtt, sf, sb, ecf, ecb, decf, decb, st):
    seq = o_ref.shape[0]
    ch = SSD_CHUNK
    nc = seq // ch
    hpg = SSD_HPG
    fl = dt_lane
    bl = dt_lane + hpg

    wx, wb, wc = cwx_ref[...], cwb_ref[...], cwc_ref[...]
    one_block = slice(0, LANES)

    def chunk_rows(c):
        return pl.ds(pl.multiple_of(c * ch, ch), ch)

    def conv_chunk(c):
        rows = chunk_rows(c)
        for j in range(SSD_GW // LANES):
            ls = slice(j * LANES, (j + 1) * LANES)
            xc[rows, ls] = _conv_silu_block(xs_ref, wx, c, ls, seq)
        bv = _conv_silu_block(b_ref, wb, c, one_block, seq)
        bc[rows, :] = bv.astype(BF16)
        bt[:, rows] = bv.T.astype(BF16)
        cc[rows, :] = _conv_silu_block(c_ref, wc, c, one_block, seq).astype(BF16)

    def conv_first(c, carry):
        conv_chunk(c)
        return carry

    lax.fori_loop(0, nc, conv_first, 0)

    hp = hp_ref[...]
    bias_row = hp[0:1, :]
    a_row = -jnp.exp(hp[1:2, :]) * hp[2:3, :] * LOG2E
    hpt = hpt_ref[...]
    dts[...] = _softplus(misc_ref[...] + bias_row)
    dtt = _softplus(dtt_ref[...] + hpt[:, 0:1])
    dtat[...] = dtt * (-jnp.exp(hpt[:, 1:2]) * LOG2E)
    ldtt[...] = jnp.log(dtt) * LOG2E
    st[...] = jnp.zeros_like(st)
    lane_lo = lax.broadcasted_iota(jnp.int32, (ch, SSD_GW), 1) % LANES < SSD_HEADDIM

    row_i = lax.broadcasted_iota(jnp.int32, (ch, ch), 0)
    col_i = lax.broadcasted_iota(jnp.int32, (ch, ch), 1)
    tri_le = col_i <= row_i
    tri_ge = col_i >= row_i
    t_le = jnp.where(tri_le, 1.0, 0.0).astype(BF16)
    t_ge = jnp.where(tri_ge, 1.0, 0.0).astype(BF16)
    ef = ef_ref[...]
    eb = eb_ref[...]

    g1 = math.gcd(nc, P1_GROUP)
    g2 = math.gcd(nc, P2_GROUP)

    def pass1(i, carry):
        cs = [i * g1 + k for k in range(g1)]
        rws = [chunk_rows(c) for c in cs]
        xs_ = [xc[r, :] for r in rws]
        bts = [bt[:, r] for r in rws]
        dts_ = [dts[r, :] for r in rws]
        dtas = [dt * a_row for dt in dts_]
        cum_fs = [_dot_01x(t_le, dta) for dta in dtas]
        cumt_fs = [_dot_x01(dtat[:, r], t_ge) for r in rws]
        cbs = [_dot_nt(cc[r, :], bc[r, :]) for r in rws]
        tots = [cf[ch - 1:ch, :] for cf in cum_fs]
        cum_bs = [tot - cf + dta for tot, cf, dta in zip(tots, cum_fs, dtas)]
        spreads = []
        for dt, tot, cf, cbw in zip(dts_, tots, cum_fs, cum_bs):
            spreads.append([_dot(jnp.concatenate([jnp.exp2(cum), dt * jnp.exp2(tot - cum)], axis=0).astype(BF16), e)
                            for cum, e in ((cf, ef), (cbw, eb))])
        for k, (c, r) in enumerate(zip(cs, rws)):
            x = xs_[k]
            bt_c = bts[k]
            dec8 = jnp.broadcast_to(jnp.exp2(tots[k]), (SUBLANES, LANES))
            for spread, e, ec_ref, s_ref, dec_ref in ((spreads[k][0], ef, ecf, sf, decf),
                                                      (spreads[k][1], eb, ecb, sb, decb)):
                ec_ref[c] = spread[0:ch].astype(BF16)
                s_ref[c] = _dot(bt_c, (x * spread[ch:]).astype(BF16)).astype(BF16)
                dec_ref[c] = _dot_x01(dec8, e, parts=2)
        for k, (c, r) in enumerate(zip(cs, rws)):
            x = xs_[k]
            dtat_c = dtat[:, r]
            ldt = ldtt[:, r]
            rf = cumt_fs[k] - ldt
            rb = cumt_fs[k][:, ch - 1:ch] - cumt_fs[k] + dtat_c - ldt
            zero = jnp.zeros_like(x)
            x2 = jnp.concatenate([jnp.where(lane_lo, x, zero), jnp.where(lane_lo, zero, x)],
                                 axis=0).astype(BF16)
            cb, cum_f, cum_b = cbs[k], cum_fs[k], cum_bs[k]
            for p in range(hpg // 2):
                ls = slice(p * LANES, (p + 1) * LANES)
                ms = []
                for h in (2 * p, 2 * p + 1):
                    seg_f = cum_f[:, fl + h:fl + h + 1] - rf[h:h + 1, :]
                    seg_b = cum_b[:, bl + h:bl + h + 1] - rb[hpg + h:hpg + h + 1, :]
                    ms.append(cb * (jnp.exp2(jnp.where(tri_le, seg_f, NEG_BIG))
                                    + jnp.exp2(jnp.where(tri_ge, seg_b, NEG_BIG))))
                m = jnp.concatenate(ms, axis=1).astype(BF16)
                yacc[r, ls] = _dot(m, x2[:, ls])
        return carry

    lax.fori_loop(0, nc // g1, pass1, 0)

    def pass2(i, carry):
        work = []
        for d, ec_ref, s_ref, dec_ref in ((0, ecf, sf, decf), (1, ecb, sb, decb)):
            prev = st[d]
            for k in range(g2):
                step = i * g2 + k
                c = step if d == 0 else nc - 1 - step
                work.append((c, ec_ref, prev.astype(BF16)))
                prev = prev * dec_ref[c][0:1, :] + s_ref[c].astype(F32)
            st[d] = prev
        offs = [_dot(cc[chunk_rows(c), :], prev_b) for c, _, prev_b in work]
        for (c, ec_ref, _), off in zip(work, offs):
            yacc[chunk_rows(c), :] += off * ec_ref[c].astype(F32)
        return carry

    lax.fori_loop(0, nc // g2, pass2, 0)

    d_x = dx_ref[...]
    gain = ng_ref[...]

    def pass3(c, carry):
        rows = chunk_rows(c)
        y = (yacc[rows, :] + xc[rows, :] * d_x) * _silu(z_ref[rows, :])
        o_ref[rows, :] = _rms(y, gain).astype(o_ref.dtype)
        return carry

    lax.fori_loop(0, nc, pass3, 0, unroll=2)


def _ssd_group(u3, dtt, cw_ext, hp, hpt, dx, ng, ef, eb, group):
    bsz, seq, _ = u3.shape
    gw = SSD_GW
    n = SSD_STATE
    dt_lane = MISC_DT + group * 2 * SSD_HPG
    nchunk = seq // SSD_CHUNK
    blk = lambda w, idx: pl.BlockSpec((None, seq, w), lambda b: (b, 0, idx))
    full2 = lambda a: _resident(a.shape, lambda b: (0, 0))
    return pl.pallas_call(
        functools.partial(_ssd_kernel, dt_lane),
        grid=(bsz,),
        in_specs=[
            blk(gw, U_Z // gw + group),
            blk(gw, U_XS // gw + group),
            blk(n, U_B // n + group),
            blk(n, U_C // n + group),
            blk(LANES, U_MISC // LANES),
            pl.BlockSpec((None, 2 * SSD_HPG, seq), lambda b: (b, group, 0)),
            _resident((SUBLANES, gw), lambda b: (0, group)),
            _resident((SUBLANES, n), lambda b: (0, SSD_WIDTH // n + group)),
            _resident((SUBLANES, n), lambda b: (0, SSD_WIDTH // n + SSD_GROUPS + group)),
            full2(hp), full2(hpt), full2(dx),
            _resident((1, gw), lambda b: (0, group)),
            full2(ef), full2(eb),
        ],
        out_specs=pl.BlockSpec((None, seq, gw), lambda b: (b, 0, 0)),
        out_shape=jax.ShapeDtypeStruct((bsz, seq, gw), BF16),
        scratch_shapes=[
            pltpu.VMEM((seq, gw), F32),
            pltpu.VMEM((seq, n), BF16),
            pltpu.VMEM((seq, n), BF16),
            pltpu.VMEM((n, seq), BF16),
            pltpu.VMEM((seq, gw), F32),
            pltpu.VMEM((seq, LANES), F32),
            pltpu.VMEM((2 * SSD_HPG, seq), F32),
            pltpu.VMEM((2 * SSD_HPG, seq), F32),
            pltpu.VMEM((nchunk, n, gw), BF16),
            pltpu.VMEM((nchunk, n, gw), BF16),
            pltpu.VMEM((nchunk, SSD_CHUNK, gw), BF16),
            pltpu.VMEM((nchunk, SSD_CHUNK, gw), BF16),
            pltpu.VMEM((nchunk, SUBLANES, gw), F32),
            pltpu.VMEM((nchunk, SUBLANES, gw), F32),
            pltpu.VMEM((2, n, gw), F32),
        ],
        compiler_params=_cparams(("parallel",)),
        name=f"ssd_g{group}",
    )(u3, u3, u3, u3, u3, dtt, cw_ext, cw_ext, cw_ext, hp, hpt, dx, ng, ef, eb)


def _rope_kernel(pos_ref, invf_ref, cos_ref, sin_ref):
    ang = pos_ref[...].astype(F32) * invf_ref[...]
    cos_ref[...] = jnp.cos(ang)
    sin_ref[...] = jnp.sin(ang)


def _rope_tables(pos3, invf):
    bsz, seq, _ = pos3.shape
    out = jax.ShapeDtypeStruct((bsz, seq, LANES), F32)
    spec = pl.BlockSpec((None, seq, LANES), lambda b: (b, 0, 0))
    return pl.pallas_call(
        _rope_kernel,
        grid=(bsz,),
        in_specs=[pl.BlockSpec((None, seq, 1), lambda b: (b, 0, 0)),
                  pl.BlockSpec(invf.shape, lambda b: (0, 0))],
        out_specs=[spec, spec],
        out_shape=[out, out],
        compiler_params=_cparams(("parallel",)),
        name="rope_tables",
    )(pos3, invf)


def _mla_kernel(qlat_ref, kvlat_ref, misc_ref, cosk_ref, sink_ref, cosq_ref, sinq_ref, qn_ref, wq_ref,
                wqr_ref, kvn_ref, wk_ref, wv_ref, gq_ref, gk_ref, go_ref, pp_ref, pr_ref, o_ref,
                k_scr, v_scr):
    hs = LANES
    lane = lax.broadcasted_iota(jnp.int32, (1, hs), 1)
    ones_lane = (MLA_V, 0)
    v_mask = (lane < MLA_V, lane >= MLA_V)

    @pl.when(pl.program_id(1) == 0)
    def _():
        kvn = _rms(kvlat_ref[...], kvn_ref[...]).astype(BF16)
        hi, mid = _split(misc_ref[...], 2)
        placed = jnp.concatenate([pp_ref[...], pr_ref[...]], axis=1)
        both = _dot(hi, placed) + _dot(mid, placed)
        pe = both[:, :hs]
        pe_rot = both[:, hs:]
        gk = gk_ref[...]
        kr = pe * gk[0:1, :] * cosk_ref[...] + pe_rot * gk[1:2, :] * sink_ref[...]
        ss_pe = jnp.sum(pe * pe, axis=-1, keepdims=True)
        for hp in range(MLA_HEADS // 2):
            ls2 = slice(2 * hp * hs, (2 * hp + 2) * hs)
            kp2 = _dot(kvn, wk_ref[:, ls2])
            v2 = _dot(kvn, wv_ref[:, ls2])
            for e in range(2):
                h = 2 * hp + e
                kp = kp2[:, e * hs:(e + 1) * hs]
                ss = jnp.sum(kp * kp, axis=-1, keepdims=True) + ss_pe
                rinv = lax.rsqrt(ss * (1.0 / MLA_QK) + RMS_EPS)
                k_scr[h] = ((kp * gk[0:1, :] + kr) * rinv).astype(BF16)
                ones_row = jnp.where(lane == ones_lane[e], 1.0, 0.0)
                v_scr[h] = (v2[:, e * hs:(e + 1) * hs] + ones_row).astype(BF16)

    qn = _rms(qlat_ref[...], qn_ref[...]).astype(BF16)
    gq = gq_ref[...]
    gcos = gq[0:1, :] * cosq_ref[...]
    gsin = gq[1:2, :] * sinq_ref[...]
    scale = MLA_QK ** -0.5 * LOG2E
    q_proj = {}

    def scores(h):
        if h % 2 == 0:
            ls2 = slice(h * hs, (h + 2) * hs)
            q_proj[h // 2] = (_dot(qn, wq_ref[:, ls2]), _dot(qn, wqr_ref[:, ls2]))
        ls = slice((h % 2) * hs, (h % 2 + 1) * hs)
        qp, qr = (a[:, ls] for a in q_proj[h // 2])
        ss = jnp.sum(qp * qp, axis=-1, keepdims=True)
        rinv = lax.rsqrt(ss * (1.0 / MLA_QK) + RMS_EPS) * scale
        q = ((qp * gcos + qr * gsin) * rinv).astype(BF16)
        return _dot_nt(q, k_scr[h])

    pair = None
    s_next = scores(0)
    for h in range(MLA_HEADS):
        s = s_next
        if h + 1 < MLA_HEADS:
            s_next = scores(h + 1)
        p = jnp.exp2(s - jnp.max(s, axis=-1, keepdims=True))
        o = _dot(p.astype(BF16), v_scr[h])
        ol = ones_lane[h % 2]
        o = jnp.where(v_mask[h % 2], o, 0.0) / o[:, ol:ol + 1]
        ms = jnp.sum(o * o, axis=-1, keepdims=True) * (1.0 / MLA_V)
        on = o * lax.rsqrt(ms + RMS_EPS)
        if h % 2 == 0:
            pair = on
        else:
            ps = slice((h // 2) * hs, (h // 2 + 1) * hs)
            o_ref[:, ps] = ((pair + on) * go_ref[:, ps]).astype(o_ref.dtype)


def _mla(u3, cos, sin, qn, wq, wqr, kvn, wk, wv, gq, gk, go, pp, pr, tq):
    bsz, seq, _ = u3.shape
    full2 = lambda a: _resident(a.shape, lambda b, i: (0, 0))
    seq_blk = lambda w, idx: pl.BlockSpec((None, seq, w), lambda b, i: (b, 0, idx))
    q_blk = lambda w, idx: pl.BlockSpec((None, tq, w), lambda b, i: (b, i, idx))
    return pl.pallas_call(
        _mla_kernel,
        grid=(bsz, seq // tq),
        in_specs=[
            q_blk(MLA_Q_LORA, U_QLAT // MLA_Q_LORA),
            seq_blk(MLA_KV_LORA, U_KVLAT // MLA_KV_LORA),
            seq_blk(LANES, U_MISC // LANES),
            seq_blk(LANES, 0), seq_blk(LANES, 0),
            q_blk(LANES, 0), q_blk(LANES, 0),
            full2(qn), full2(wq), full2(wqr), full2(kvn), full2(wk), full2(wv),
            full2(gq), full2(gk), full2(go), full2(pp), full2(pr),
        ],
        out_specs=pl.BlockSpec((None, tq, MLA_WIDTH), lambda b, i: (b, i, 0)),
        out_shape=jax.ShapeDtypeStruct((bsz, seq, MLA_WIDTH), BF16),
        scratch_shapes=[
            pltpu.VMEM((MLA_HEADS, seq, LANES), BF16),
            pltpu.VMEM((MLA_HEADS, seq, LANES), BF16),
        ],
        compiler_params=_cparams(("parallel", "arbitrary")),
        name="mla",
    )(u3, u3, u3, cos, sin, cos, sin, qn, wq, wqr, kvn, wk, wv, gq, gk, go, pp, pr)


def _gconv_kernel(h_ref, b_ref, c_ref, w_ref, g_ref, gm_ref, o_ref, pad):
    seq = o_ref.shape[0]
    half = CONV_K // 2
    zeros = jnp.zeros((SUBLANES, LANES), F32)
    pad[0:SUBLANES, :] = zeros
    pad[SUBLANES + seq:2 * SUBLANES + seq, :] = zeros
    pad[SUBLANES:SUBLANES + seq, :] = c_ref[...] * h_ref[...]
    w = w_ref[...]
    gm = gm_ref[...]
    rows = 256
    gsz = CONV_WIDTH // CONV_GROUPS
    for c in range(seq // rows):
        acc = jnp.zeros((rows, LANES), F32)
        for k in range(CONV_K):
            r0 = SUBLANES - half + k + c * rows
            acc = acc + w[k:k + 1, :] * pad[r0:r0 + rows, :]
        rs = slice(c * rows, (c + 1) * rows)
        y = acc * b_ref[rs, :]
        ss = _dot_x01(y * y, gm, parts=2)
        o_ref[rs, :] = (y * lax.rsqrt(ss * (1.0 / gsz) + RMS_EPS) * g_ref[...]).astype(o_ref.dtype)


def _gconv(u3, w_ext, g, gm):
    bsz, seq, _ = u3.shape
    nj = CONV_WIDTH // LANES
    blk = lambda off: pl.BlockSpec((None, seq, LANES), lambda b, j: (b, 0, off // LANES + j))
    return pl.pallas_call(
        _gconv_kernel,
        grid=(bsz, nj),
        in_specs=[
            blk(U_CH), blk(U_CB), blk(U_CC),
            pl.BlockSpec((SUBLANES, LANES), lambda b, j: (0, j)),
            pl.BlockSpec((1, LANES), lambda b, j: (0, j)),
            pl.BlockSpec(gm.shape, lambda b, j: (0, 0)),
        ],
        out_specs=pl.BlockSpec((None, seq, LANES), lambda b, j: (b, 0, j)),
        out_shape=jax.ShapeDtypeStruct((bsz, seq, CONV_WIDTH), BF16),
        scratch_shapes=[pltpu.VMEM((seq + 2 * SUBLANES, LANES), F32)],
        compiler_params=_cparams(("parallel", "parallel")),
        name="gconv",
    )(u3, u3, u3, w_ext, g, gm)


def _rot_half_perm(n):
    h = n // 2
    idx = np.array([(j + h) % n for j in range(n)])
    sign = np.array([-1.0 if j < h else 1.0 for j in range(n)], np.float32)
    return idx, sign


def _prep_w_in(w_in):
    depth = w_in.shape[0]
    o_dt = 2 * SSD_WIDTH + 2 * SSD_GROUPS * SSD_STATE
    o_q = o_dt + 2 * SSD_HEADS
    o_ch = o_q + MLA_Q_LORA + MLA_KV_LORA + MLA_ROPE
    mid = w_in[:, :, o_dt:o_ch].astype(BF16)
    dt_cols = [mid[:, :, d * SSD_HEADS + g * SSD_HPG:d * SSD_HEADS + (g + 1) * SSD_HPG]
               for g in range(SSD_GROUPS) for d in range(2)]
    pad = jnp.zeros((depth, D_MODEL, U_CH - U_MISC - MLA_ROPE - 2 * SSD_HEADS), BF16)
    w_mid = jnp.concatenate([mid[:, :, o_q - o_dt:]] + dt_cols + [pad], axis=-1)
    return w_in[:, :, :o_dt].astype(BF16), w_mid, w_in[:, :, o_ch:].astype(BF16)


def _prep_mla(w_uq, w_ukv, q_head_norm, k_head_norm):
    depth = w_uq.shape[0]
    ridx, rsign = _rot_half_perm(MLA_ROPE)
    wq3 = w_uq.astype(BF16).reshape(depth, MLA_Q_LORA, MLA_HEADS, MLA_QK)
    zq = jnp.zeros((depth, MLA_Q_LORA, MLA_HEADS, LANES - MLA_QK), BF16)
    wq = jnp.concatenate([wq3, zq], axis=-1).reshape(depth, MLA_Q_LORA, MLA_HEADS * LANES)
    half = MLA_ROPE // 2
    rot = jnp.concatenate([-wq3[..., MLA_NOPE + half:], wq3[..., MLA_NOPE:MLA_NOPE + half]], axis=-1)
    zn = jnp.zeros((depth, MLA_Q_LORA, MLA_HEADS, MLA_NOPE), BF16)
    wqr = jnp.concatenate([zn, rot, zq], axis=-1).reshape(depth, MLA_Q_LORA, MLA_HEADS * LANES)
    wkv3 = w_ukv.astype(BF16).reshape(depth, MLA_KV_LORA, MLA_HEADS, MLA_NOPE + MLA_V)
    zk = jnp.zeros((depth, MLA_KV_LORA, MLA_HEADS, LANES - MLA_NOPE), BF16)
    wk = jnp.concatenate([wkv3[..., :MLA_NOPE], zk], axis=-1).reshape(depth, MLA_KV_LORA, MLA_HEADS * LANES)
    v4 = wkv3[..., MLA_NOPE:].reshape(depth, MLA_KV_LORA, MLA_HEADS // 2, 2, MLA_V)
    zv = jnp.zeros_like(v4[:, :, :, 0])
    wv = jnp.stack([jnp.concatenate([v4[:, :, :, 0], zv], axis=-1),
                    jnp.concatenate([zv, v4[:, :, :, 1]], axis=-1)], axis=3)
    wv = wv.reshape(depth, MLA_KV_LORA, MLA_HEADS * LANES)

    def gains(g):
        zt = jnp.zeros((depth, LANES - MLA_QK), F32)
        plain = jnp.concatenate([g, zt], axis=-1)
        g_rope = g[:, MLA_NOPE:]
        rot_g = jnp.concatenate([jnp.zeros((depth, MLA_NOPE), F32), g_rope[:, half:], g_rope[:, :half], zt],
                                axis=-1)
        rest = jnp.zeros((depth, SUBLANES - 2, LANES), F32)
        return jnp.concatenate([plain[:, None], rot_g[:, None], rest], axis=1)

    return wq, wqr, wk, wv, gains(q_head_norm), gains(k_head_norm)


def _mla_consts():
    ridx, rsign = _rot_half_perm(MLA_ROPE)
    half = MLA_ROPE // 2
    inv = ROPE_BASE ** (-jnp.arange(half, dtype=F32) / half)
    invf = jnp.concatenate([jnp.zeros((MLA_NOPE,), F32), inv, inv, jnp.zeros((LANES - MLA_QK,), F32)])[None, :]
    src = np.arange(MLA_ROPE)
    pp = np.zeros((LANES, LANES), np.float32)
    pp[MISC_KPE + src, MLA_NOPE + src] = 1.0
    pr = np.zeros((LANES, LANES), np.float32)
    pr[MISC_KPE + ridx, MLA_NOPE + src] = rsign
    return invf, jnp.asarray(pp, BF16), jnp.asarray(pr, BF16)


def _prep_ssd(ssd_conv_w, ssd_conv_b, ssd_dt_bias, ssd_a_log, ssd_d):
    depth = ssd_conv_w.shape[0]
    cw_ext = jnp.concatenate([ssd_conv_w, ssd_conv_b[:, None, :],
                              jnp.zeros((depth, SUBLANES - SSD_CONV - 1, ssd_conv_w.shape[-1]), F32)], axis=1)

    def per_group(p):
        return p.reshape(depth, 2, SSD_GROUPS, SSD_HPG).transpose(0, 2, 1, 3).reshape(depth, SSD_GROUPS, 2 * SSD_HPG)

    bias_g, alog_g = per_group(ssd_dt_bias), per_group(ssd_a_log)
    n16 = 2 * SSD_HPG
    hps = []
    for g in range(SSD_GROUPS):
        lo = MISC_DT + g * n16
        lane_pad = ((0, 0), (lo, LANES - lo - n16))
        rows = [jnp.pad(bias_g[:, g], lane_pad), jnp.pad(alog_g[:, g], lane_pad),
                jnp.pad(jnp.ones((depth, n16), F32), lane_pad)]
        rows += [jnp.zeros((depth, LANES), F32)] * (SUBLANES - len(rows))
        hps.append(jnp.stack(rows, axis=1))
    hp = jnp.stack(hps, axis=1)
    hpt = jnp.stack([bias_g, alog_g] + [jnp.zeros_like(bias_g)] * (SUBLANES - 2), axis=-1)
    dx = jnp.repeat(ssd_d.reshape(depth, SSD_GROUPS, SSD_HPG), SSD_HEADDIM, axis=-1)[:, :, None, :]
    return cw_ext, hp, hpt, dx


def _ssd_expanders():
    out = []
    for g in range(SSD_GROUPS):
        per_dir = []
        for d in range(2):
            lo = MISC_DT + g * 2 * SSD_HPG + d * SSD_HPG
            e = np.zeros((LANES, SSD_GW), np.float32)
            e[lo + np.arange(SSD_GW) // SSD_HEADDIM, np.arange(SSD_GW)] = 1.0
            per_dir.append(jnp.asarray(e, BF16))
        out.append(per_dir)
    return out


def _prep_mixers(positions, ssd_conv_w, ssd_conv_b, ssd_dt_bias, ssd_a_log, ssd_d, ssd_norm,
                 mla_q_norm, mla_w_uq, mla_kv_norm, mla_w_ukv, mla_q_head_norm, mla_k_head_norm,
                 mla_out_norm, conv_w, conv_out_norm):
    depth = conv_w.shape[0]
    bsz, seq = positions.shape
    gsz = CONV_WIDTH // CONV_GROUPS
    li = np.arange(LANES)
    invf, pp, pr = _mla_consts()
    cos, sin = _rope_tables(positions.reshape(bsz, seq, 1), invf)
    return dict(
        mla_w=_prep_mla(mla_w_uq, mla_w_ukv, mla_q_head_norm, mla_k_head_norm),
        mla_c=(cos, sin, pp, pr),
        mla_n=(mla_q_norm, mla_kv_norm, mla_out_norm),
        ssd=_prep_ssd(ssd_conv_w, ssd_conv_b, ssd_dt_bias, ssd_a_log, ssd_d),
        ssd_norm=ssd_norm,
        expanders=_ssd_expanders(),
        gc_w=jnp.concatenate([conv_w, jnp.zeros((depth, SUBLANES - CONV_K, CONV_WIDTH), F32)], axis=1),
        gc_g=conv_out_norm,
        gm=jnp.asarray((li[:, None] // gsz == li[None, :] // gsz).astype(np.float32), BF16),
    )


def _mixers(u3, dtt, mp, l):
    bsz, seq, _ = u3.shape
    t = bsz * seq
    tq = 512 if seq % 512 == 0 else SSD_CHUNK
    cw_ext, hp, hpt, dx = mp["ssd"]
    ys = [_ssd_group(u3, dtt, cw_ext[l], hp[l, g], hpt[l, g], dx[l, g], mp["ssd_norm"][l][None, :],
                     *mp["expanders"][g], g).reshape(t, SSD_GW)
          for g in range(SSD_GROUPS)]
    wq, wqr, wk, wv, gq, gk = mp["mla_w"]
    cos, sin, pp, pr = mp["mla_c"]
    qn, kvn, go = mp["mla_n"]
    y_mla = _mla(u3, cos, sin, qn[l][None, :], wq[l], wqr[l], kvn[l][None, :], wk[l], wv[l],
                 gq[l], gk[l], go[l][None, :], pp, pr, tq).reshape(t, MLA_WIDTH)
    y_conv = _gconv(u3, mp["gc_w"][l], mp["gc_g"][l][None, :], mp["gm"]).reshape(t, CONV_WIDTH)
    return ys + [y_mla, y_conv]


def kernel(x, positions, ffn1_norm, ffn1_w_gate, ffn1_w_up, ffn1_w_down, mix_norm, w_in, ssd_conv_w, ssd_conv_b, ssd_dt_bias, ssd_a_log, ssd_d, ssd_norm, mla_q_norm, mla_w_uq, mla_kv_norm, mla_w_ukv, mla_q_head_norm, mla_k_head_norm, mla_out_norm, conv_w, conv_out_norm, w_out, ffn2_norm, ffn2_w_gate, ffn2_w_up, ffn2_w_down):
    bsz, seq, _ = x.shape
    depth = w_in.shape[0]
    t = bsz * seq
    assert seq % 256 == 0
    tm = 1024 if t % 1024 == 0 else 256
    ts = 512 if seq % 512 == 0 else 256

    row = lambda a: a[:, None, :]
    f1 = (row(ffn1_norm), ffn1_w_gate.astype(BF16), ffn1_w_up.astype(BF16), (0.5 * ffn1_w_down).astype(BF16))
    f2 = (row(ffn2_norm), ffn2_w_gate.astype(BF16), ffn2_w_up.astype(BF16), (0.5 * ffn2_w_down).astype(BF16))
    w_in_p = _prep_w_in(w_in)
    w_out_b = w_out.astype(BF16)
    mp = _prep_mixers(positions, ssd_conv_w, ssd_conv_b, ssd_dt_bias, ssd_a_log, ssd_d, ssd_norm,
                      mla_q_norm, mla_w_uq, mla_kv_norm, mla_w_ukv, mla_q_head_norm, mla_k_head_norm,
                      mla_out_norm, conv_w, conv_out_norm)

    xt = x.reshape(t, D_MODEL)
    for l in range(depth):
        xt = _ffn(xt, *f1, l, tm)
        u3, dtt = _inproj(xt.reshape(bsz, seq, D_MODEL), row(mix_norm), w_in_p, l, ts)
        ys = _mixers(u3, dtt, mp, l)
        xt = _outproj(xt, ys, w_out_b, l, tm)
        xt = _ffn(xt, *f2, l, tm)
    return xt.reshape(bsz, seq, D_MODEL)
```

```python
import functools
import math

import numpy as np
import jax
import jax.numpy as jnp
from jax import lax
from jax.experimental import pallas as pl
from jax.experimental.pallas import tpu as pltpu

F32 = jnp.float32
BF16 = jnp.bfloat16

D_MODEL = 1024
D_FF = 2816
RMS_EPS = 1e-6

SSD_WIDTH = 1024
SSD_HEADDIM = 64
SSD_HEADS = 16
SSD_GROUPS = 2
SSD_HPG = 8
SSD_STATE = 128
SSD_CONV = 5
SSD_CHUNK = 128
SSD_GW = SSD_HPG * SSD_HEADDIM

MLA_HEADS = 8
MLA_Q_LORA = 256
MLA_KV_LORA = 128
MLA_NOPE = 64
MLA_ROPE = 32
MLA_QK = 96
MLA_V = 64
MLA_WIDTH = 512
ROPE_BASE = 10000.0

CONV_WIDTH = 512
CONV_GROUPS = 8
CONV_K = 3

D_MIX = 2048
LANES = 128
SUBLANES = 8

U_Z = 0
U_XS = 1024
U_B = 2048
U_C = 2304
U_QLAT = 2560
U_KVLAT = 2816
U_MISC = 2944
U_CH = 3072
U_CB = 3584
U_CC = 4096
U_WIDTH = 4608
MISC_KPE = 0
MISC_DT = 32

NEG_BIG = -1e30
LOG2E = math.log2(math.e)
VMEM_LIMIT = 56 * 1024 * 1024
FFN_TF = 256
P1_GROUP = 4
P2_GROUP = 4


def _cparams(sem):
    return pltpu.CompilerParams(dimension_semantics=sem, vmem_limit_bytes=VMEM_LIMIT)


def _resident(block_shape, index_map):
    return pl.BlockSpec(block_shape, index_map, pipeline_mode=pl.Buffered(1))


def _dot(a, b):
    return jnp.dot(a, b, preferred_element_type=F32)


def _dot_nt(a, b):
    return lax.dot_general(a, b, (((1,), (1,)), ((), ())), preferred_element_type=F32)


def _split(v, parts):
    out = []
    r = v
    for i in range(parts):
        t = r.astype(BF16)
        out.append(t)
        if i + 1 < parts:
            r = r - t.astype(F32)
    return out


def _dot_x01(v, m01, parts=3):
    acc = None
    for t in _split(v, parts):
        d = _dot(t, m01)
        acc = d if acc is None else acc + d
    return acc


def _dot_01x(m01, v, parts=3):
    acc = None
    for t in _split(v, parts):
        d = _dot(m01, t)
        acc = d if acc is None else acc + d
    return acc


def _silu(x):
    return x * jax.nn.sigmoid(x)


def _softplus(x):
    return jnp.maximum(x, 0.0) + jnp.log1p(jnp.exp(-jnp.abs(x)))


def _rms(x, g, n=None):
    n = x.shape[-1] if n is None else n
    ms = jnp.sum(x * x, axis=-1, keepdims=True) * (1.0 / n)
    return x * lax.rsqrt(ms + RMS_EPS) * g


def _ffn_kernel(x_ref, g_ref, wg_ref, wu_ref, wd_ref, o_ref, xn_ref):
    x = x_ref[...]
    xn_ref[...] = _rms(x, g_ref[...]).astype(BF16)
    o_ref[...] = x

    def body(f, carry):
        cs = pl.ds(pl.multiple_of(f * FFN_TF, FFN_TF), FFN_TF)
        xn = xn_ref[...]
        h = _silu(_dot(xn, wg_ref[:, cs])) * _dot(xn, wu_ref[:, cs])
        o_ref[...] += _dot(h.astype(BF16), wd_ref[cs, :])
        return carry

    lax.fori_loop(0, D_FF // FFN_TF, body, 0, unroll=True)


def _ffn(x, norm, wg, wu, wd_half, layer, tm):
    t = x.shape[0]
    return pl.pallas_call(
        _ffn_kernel,
        grid=(t // tm,),
        in_specs=[
            pl.BlockSpec((tm, D_MODEL), lambda i: (i, 0)),
            _resident((None, 1, D_MODEL), lambda i: (layer, 0, 0)),
            _resident((None, D_MODEL, D_FF), lambda i: (layer, 0, 0)),
            _resident((None, D_MODEL, D_FF), lambda i: (layer, 0, 0)),
            _resident((None, D_FF, D_MODEL), lambda i: (layer, 0, 0)),
        ],
        out_specs=pl.BlockSpec((tm, D_MODEL), lambda i: (i, 0)),
        out_shape=jax.ShapeDtypeStruct((t, D_MODEL), F32),
        scratch_shapes=[pltpu.VMEM((tm, D_MODEL), BF16)],
        compiler_params=_cparams(("parallel",)),
        name="ffn",
    )(x, norm, wg, wu, wd_half)


def _inproj_kernel(x_ref, g_ref, wa_ref, wm_ref, wc_ref, u_ref, misc_ref, dtt_ref):
    xn = _rms(x_ref[...], g_ref[...]).astype(BF16)
    u_ref[:, :U_QLAT] = _dot(xn, wa_ref[...]).astype(BF16)
    mid = _dot(xn, wm_ref[...])
    u_ref[:, U_QLAT:U_CH] = mid.astype(BF16)
    u_ref[:, U_CH:] = _dot(xn, wc_ref[...]).astype(BF16)
    misc = mid[:, U_MISC - U_QLAT:U_MISC - U_QLAT + LANES]
    misc_ref[...] = misc
    dtt_ref[...] = misc.T[MISC_DT:MISC_DT + 2 * SSD_HEADS, :]


def _inproj(x3, norm, w_parts, layer, ts):
    bsz, seq, _ = x3.shape
    w_specs = [_resident((None, D_MODEL, w.shape[-1]), lambda b, i: (layer, 0, 0)) for w in w_parts]
    return pl.pallas_call(
        _inproj_kernel,
        grid=(bsz, seq // ts),
        in_specs=[
            pl.BlockSpec((None, ts, D_MODEL), lambda b, i: (b, i, 0)),
            _resident((None, 1, D_MODEL), lambda b, i: (layer, 0, 0)),
        ] + w_specs,
        out_specs=[
            pl.BlockSpec((None, ts, U_WIDTH), lambda b, i: (b, i, 0)),
            pl.BlockSpec((None, ts, LANES), lambda b, i: (b, i, 0)),
            pl.BlockSpec((None, 2 * SSD_HEADS, ts), lambda b, i: (b, 0, i)),
        ],
        out_shape=[
            jax.ShapeDtypeStruct((bsz, seq, U_WIDTH), BF16),
            jax.ShapeDtypeStruct((bsz, seq, LANES), F32),
            jax.ShapeDtypeStruct((bsz, 2 * SSD_HEADS, seq), F32),
        ],
        compiler_params=_cparams(("parallel", "parallel")),
        name="inproj",
    )(x3, norm, *w_parts)


def _outproj_kernel(x_ref, y0_ref, y1_ref, y2_ref, y3_ref, w0_ref, w1_ref, w2_ref, w3_ref, o_ref):
    acc = _dot(y0_ref[...], w0_ref[...])
    acc += _dot(y1_ref[...], w1_ref[...])
    acc += _dot(y2_ref[...], w2_ref[...])
    acc += _dot(y3_ref[...], w3_ref[...])
    o_ref[...] = x_ref[...] + acc


def _outproj(x, ys, w, layer, tm):
    t = x.shape[0]
    wd = D_MIX // len(ys)
    y_spec = pl.BlockSpec((tm, wd), lambda i: (i, 0))
    w_specs = [_resident((None, wd, D_MODEL), functools.partial(lambda k, i: (layer, k, 0), k))
               for k in range(len(ys))]
    return pl.pallas_call(
        _outproj_kernel,
        grid=(t // tm,),
        in_specs=[pl.BlockSpec((tm, D_MODEL), lambda i: (i, 0))] + [y_spec] * len(ys) + w_specs,
        out_specs=pl.BlockSpec((tm, D_MODEL), lambda i: (i, 0)),
        out_shape=jax.ShapeDtypeStruct((t, D_MODEL), F32),
        compiler_params=_cparams(("parallel",)),
        name="outproj",
    )(x, *ys, *([w] * len(ys)))


def _conv_silu_block(src_ref, w, c, ls, seq):
    rows = SSD_CHUNK
    half = SSD_CONV // 2
    halo = 2 * SUBLANES
    r0 = pl.multiple_of(c * rows, rows)
    before = pl.multiple_of(jnp.maximum(r0 - halo, 0), halo)
    after = pl.multiple_of(jnp.minimum(r0 + rows, seq - halo), halo)
    prev = jnp.where(c > 0, src_ref[pl.ds(before, halo), ls].astype(F32), 0.0)
    nxt = jnp.where(c < seq // rows - 1, src_ref[pl.ds(after, halo), ls].astype(F32), 0.0)
    cat = jnp.concatenate([prev, src_ref[pl.ds(r0, rows), ls].astype(F32), nxt], axis=0)
    acc = jnp.broadcast_to(w[SSD_CONV:SSD_CONV + 1, ls], (rows, LANES))
    for k in range(SSD_CONV):
        shifted = cat if k == half else pltpu.roll(cat, (half - k) % cat.shape[0], axis=0)
        acc = acc + w[k:k + 1, ls] * shifted[halo:halo + rows]
    return _silu(acc)


def _ssd_kernel(dt_lane, z_ref, xs_ref, b_ref, c_ref, misc_ref, dtt_ref, cwx_ref, cwb_ref, cwc_ref,
                hp_ref, hpt_ref, dx_ref, ng_ref, ef_ref, eb_ref, o_ref,
                xc, bc, cc, bt, yacc, dts, dtat, ldtt, sf, sb, ecf, ecb, decf, decb, st):
    seq = o_ref.shape[0]
    ch = SSD_CHUNK
    nc = seq // ch
    hpg = SSD_HPG
    fl = dt_lane
    bl = dt_lane + hpg

    wx, wb, wc = cwx_ref[...], cwb_ref[...], cwc_ref[...]
    one_block = slice(0, LANES)

    def chunk_rows(c):
        return pl.ds(pl.multiple_of(c * ch, ch), ch)

    def conv_chunk(c):
        rows = chunk_rows(c)
        for j in range(SSD_GW // LANES):
            ls = slice(j * LANES, (j + 1) * LANES)
            xc[rows, ls] = _conv_silu_block(xs_ref, wx, c, ls, seq)
        bv = _conv_silu_block(b_ref, wb, c, one_block, seq)
        bc[rows, :] = bv.astype(BF16)
        bt[:, rows] = bv.T.astype(BF16)
        cc[rows, :] = _conv_silu_block(c_ref, wc, c, one_block, seq).astype(BF16)

    def conv_first(c, carry):
        conv_chunk(c)
        return carry

    lax.fori_loop(0, nc, conv_first, 0)

    hp = hp_ref[...]
    bias_row = hp[0:1, :]
    a_row = -jnp.exp(hp[1:2, :]) * hp[2:3, :] * LOG2E
    hpt = hpt_ref[...]
    dts[...] = _softplus(misc_ref[...] + bias_row)
    dtt = _softplus(dtt_ref[...] + hpt[:, 0:1])
    dtat[...] = dtt * (-jnp.exp(hpt[:, 1:2]) * LOG2E)
    ldtt[...] = jnp.log(dtt) * LOG2E
    st[...] = jnp.zeros_like(st)
    lane_lo = lax.broadcasted_iota(jnp.int32, (ch, SSD_GW), 1) % LANES < SSD_HEADDIM

    row_i = lax.broadcasted_iota(jnp.int32, (ch, ch), 0)
    col_i = lax.broadcasted_iota(jnp.int32, (ch, ch), 1)
    tri_le = col_i <= row_i
    tri_ge = col_i >= row_i
    t_le = jnp.where(tri_le, 1.0, 0.0).astype(BF16)
    t_ge = jnp.where(tri_ge, 1.0, 0.0).astype(BF16)
    ef = ef_ref[...]
    eb = eb_ref[...]

    g1 = math.gcd(nc, P1_GROUP)
    g2 = math.gcd(nc, P2_GROUP)

    def pass1(i, carry):
        cs = [i * g1 + k for k in range(g1)]
        rws = [chunk_rows(c) for c in cs]
        xs_ = [xc[r, :] for r in rws]
        bts = [bt[:, r] for r in rws]
        dts_ = [dts[r, :] for r in rws]
        dtas = [dt * a_row for dt in dts_]
        cum_fs = [_dot_01x(t_le, dta) for dta in dtas]
        cumt_fs = [_dot_x01(dtat[:, r], t_ge) for r in rws]
        cbs = [_dot_nt(cc[r, :], bc[r, :]) for r in rws]
        tots = [cf[ch - 1:ch, :] for cf in cum_fs]
        cum_bs = [tot - cf + dta for tot, cf, dta in zip(tots, cum_fs, dtas)]
        spreads = []
        for dt, tot, cf, cbw in zip(dts_, tots, cum_fs, cum_bs):
            spreads.append([_dot(jnp.concatenate([jnp.exp2(cum), dt * jnp.exp2(tot - cum)], axis=0).astype(BF16), e)
                            for cum, e in ((cf, ef), (cbw, eb))])
        for k, (c, r) in enumerate(zip(cs, rws)):
            x = xs_[k]
            bt_c = bts[k]
            dec8 = jnp.broadcast_to(jnp.exp2(tots[k]), (SUBLANES, LANES))
            for spread, e, ec_ref, s_ref, dec_ref in ((spreads[k][0], ef, ecf, sf, decf),
                                                      (spreads[k][1], eb, ecb, sb, decb)):
                ec_ref[c] = spread[0:ch].astype(BF16)
                s_ref[c] = _dot(bt_c, (x * spread[ch:]).astype(BF16)).astype(BF16)
                dec_ref[c] = _dot_x01(dec8, e, parts=2)
        for k, (c, r) in enumerate(zip(cs, rws)):
            x = xs_[k]
            dtat_c = dtat[:, r]
            ldt = ldtt[:, r]
            rf = cumt_fs[k] - ldt
            rb = cumt_fs[k][:, ch - 1:ch] - cumt_fs[k] + dtat_c - ldt
            zero = jnp.zeros_like(x)
            x2 = jnp.concatenate([jnp.where(lane_lo, x, zero), jnp.where(lane_lo, zero, x)],
                                 axis=0).astype(BF16)
            cb, cum_f, cum_b = cbs[k], cum_fs[k], cum_bs[k]
            for p in range(hpg // 2):
                ls = slice(p * LANES, (p + 1) * LANES)
                ms = []
                for h in (2 * p, 2 * p + 1):
                    seg_f = cum_f[:, fl + h:fl + h + 1] - rf[h:h + 1, :]
                    seg_b = cum_b[:, bl + h:bl + h + 1] - rb[hpg + h:hpg + h + 1, :]
                    ms.append(cb * (jnp.exp2(jnp.where(tri_le, seg_f, NEG_BIG))
                                    + jnp.exp2(jnp.where(tri_ge, seg_b, NEG_BIG))))
                m = jnp.concatenate(ms, axis=1).astype(BF16)
                yacc[r, ls] = _dot(m, x2[:, ls])
        return carry

    lax.fori_loop(0, nc // g1, pass1, 0)

    def pass2(i, carry):
        work = []
        for d, ec_ref, s_ref, dec_ref in ((0, ecf, sf, decf), (1, ecb, sb, decb)):
            prev = st[d]
            for k in range(g2):
                step = i * g2 + k
                c = step if d == 0 else nc - 1 - step
                work.append((c, ec_ref, prev.astype(BF16)))
                prev = prev * dec_ref[c][0:1, :] + s_ref[c].astype(F32)
            st[d] = prev
        offs = [_dot(cc[chunk_rows(c), :], prev_b) for c, _, prev_b in work]
        for (c, ec_ref, _), off in zip(work, offs):
            yacc[chunk_rows(c), :] += off * ec_ref[c].astype(F32)
        return carry

    lax.fori_loop(0, nc // g2, pass2, 0)

    d_x = dx_ref[...]
    gain = ng_ref[...]

    def pass3(c, carry):
        rows = chunk_rows(c)
        y = (yacc[rows, :] + xc[rows, :] * d_x) * _silu(z_ref[rows, :].astype(F32))
        o_ref[rows, :] = _rms(y, gain).astype(o_ref.dtype)
        return carry

    lax.fori_loop(0, nc, pass3, 0, unroll=2)


def _ssd_group(u3, misc, dtt, cw_ext, hp, hpt, dx, ng, ef, eb, group):
    bsz, seq, _ = u3.shape
    gw = SSD_GW
    n = SSD_STATE
    dt_lane = MISC_DT + group * 2 * SSD_HPG
    nchunk = seq // SSD_CHUNK
    blk = lambda w, idx: pl.BlockSpec((None, seq, w), lambda b: (b, 0, idx))
    full2 = lambda a: _resident(a.shape, lambda b: (0, 0))
    return pl.pallas_call(
        functools.partial(_ssd_kernel, dt_lane),
        grid=(bsz,),
        in_specs=[
            blk(gw, U_Z // gw + group),
            blk(gw, U_XS // gw + group),
            blk(n, U_B // n + group),
            blk(n, U_C // n + group),
            blk(LANES, 0),
            pl.BlockSpec((None, 2 * SSD_HPG, seq), lambda b: (b, group, 0)),
            _resident((SUBLANES, gw), lambda b: (0, group)),
            _resident((SUBLANES, n), lambda b: (0, SSD_WIDTH // n + group)),
            _resident((SUBLANES, n), lambda b: (0, SSD_WIDTH // n + SSD_GROUPS + group)),
            full2(hp), full2(hpt), full2(dx),
            _resident((1, gw), lambda b: (0, group)),
            full2(ef), full2(eb),
        ],
        out_specs=pl.BlockSpec((None, seq, gw), lambda b: (b, 0, 0)),
        out_shape=jax.ShapeDtypeStruct((bsz, seq, gw), BF16),
        scratch_shapes=[
            pltpu.VMEM((seq, gw), F32),
            pltpu.VMEM((seq, n), BF16),
            pltpu.VMEM((seq, n), BF16),
            pltpu.VMEM((n, seq), BF16),
            pltpu.VMEM((seq, gw), F32),
            pltpu.VMEM((seq, LANES), F32),
            pltpu.VMEM((2 * SSD_HPG, seq), F32),
            pltpu.VMEM((2 * SSD_HPG, seq), F32),
            pltpu.VMEM((nchunk, n, gw), BF16),
            pltpu.VMEM((nchunk, n, gw), BF16),
            pltpu.VMEM((nchunk, SSD_CHUNK, gw), BF16),
            pltpu.VMEM((nchunk, SSD_CHUNK, gw), BF16),
            pltpu.VMEM((nchunk, SUBLANES, gw), F32),
            pltpu.VMEM((nchunk, SUBLANES, gw), F32),
            pltpu.VMEM((2, n, gw), F32),
        ],
        compiler_params=_cparams(("parallel",)),
        name=f"ssd_g{group}",
    )(u3, u3, u3, u3, misc, dtt, cw_ext, cw_ext, cw_ext, hp, hpt, dx, ng, ef, eb)


def _rope_kernel(pos_ref, invf_ref, cos_ref, sin_ref):
    ang = pos_ref[...].astype(F32) * invf_ref[...]
    cos_ref[...] = jnp.cos(ang)
    sin_ref[...] = jnp.sin(ang)


def _rope_tables(pos3, invf):
    bsz, seq, _ = pos3.shape
    out = jax.ShapeDtypeStruct((bsz, seq, LANES), F32)
    spec = pl.BlockSpec((None, seq, LANES), lambda b: (b, 0, 0))
    return pl.pallas_call(
        _rope_kernel,
        grid=(bsz,),
        in_specs=[pl.BlockSpec((None, seq, 1), lambda b: (b, 0, 0)),
                  pl.BlockSpec(invf.shape, lambda b: (0, 0))],
        out_specs=[spec, spec],
        out_shape=[out, out],
        compiler_params=_cparams(("parallel",)),
        name="rope_tables",
    )(pos3, invf)


def _mla_kernel(qlat_ref, kvlat_ref, misc_ref, cosk_ref, sink_ref, cosq_ref, sinq_ref, qn_ref, wq_ref,
                wqr_ref, kvn_ref, wk_ref, wv_ref, gq_ref, gk_ref, go_ref, pp_ref, pr_ref, o_ref,
                k_scr, v_scr):
    hs = LANES
    lane = lax.broadcasted_iota(jnp.int32, (1, hs), 1)
    ones_lane = (MLA_V, 0)
    v_mask = (lane < MLA_V, lane >= MLA_V)

    @pl.when(pl.program_id(1) == 0)
    def _():
        kvn = _rms(kvlat_ref[...].astype(F32), kvn_ref[...]).astype(BF16)
        hi, mid = _split(misc_ref[...], 2)
        placed = jnp.concatenate([pp_ref[...], pr_ref[...]], axis=1)
        both = _dot(hi, placed) + _dot(mid, placed)
        pe = both[:, :hs]
        pe_rot = both[:, hs:]
        gk = gk_ref[...]
        kr = pe * gk[0:1, :] * cosk_ref[...] + pe_rot * gk[1:2, :] * sink_ref[...]
        ss_pe = jnp.sum(pe * pe, axis=-1, keepdims=True)
        for hp in range(MLA_HEADS // 2):
            ls2 = slice(2 * hp * hs, (2 * hp + 2) * hs)
            kp2 = _dot(kvn, wk_ref[:, ls2])
            v2 = _dot(kvn, wv_ref[:, ls2])
            for e in range(2):
                h = 2 * hp + e
                kp = kp2[:, e * hs:(e + 1) * hs]
                ss = jnp.sum(kp * kp, axis=-1, keepdims=True) + ss_pe
                rinv = lax.rsqrt(ss * (1.0 / MLA_QK) + RMS_EPS)
                k_scr[h] = ((kp * gk[0:1, :] + kr) * rinv).astype(BF16)
                ones_row = jnp.where(lane == ones_lane[e], 1.0, 0.0)
                v_scr[h] = (v2[:, e * hs:(e + 1) * hs] + ones_row).astype(BF16)

    qn = _rms(qlat_ref[...].astype(F32), qn_ref[...]).astype(BF16)
    gq = gq_ref[...]
    gcos = gq[0:1, :] * cosq_ref[...]
    gsin = gq[1:2, :] * sinq_ref[...]
    scale = MLA_QK ** -0.5 * LOG2E
    q_proj = {}

    def scores(h):
        if h % 2 == 0:
            ls2 = slice(h * hs, (h + 2) * hs)
            q_proj[h // 2] = (_dot(qn, wq_ref[:, ls2]), _dot(qn, wqr_ref[:, ls2]))
        ls = slice((h % 2) * hs, (h % 2 + 1) * hs)
        qp, qr = (a[:, ls] for a in q_proj[h // 2])
        ss = jnp.sum(qp * qp, axis=-1, keepdims=True)
        rinv = lax.rsqrt(ss * (1.0 / MLA_QK) + RMS_EPS) * scale
        q = ((qp * gcos + qr * gsin) * rinv).astype(BF16)
        return _dot_nt(q, k_scr[h])

    pair = None
    s_next = scores(0)
    for h in range(MLA_HEADS):
        s = s_next
        if h + 1 < MLA_HEADS:
            s_next = scores(h + 1)
        p = jnp.exp2(s - jnp.max(s, axis=-1, keepdims=True))
        o = _dot(p.astype(BF16), v_scr[h])
        ol = ones_lane[h % 2]
        o = jnp.where(v_mask[h % 2], o, 0.0) / o[:, ol:ol + 1]
        ms = jnp.sum(o * o, axis=-1, keepdims=True) * (1.0 / MLA_V)
        on = o * lax.rsqrt(ms + RMS_EPS)
        if h % 2 == 0:
            pair = on
        else:
            ps = slice((h // 2) * hs, (h // 2 + 1) * hs)
            o_ref[:, ps] = ((pair + on) * go_ref[:, ps]).astype(o_ref.dtype)


def _mla(u3, misc, cos, sin, qn, wq, wqr, kvn, wk, wv, gq, gk, go, pp, pr, tq):
    bsz, seq, _ = u3.shape
    full2 = lambda a: _resident(a.shape, lambda b, i: (0, 0))
    seq_blk = lambda w, idx: pl.BlockSpec((None, seq, w), lambda b, i: (b, 0, idx))
    q_blk = lambda w, idx: pl.BlockSpec((None, tq, w), lambda b, i: (b, i, idx))
    return pl.pallas_call(
        _mla_kernel,
        grid=(bsz, seq // tq),
        in_specs=[
            q_blk(MLA_Q_LORA, U_QLAT // MLA_Q_LORA),
            seq_blk(MLA_KV_LORA, U_KVLAT // MLA_KV_LORA),
            seq_blk(LANES, 0),
            seq_blk(LANES, 0), seq_blk(LANES, 0),
            q_blk(LANES, 0), q_blk(LANES, 0),
            full2(qn), full2(wq), full2(wqr), full2(kvn), full2(wk), full2(wv),
            full2(gq), full2(gk), full2(go), full2(pp), full2(pr),
        ],
        out_specs=pl.BlockSpec((None, tq, MLA_WIDTH), lambda b, i: (b, i, 0)),
        out_shape=jax.ShapeDtypeStruct((bsz, seq, MLA_WIDTH), BF16),
        scratch_shapes=[
            pltpu.VMEM((MLA_HEADS, seq, LANES), BF16),
            pltpu.VMEM((MLA_HEADS, seq, LANES), BF16),
        ],
        compiler_params=_cparams(("parallel", "arbitrary")),
        name="mla",
    )(u3, u3, misc, cos, sin, cos, sin, qn, wq, wqr, kvn, wk, wv, gq, gk, go, pp, pr)


def _gconv_kernel(h_ref, b_ref, c_ref, w_ref, g_ref, gm_ref, o_ref, pad):
    seq = o_ref.shape[0]
    half = CONV_K // 2
    zeros = jnp.zeros((SUBLANES, LANES), F32)
    pad[0:SUBLANES, :] = zeros
    pad[SUBLANES + seq:2 * SUBLANES + seq, :] = zeros
    pad[SUBLANES:SUBLANES + seq, :] = c_ref[...].astype(F32) * h_ref[...].astype(F32)
    w = w_ref[...]
    gm = gm_ref[...]
    rows = 256
    gsz = CONV_WIDTH // CONV_GROUPS
    for c in range(seq // rows):
        acc = jnp.zeros((rows, LANES), F32)
        for k in range(CONV_K):
            r0 = SUBLANES - half + k + c * rows
            acc = acc + w[k:k + 1, :] * pad[r0:r0 + rows, :]
        rs = slice(c * rows, (c + 1) * rows)
        y = acc * b_ref[rs, :].astype(F32)
        ss = _dot_x01(y * y, gm, parts=2)
        o_ref[rs, :] = (y * lax.rsqrt(ss * (1.0 / gsz) + RMS_EPS) * g_ref[...]).astype(o_ref.dtype)


def _gconv(u3, w_ext, g, gm):
    bsz, seq, _ = u3.shape
    nj = CONV_WIDTH // LANES
    blk = lambda off: pl.BlockSpec((None, seq, LANES), lambda b, j: (b, 0, off // LANES + j))
    return pl.pallas_call(
        _gconv_kernel,
        grid=(bsz, nj),
        in_specs=[
            blk(U_CH), blk(U_CB), blk(U_CC),
            pl.BlockSpec((SUBLANES, LANES), lambda b, j: (0, j)),
            pl.BlockSpec((1, LANES), lambda b, j: (0, j)),
            pl.BlockSpec(gm.shape, lambda b, j: (0, 0)),
        ],
        out_specs=pl.BlockSpec((None, seq, LANES), lambda b, j: (b, 0, j)),
        out_shape=jax.ShapeDtypeStruct((bsz, seq, CONV_WIDTH), BF16),
        scratch_shapes=[pltpu.VMEM((seq + 2 * SUBLANES, LANES), F32)],
        compiler_params=_cparams(("parallel", "parallel")),
        name="gconv",
    )(u3, u3, u3, w_ext, g, gm)


def _rot_half_perm(n):
    h = n // 2
    idx = np.array([(j + h) % n for j in range(n)])
    sign = np.array([-1.0 if j < h else 1.0 for j in range(n)], np.float32)
    return idx, sign


def _prep_w_in(w_in):
    depth = w_in.shape[0]
    o_dt = 2 * SSD_WIDTH + 2 * SSD_GROUPS * SSD_STATE
    o_q = o_dt + 2 * SSD_HEADS
    o_ch = o_q + MLA_Q_LORA + MLA_KV_LORA + MLA_ROPE
    mid = w_in[:, :, o_dt:o_ch].astype(BF16)
    dt_cols = [mid[:, :, d * SSD_HEADS + g * SSD_HPG:d * SSD_HEADS + (g + 1) * SSD_HPG]
               for g in range(SSD_GROUPS) for d in range(2)]
    pad = jnp.zeros((depth, D_MODEL, U_CH - U_MISC - MLA_ROPE - 2 * SSD_HEADS), BF16)
    w_mid = jnp.concatenate([mid[:, :, o_q - o_dt:]] + dt_cols + [pad], axis=-1)
    return w_in[:, :, :o_dt].astype(BF16), w_mid, w_in[:, :, o_ch:].astype(BF16)


def _prep_mla(w_uq, w_ukv, q_head_norm, k_head_norm):
    depth = w_uq.shape[0]
    ridx, rsign = _rot_half_perm(MLA_ROPE)
    wq3 = w_uq.astype(BF16).reshape(depth, MLA_Q_LORA, MLA_HEADS, MLA_QK)
    zq = jnp.zeros((depth, MLA_Q_LORA, MLA_HEADS, LANES - MLA_QK), BF16)
    wq = jnp.concatenate([wq3, zq], axis=-1).reshape(depth, MLA_Q_LORA, MLA_HEADS * LANES)
    half = MLA_ROPE // 2
    rot = jnp.concatenate([-wq3[..., MLA_NOPE + half:], wq3[..., MLA_NOPE:MLA_NOPE + half]], axis=-1)
    zn = jnp.zeros((depth, MLA_Q_LORA, MLA_HEADS, MLA_NOPE), BF16)
    wqr = jnp.concatenate([zn, rot, zq], axis=-1).reshape(depth, MLA_Q_LORA, MLA_HEADS * LANES)
    wkv3 = w_ukv.astype(BF16).reshape(depth, MLA_KV_LORA, MLA_HEADS, MLA_NOPE + MLA_V)
    zk = jnp.zeros((depth, MLA_KV_LORA, MLA_HEADS, LANES - MLA_NOPE), BF16)
    wk = jnp.concatenate([wkv3[..., :MLA_NOPE], zk], axis=-1).reshape(depth, MLA_KV_LORA, MLA_HEADS * LANES)
    v4 = wkv3[..., MLA_NOPE:].reshape(depth, MLA_KV_LORA, MLA_HEADS // 2, 2, MLA_V)
    zv = jnp.zeros_like(v4[:, :, :, 0])
    wv = jnp.stack([jnp.concatenate([v4[:, :, :, 0], zv], axis=-1),
                    jnp.concatenate([zv, v4[:, :, :, 1]], axis=-1)], axis=3)
    wv = wv.reshape(depth, MLA_KV_LORA, MLA_HEADS * LANES)

    def gains(g):
        zt = jnp.zeros((depth, LANES - MLA_QK), F32)
        plain = jnp.concatenate([g, zt], axis=-1)
        g_rope = g[:, MLA_NOPE:]
        rot_g = jnp.concatenate([jnp.zeros((depth, MLA_NOPE), F32), g_rope[:, half:], g_rope[:, :half], zt],
                                axis=-1)
        rest = jnp.zeros((depth, SUBLANES - 2, LANES), F32)
        return jnp.concatenate([plain[:, None], rot_g[:, None], rest], axis=1)

    return wq, wqr, wk, wv, gains(q_head_norm), gains(k_head_norm)


def _mla_consts():
    ridx, rsign = _rot_half_perm(MLA_ROPE)
    half = MLA_ROPE // 2
    inv = ROPE_BASE ** (-jnp.arange(half, dtype=F32) / half)
    invf = jnp.concatenate([jnp.zeros((MLA_NOPE,), F32), inv, inv, jnp.zeros((LANES - MLA_QK,), F32)])[None, :]
    src = np.arange(MLA_ROPE)
    pp = np.zeros((LANES, LANES), np.float32)
    pp[MISC_KPE + src, MLA_NOPE + src] = 1.0
    pr = np.zeros((LANES, LANES), np.float32)
    pr[MISC_KPE + ridx, MLA_NOPE + src] = rsign
    return invf, jnp.asarray(pp, BF16), jnp.asarray(pr, BF16)


def _prep_ssd(ssd_conv_w, ssd_conv_b, ssd_dt_bias, ssd_a_log, ssd_d):
    depth = ssd_conv_w.shape[0]
    cw_ext = jnp.concatenate([ssd_conv_w, ssd_conv_b[:, None, :],
                              jnp.zeros((depth, SUBLANES - SSD_CONV - 1, ssd_conv_w.shape[-1]), F32)], axis=1)

    def per_group(p):
        return p.reshape(depth, 2, SSD_GROUPS, SSD_HPG).transpose(0, 2, 1, 3).reshape(depth, SSD_GROUPS, 2 * SSD_HPG)

    bias_g, alog_g = per_group(ssd_dt_bias), per_group(ssd_a_log)
    n16 = 2 * SSD_HPG
    hps = []
    for g in range(SSD_GROUPS):
        lo = MISC_DT + g * n16
        lane_pad = ((0, 0), (lo, LANES - lo - n16))
        rows = [jnp.pad(bias_g[:, g], lane_pad), jnp.pad(alog_g[:, g], lane_pad),
                jnp.pad(jnp.ones((depth, n16), F32), lane_pad)]
        rows += [jnp.zeros((depth, LANES), F32)] * (SUBLANES - len(rows))
        hps.append(jnp.stack(rows, axis=1))
    hp = jnp.stack(hps, axis=1)
    hpt = jnp.stack([bias_g, alog_g] + [jnp.zeros_like(bias_g)] * (SUBLANES - 2), axis=-1)
    dx = jnp.repeat(ssd_d.reshape(depth, SSD_GROUPS, SSD_HPG), SSD_HEADDIM, axis=-1)[:, :, None, :]
    return cw_ext, hp, hpt, dx


def _ssd_expanders():
    out = []
    for g in range(SSD_GROUPS):
        per_dir = []
        for d in range(2):
            lo = MISC_DT + g * 2 * SSD_HPG + d * SSD_HPG
            e = np.zeros((LANES, SSD_GW), np.float32)
            e[lo + np.arange(SSD_GW) // SSD_HEADDIM, np.arange(SSD_GW)] = 1.0
            per_dir.append(jnp.asarray(e, BF16))
        out.append(per_dir)
    return out


def _prep_mixers(positions, ssd_conv_w, ssd_conv_b, ssd_dt_bias, ssd_a_log, ssd_d, ssd_norm,
                 mla_q_norm, mla_w_uq, mla_kv_norm, mla_w_ukv, mla_q_head_norm, mla_k_head_norm,
                 mla_out_norm, conv_w, conv_out_norm):
    depth = conv_w.shape[0]
    bsz, seq = positions.shape
    gsz = CONV_WIDTH // CONV_GROUPS
    li = np.arange(LANES)
    invf, pp, pr = _mla_consts()
    cos, sin = _rope_tables(positions.reshape(bsz, seq, 1), invf)
    return dict(
        mla_w=_prep_mla(mla_w_uq, mla_w_ukv, mla_q_head_norm, mla_k_head_norm),
        mla_c=(cos, sin, pp, pr),
        mla_n=(mla_q_norm, mla_kv_norm, mla_out_norm),
        ssd=_prep_ssd(ssd_conv_w, ssd_conv_b, ssd_dt_bias, ssd_a_log, ssd_d),
        ssd_norm=ssd_norm,
        expanders=_ssd_expanders(),
        gc_w=jnp.concatenate([conv_w, jnp.zeros((depth, SUBLANES - CONV_K, CONV_WIDTH), F32)], axis=1),
        gc_g=conv_out_norm,
        gm=jnp.asarray((li[:, None] // gsz == li[None, :] // gsz).astype(np.float32), BF16),
    )


def _mixers(u3, misc, dtt, mp, l):
    bsz, seq, _ = u3.shape
    t = bsz * seq
    tq = 512 if seq % 512 == 0 else SSD_CHUNK
    cw_ext, hp, hpt, dx = mp["ssd"]
    ys = [_ssd_group(u3, misc, dtt, cw_ext[l], hp[l, g], hpt[l, g], dx[l, g], mp["ssd_norm"][l][None, :],
                     *mp["expanders"][g], g).reshape(t, SSD_GW)
          for g in range(SSD_GROUPS)]
    wq, wqr, wk, wv, gq, gk = mp["mla_w"]
    cos, sin, pp, pr = mp["mla_c"]
    qn, kvn, go = mp["mla_n"]
    y_mla = _mla(u3, misc, cos, sin, qn[l][None, :], wq[l], wqr[l], kvn[l][None, :], wk[l], wv[l],
                 gq[l], gk[l], go[l][None, :], pp, pr, tq).reshape(t, MLA_WIDTH)
    y_conv = _gconv(u3, mp["gc_w"][l], mp["gc_g"][l][None, :], mp["gm"]).reshape(t, CONV_WIDTH)
    return ys + [y_mla, y_conv]


def kernel(x, positions, ffn1_norm, ffn1_w_gate, ffn1_w_up, ffn1_w_down, mix_norm, w_in, ssd_conv_w, ssd_conv_b, ssd_dt_bias, ssd_a_log, ssd_d, ssd_norm, mla_q_norm, mla_w_uq, mla_kv_norm, mla_w_ukv, mla_q_head_norm, mla_k_head_norm, mla_out_norm, conv_w, conv_out_norm, w_out, ffn2_norm, ffn2_w_gate, ffn2_w_up, ffn2_w_down):
    bsz, seq, _ = x.shape
    depth = w_in.shape[0]
    t = bsz * seq
    assert seq % 256 == 0
    tm = 1024 if t % 1024 == 0 else 256
    ts = 512 if seq % 512 == 0 else 256

    row = lambda a: a[:, None, :]
    f1 = (row(ffn1_norm), ffn1_w_gate.astype(BF16), ffn1_w_up.astype(BF16), (0.5 * ffn1_w_down).astype(BF16))
    f2 = (row(ffn2_norm), ffn2_w_gate.astype(BF16), ffn2_w_up.astype(BF16), (0.5 * ffn2_w_down).astype(BF16))
    w_in_p = _prep_w_in(w_in)
    w_out_b = w_out.astype(BF16)
    mp = _prep_mixers(positions, ssd_conv_w, ssd_conv_b, ssd_dt_bias, ssd_a_log, ssd_d, ssd_norm,
                      mla_q_norm, mla_w_uq, mla_kv_norm, mla_w_ukv, mla_q_head_norm, mla_k_head_norm,
                      mla_out_norm, conv_w, conv_out_norm)

    xt = x.reshape(t, D_MODEL)
    for l in range(depth):
        xt = _ffn(xt, *f1, l, tm)
        u3, misc, dtt = _inproj(xt.reshape(bsz, seq, D_MODEL), row(mix_norm), w_in_p, l, ts)
        ys = _mixers(u3, misc, dtt, mp, l)
        xt = _outproj(xt, ys, w_out_b, l, tm)
        xt = _ffn(xt, *f2, l, tm)
    return xt.reshape(bsz, seq, D_MODEL)
```

```python
import functools
import math

import numpy as np
import jax
import jax.numpy as jnp
from jax import lax
from jax.experimental import pallas as pl
from jax.experimental.pallas import tpu as pltpu

F32 = jnp.float32
BF16 = jnp.bfloat16

D_MODEL = 1024
D_FF = 2816
RMS_EPS = 1e-6

SSD_WIDTH = 1024
SSD_HEADDIM = 64
SSD_HEADS = 16
SSD_GROUPS = 2
SSD_HPG = 8
SSD_STATE = 128
SSD_CONV = 5
SSD_CHUNK = 128
SSD_GW = SSD_HPG * SSD_HEADDIM

MLA_HEADS = 8
MLA_Q_LORA = 256
MLA_KV_LORA = 128
MLA_NOPE = 64
MLA_ROPE = 32
MLA_QK = 96
MLA_V = 64
MLA_WIDTH = 512
ROPE_BASE = 10000.0

CONV_WIDTH = 512
CONV_GROUPS = 8
CONV_K = 3

D_MIX = 2048
LANES = 128
SUBLANES = 8

U_Z = 0
U_XS = 1024
U_B = 2048
U_C = 2304
U_QLAT = 2560
U_KVLAT = 2816
U_MISC = 2944
U_CH = 3072
U_CB = 3584
U_CC = 4096
U_WIDTH = 4608
MISC_KPE = 0
MISC_DT = 32

NEG_BIG = -1e30
LOG2E = math.log2(math.e)
VMEM_LIMIT = 56 * 1024 * 1024
FFN_TF = 256
P1_GROUP = 4
P2_GROUP = 4
GCONV_BLOCK = 256


def _cparams(sem):
    return pltpu.CompilerParams(dimension_semantics=sem, vmem_limit_bytes=VMEM_LIMIT)


def _resident(block_shape, index_map):
    return pl.BlockSpec(block_shape, index_map, pipeline_mode=pl.Buffered(1))


def _dot(a, b):
    return jnp.dot(a, b, preferred_element_type=F32)


def _dot_nt(a, b):
    return lax.dot_general(a, b, (((1,), (1,)), ((), ())), preferred_element_type=F32)


def _split(v, parts):
    out = []
    r = v
    for i in range(parts):
        t = r.astype(BF16)
        out.append(t)
        if i + 1 < parts:
            r = r - t.astype(F32)
    return out


def _dot_x01(v, m01, parts=3):
    acc = None
    for t in _split(v, parts):
        d = _dot(t, m01)
        acc = d if acc is None else acc + d
    return acc


def _dot_01x(m01, v, parts=3):
    acc = None
    for t in _split(v, parts):
        d = _dot(m01, t)
        acc = d if acc is None else acc + d
    return acc


def _silu(x):
    return x * jax.nn.sigmoid(x)


def _softplus(x):
    return jnp.maximum(x, 0.0) + jnp.log1p(jnp.exp(-jnp.abs(x)))


def _rms(x, g, n=None):
    n = x.shape[-1] if n is None else n
    ms = jnp.sum(x * x, axis=-1, keepdims=True) * (1.0 / n)
    return x * lax.rsqrt(ms + RMS_EPS) * g


def _ffn_kernel(x_ref, g_ref, wg_ref, wu_ref, wd_ref, o_ref, xn_ref):
    x = x_ref[...]
    xn_ref[...] = _rms(x, g_ref[...]).astype(BF16)
    o_ref[...] = x

    def body(f, carry):
        cs = pl.ds(pl.multiple_of(f * FFN_TF, FFN_TF), FFN_TF)
        xn = xn_ref[...]
        h = _silu(_dot(xn, wg_ref[:, cs])) * _dot(xn, wu_ref[:, cs])
        o_ref[...] += _dot(h.astype(BF16), wd_ref[cs, :])
        return carry

    lax.fori_loop(0, D_FF // FFN_TF, body, 0, unroll=True)


def _ffn(x, norm, wg, wu, wd_half, layer, tm):
    t = x.shape[0]
    return pl.pallas_call(
        _ffn_kernel,
        grid=(t // tm,),
        in_specs=[
            pl.BlockSpec((tm, D_MODEL), lambda i: (i, 0)),
            _resident((None, 1, D_MODEL), lambda i: (layer, 0, 0)),
            _resident((None, D_MODEL, D_FF), lambda i: (layer, 0, 0)),
            _resident((None, D_MODEL, D_FF), lambda i: (layer, 0, 0)),
            _resident((None, D_FF, D_MODEL), lambda i: (layer, 0, 0)),
        ],
        out_specs=pl.BlockSpec((tm, D_MODEL), lambda i: (i, 0)),
        out_shape=jax.ShapeDtypeStruct((t, D_MODEL), F32),
        scratch_shapes=[pltpu.VMEM((tm, D_MODEL), BF16)],
        compiler_params=_cparams(("parallel",)),
        name="ffn",
    )(x, norm, wg, wu, wd_half)


def _inproj_kernel(x_ref, g_ref, wa_ref, wm_ref, wc_ref, u_ref, dtt_ref):
    xn = _rms(x_ref[...], g_ref[...]).astype(BF16)
    u_ref[:, :U_QLAT] = _dot(xn, wa_ref[...])
    u_ref[:, U_QLAT:U_CH] = _dot(xn, wm_ref[...])
    u_ref[:, U_CH:] = _dot(xn, wc_ref[...])
    misc_t = u_ref[:, U_MISC:U_MISC + LANES].T
    dtt_ref[...] = misc_t[MISC_DT:MISC_DT + 2 * SSD_HEADS, :]


def _inproj(x3, norm, w_parts, layer, ts):
    bsz, seq, _ = x3.shape
    w_specs = [_resident((None, D_MODEL, w.shape[-1]), lambda b, i: (layer, 0, 0)) for w in w_parts]
    return pl.pallas_call(
        _inproj_kernel,
        grid=(bsz, seq // ts),
        in_specs=[
            pl.BlockSpec((None, ts, D_MODEL), lambda b, i: (b, i, 0)),
            _resident((None, 1, D_MODEL), lambda b, i: (layer, 0, 0)),
        ] + w_specs,
        out_specs=[
            pl.BlockSpec((None, ts, U_WIDTH), lambda b, i: (b, i, 0)),
            pl.BlockSpec((None, 2 * SSD_HEADS, ts), lambda b, i: (b, 0, i)),
        ],
        out_shape=[
            jax.ShapeDtypeStruct((bsz, seq, U_WIDTH), F32),
            jax.ShapeDtypeStruct((bsz, 2 * SSD_HEADS, seq), F32),
        ],
        compiler_params=_cparams(("parallel", "parallel")),
        name="inproj",
    )(x3, norm, *w_parts)


def _outproj_kernel(x_ref, y0_ref, y1_ref, y2_ref, y3_ref, w0_ref, w1_ref, w2_ref, w3_ref, o_ref):
    acc = _dot(y0_ref[...], w0_ref[...])
    acc += _dot(y1_ref[...], w1_ref[...])
    acc += _dot(y2_ref[...], w2_ref[...])
    acc += _dot(y3_ref[...], w3_ref[...])
    o_ref[...] = x_ref[...] + acc


def _outproj(x, ys, w, layer, tm):
    t = x.shape[0]
    wd = D_MIX // len(ys)
    y_spec = pl.BlockSpec((tm, wd), lambda i: (i, 0))
    w_specs = [_resident((None, wd, D_MODEL), functools.partial(lambda k, i: (layer, k, 0), k))
               for k in range(len(ys))]
    return pl.pallas_call(
        _outproj_kernel,
        grid=(t // tm,),
        in_specs=[pl.BlockSpec((tm, D_MODEL), lambda i: (i, 0))] + [y_spec] * len(ys) + w_specs,
        out_specs=pl.BlockSpec((tm, D_MODEL), lambda i: (i, 0)),
        out_shape=jax.ShapeDtypeStruct((t, D_MODEL), F32),
        compiler_params=_cparams(("parallel",)),
        name="outproj",
    )(x, *ys, *([w] * len(ys)))


def _conv_silu_block(src_ref, w, c, ls, seq):
    rows = SSD_CHUNK
    half = SSD_CONV // 2
    r0 = pl.multiple_of(c * rows, rows)
    before = pl.multiple_of(jnp.maximum(r0 - SUBLANES, 0), SUBLANES)
    after = pl.multiple_of(jnp.minimum(r0 + rows, seq - SUBLANES), SUBLANES)
    prev8 = jnp.where(c > 0, src_ref[pl.ds(before, SUBLANES), ls], 0.0)
    next8 = jnp.where(c < seq // rows - 1, src_ref[pl.ds(after, SUBLANES), ls], 0.0)
    cat = jnp.concatenate([prev8, src_ref[pl.ds(r0, rows), ls], next8], axis=0)
    acc = jnp.broadcast_to(w[SSD_CONV:SSD_CONV + 1, ls], (rows, LANES))
    for k in range(SSD_CONV):
        shifted = cat if k == half else pltpu.roll(cat, (half - k) % cat.shape[0], axis=0)
        acc = acc + w[k:k + 1, ls] * shifted[SUBLANES:SUBLANES + rows]
    return _silu(acc)


def _ssd_kernel(dt_lane, z_ref, xs_ref, b_ref, c_ref, misc_ref, dtt_ref, cwx_ref, cwb_ref, cwc_ref,
                hp_ref, hpt_ref, dx_ref, ng_ref, ef_ref, eb_ref, o_ref,
                xc, bc, cc, bt, yacc, dts, dtat, ldtt, sf, sb, ecf, ecb, decf, decb, st):
    seq = o_ref.shape[0]
    ch = SSD_CHUNK
    nc = seq // ch
    hpg = SSD_HPG
    fl = dt_lane
    bl = dt_lane + hpg

    wx, wb, wc = cwx_ref[...], cwb_ref[...], cwc_ref[...]
    one_block = slice(0, LANES)

    def chunk_rows(c):
        return pl.ds(pl.multiple_of(c * ch, ch), ch)

    def conv_chunk(c):
        rows = chunk_rows(c)
        for j in range(SSD_GW // LANES):
            ls = slice(j * LANES, (j + 1) * LANES)
            xc[rows, ls] = _conv_silu_block(xs_ref, wx, c, ls, seq)
        bv = _conv_silu_block(b_ref, wb, c, one_block, seq)
        bc[rows, :] = bv.astype(BF16)
        bt[:, rows] = bv.T.astype(BF16)
        cc[rows, :] = _conv_silu_block(c_ref, wc, c, one_block, seq).astype(BF16)

    def conv_first(c, carry):
        conv_chunk(c)
        return carry

    lax.fori_loop(0, nc, conv_first, 0)

    hp = hp_ref[...]
    bias_row = hp[0:1, :]
    a_row = -jnp.exp(hp[1:2, :]) * hp[2:3, :] * LOG2E
    hpt = hpt_ref[...]
    dts[...] = _softplus(misc_ref[...] + bias_row)
    dtt = _softplus(dtt_ref[...] + hpt[:, 0:1])
    dtat[...] = dtt * (-jnp.exp(hpt[:, 1:2]) * LOG2E)
    ldtt[...] = jnp.log(dtt) * LOG2E
    st[...] = jnp.zeros_like(st)
    lane_lo = lax.broadcasted_iota(jnp.int32, (ch, SSD_GW), 1) % LANES < SSD_HEADDIM

    row_i = lax.broadcasted_iota(jnp.int32, (ch, ch), 0)
    col_i = lax.broadcasted_iota(jnp.int32, (ch, ch), 1)
    tri_le = col_i <= row_i
    tri_ge = col_i >= row_i
    t_le = jnp.where(tri_le, 1.0, 0.0).astype(BF16)
    t_ge = jnp.where(tri_ge, 1.0, 0.0).astype(BF16)
    ef = ef_ref[...]
    eb = eb_ref[...]

    g1 = math.gcd(nc, P1_GROUP)
    g2 = math.gcd(nc, P2_GROUP)

    def pass1(i, carry):
        cs = [i * g1 + k for k in range(g1)]
        rws = [chunk_rows(c) for c in cs]
        xs_ = [xc[r, :] for r in rws]
        bts = [bt[:, r] for r in rws]
        dts_ = [dts[r, :] for r in rws]
        dtas = [dt * a_row for dt in dts_]
        cum_fs = [_dot_01x(t_le, dta) for dta in dtas]
        cumt_fs = [_dot_x01(dtat[:, r], t_ge) for r in rws]
        cbs = [_dot_nt(cc[r, :], bc[r, :]) for r in rws]
        tots = [cf[ch - 1:ch, :] for cf in cum_fs]
        cum_bs = [tot - cf + dta for tot, cf, dta in zip(tots, cum_fs, dtas)]
        spreads = []
        for dt, tot, cf, cbw in zip(dts_, tots, cum_fs, cum_bs):
            spreads.append([_dot(jnp.concatenate([jnp.exp2(cum), dt * jnp.exp2(tot - cum)], axis=0).astype(BF16), e)
                            for cum, e in ((cf, ef), (cbw, eb))])
        for k, (c, r) in enumerate(zip(cs, rws)):
            x = xs_[k]
            bt_c = bts[k]
            dec8 = jnp.broadcast_to(jnp.exp2(tots[k]), (SUBLANES, LANES))
            for spread, e, ec_ref, s_ref, dec_ref in ((spreads[k][0], ef, ecf, sf, decf),
                                                      (spreads[k][1], eb, ecb, sb, decb)):
                ec_ref[c] = spread[0:ch].astype(BF16)
                s_ref[c] = _dot(bt_c, (x * spread[ch:]).astype(BF16)).astype(BF16)
                dec_ref[c] = _dot_x01(dec8, e, parts=2)
        for k, (c, r) in enumerate(zip(cs, rws)):
            x = xs_[k]
            dtat_c = dtat[:, r]
            ldt = ldtt[:, r]
            rf = cumt_fs[k] - ldt
            rb = cumt_fs[k][:, ch - 1:ch] - cumt_fs[k] + dtat_c - ldt
            zero = jnp.zeros_like(x)
            x2 = jnp.concatenate([jnp.where(lane_lo, x, zero), jnp.where(lane_lo, zero, x)],
                                 axis=0).astype(BF16)
            cb, cum_f, cum_b = cbs[k], cum_fs[k], cum_bs[k]
            for p in range(hpg // 2):
                ls = slice(p * LANES, (p + 1) * LANES)
                ms = []
                for h in (2 * p, 2 * p + 1):
                    seg_f = cum_f[:, fl + h:fl + h + 1] - rf[h:h + 1, :]
                    seg_b = cum_b[:, bl + h:bl + h + 1] - rb[hpg + h:hpg + h + 1, :]
                    ms.append(cb * (jnp.exp2(jnp.where(tri_le, seg_f, NEG_BIG))
                                    + jnp.exp2(jnp.where(tri_ge, seg_b, NEG_BIG))))
                m = jnp.concatenate(ms, axis=1).astype(BF16)
                yacc[r, ls] = _dot(m, x2[:, ls])
        return carry

    lax.fori_loop(0, nc // g1, pass1, 0)

    def pass2(i, carry):
        work = []
        for d, ec_ref, s_ref, dec_ref in ((0, ecf, sf, decf), (1, ecb, sb, decb)):
            prev = st[d]
            for k in range(g2):
                step = i * g2 + k
                c = step if d == 0 else nc - 1 - step
                work.append((c, ec_ref, prev.astype(BF16)))
                prev = prev * dec_ref[c][0:1, :] + s_ref[c].astype(F32)
            st[d] = prev
        offs = [_dot(cc[chunk_rows(c), :], prev_b) for c, _, prev_b in work]
        for (c, ec_ref, _), off in zip(work, offs):
            yacc[chunk_rows(c), :] += off * ec_ref[c].astype(F32)
        return carry

    lax.fori_loop(0, nc // g2, pass2, 0)

    d_x = dx_ref[...]
    gain = ng_ref[...]

    def pass3(c, carry):
        rows = chunk_rows(c)
        y = (yacc[rows, :] + xc[rows, :] * d_x) * _silu(z_ref[rows, :])
        o_ref[rows, :] = _rms(y, gain).astype(o_ref.dtype)
        return carry

    lax.fori_loop(0, nc, pass3, 0, unroll=2)


def _ssd_group(u3, dtt, cw_ext, hp, hpt, dx, ng, ef, eb, group):
    bsz, seq, _ = u3.shape
    gw = SSD_GW
    n = SSD_STATE
    dt_lane = MISC_DT + group * 2 * SSD_HPG
    nchunk = seq // SSD_CHUNK
    blk = lambda w, idx: pl.BlockSpec((None, seq, w), lambda b: (b, 0, idx))
    full2 = lambda a: _resident(a.shape, lambda b: (0, 0))
    return pl.pallas_call(
        functools.partial(_ssd_kernel, dt_lane),
        grid=(bsz,),
        in_specs=[
            blk(gw, U_Z // gw + group),
            blk(gw, U_XS // gw + group),
            blk(n, U_B // n + group),
            blk(n, U_C // n + group),
            blk(LANES, U_MISC // LANES),
            pl.BlockSpec((None, 2 * SSD_HPG, seq), lambda b: (b, group, 0)),
            _resident((SUBLANES, gw), lambda b: (0, group)),
            _resident((SUBLANES, n), lambda b: (0, SSD_WIDTH // n + group)),
            _resident((SUBLANES, n), lambda b: (0, SSD_WIDTH // n + SSD_GROUPS + group)),
            full2(hp), full2(hpt), full2(dx),
            _resident((1, gw), lambda b: (0, group)),
            full2(ef), full2(eb),
        ],
        out_specs=pl.BlockSpec((None, seq, gw), lambda b: (b, 0, 0)),
        out_shape=jax.ShapeDtypeStruct((bsz, seq, gw), BF16),
        scratch_shapes=[
            pltpu.VMEM((seq, gw), F32),
            pltpu.VMEM((seq, n), BF16),
            pltpu.VMEM((seq, n), BF16),
            pltpu.VMEM((n, seq), BF16),
            pltpu.VMEM((seq, gw), F32),
            pltpu.VMEM((seq, LANES), F32),
            pltpu.VMEM((2 * SSD_HPG, seq), F32),
            pltpu.VMEM((2 * SSD_HPG, seq), F32),
            pltpu.VMEM((nchunk, n, gw), BF16),
            pltpu.VMEM((nchunk, n, gw), BF16),
            pltpu.VMEM((nchunk, SSD_CHUNK, gw), BF16),
            pltpu.VMEM((nchunk, SSD_CHUNK, gw), BF16),
            pltpu.VMEM((nchunk, SUBLANES, gw), F32),
            pltpu.VMEM((nchunk, SUBLANES, gw), F32),
            pltpu.VMEM((2, n, gw), F32),
        ],
        compiler_params=_cparams(("parallel",)),
        name=f"ssd_g{group}",
    )(u3, u3, u3, u3, u3, dtt, cw_ext, cw_ext, cw_ext, hp, hpt, dx, ng, ef, eb)


def _rope_kernel(pos_ref, invf_ref, cos_ref, sin_ref):
    ang = pos_ref[...].astype(F32) * invf_ref[...]
    cos_ref[...] = jnp.cos(ang)
    sin_ref[...] = jnp.sin(ang)


def _rope_tables(pos3, invf):
    bsz, seq, _ = pos3.shape
    out = jax.ShapeDtypeStruct((bsz, seq, LANES), F32)
    spec = pl.BlockSpec((None, seq, LANES), lambda b: (b, 0, 0))
    return pl.pallas_call(
        _rope_kernel,
        grid=(bsz,),
        in_specs=[pl.BlockSpec((None, seq, 1), lambda b: (b, 0, 0)),
                  pl.BlockSpec(invf.shape, lambda b: (0, 0))],
        out_specs=[spec, spec],
        out_shape=[out, out],
        compiler_params=_cparams(("parallel",)),
        name="rope_tables",
    )(pos3, invf)


def _mla_kernel(qlat_ref, kvlat_ref, misc_ref, cosk_ref, sink_ref, cosq_ref, sinq_ref, qn_ref, wq_ref,
                wqr_ref, kvn_ref, wk_ref, wv_ref, gq_ref, gk_ref, go_ref, pp_ref, pr_ref, o_ref,
                k_scr, v_scr):
    hs = LANES
    lane = lax.broadcasted_iota(jnp.int32, (1, hs), 1)
    ones_lane = (MLA_V, 0)
    v_mask = (lane < MLA_V, lane >= MLA_V)

    @pl.when(pl.program_id(1) == 0)
    def _():
        kvn = _rms(kvlat_ref[...], kvn_ref[...]).astype(BF16)
        hi, mid = _split(misc_ref[...], 2)
        placed = jnp.concatenate([pp_ref[...], pr_ref[...]], axis=1)
        both = _dot(hi, placed) + _dot(mid, placed)
        pe = both[:, :hs]
        pe_rot = both[:, hs:]
        gk = gk_ref[...]
        kr = pe * gk[0:1, :] * cosk_ref[...] + pe_rot * gk[1:2, :] * sink_ref[...]
        ss_pe = jnp.sum(pe * pe, axis=-1, keepdims=True)
        for hp in range(MLA_HEADS // 2):
            ls2 = slice(2 * hp * hs, (2 * hp + 2) * hs)
            kp2 = _dot(kvn, wk_ref[:, ls2])
            v2 = _dot(kvn, wv_ref[:, ls2])
            for e in range(2):
                h = 2 * hp + e
                kp = kp2[:, e * hs:(e + 1) * hs]
                ss = jnp.sum(kp * kp, axis=-1, keepdims=True) + ss_pe
                rinv = lax.rsqrt(ss * (1.0 / MLA_QK) + RMS_EPS)
                k_scr[h] = ((kp * gk[0:1, :] + kr) * rinv).astype(BF16)
                ones_row = jnp.where(lane == ones_lane[e], 1.0, 0.0)
                v_scr[h] = (v2[:, e * hs:(e + 1) * hs] + ones_row).astype(BF16)

    qn = _rms(qlat_ref[...], qn_ref[...]).astype(BF16)
    gq = gq_ref[...]
    gcos = gq[0:1, :] * cosq_ref[...]
    gsin = gq[1:2, :] * sinq_ref[...]
    scale = MLA_QK ** -0.5 * LOG2E
    q_proj = {}

    def scores(h):
        if h % 2 == 0:
            ls2 = slice(h * hs, (h + 2) * hs)
            q_proj[h // 2] = (_dot(qn, wq_ref[:, ls2]), _dot(qn, wqr_ref[:, ls2]))
        ls = slice((h % 2) * hs, (h % 2 + 1) * hs)
        qp, qr = (a[:, ls] for a in q_proj[h // 2])
        ss = jnp.sum(qp * qp, axis=-1, keepdims=True)
        rinv = lax.rsqrt(ss * (1.0 / MLA_QK) + RMS_EPS) * scale
        q = ((qp * gcos + qr * gsin) * rinv).astype(BF16)
        return _dot_nt(q, k_scr[h])

    pair = None
    s_next = scores(0)
    for h in range(MLA_HEADS):
        s = s_next
        if h + 1 < MLA_HEADS:
            s_next = scores(h + 1)
        p = jnp.exp2(s - jnp.max(s, axis=-1, keepdims=True))
        o = _dot(p.astype(BF16), v_scr[h])
        ol = ones_lane[h % 2]
        o = jnp.where(v_mask[h % 2], o, 0.0) / o[:, ol:ol + 1]
        ms = jnp.sum(o * o, axis=-1, keepdims=True) * (1.0 / MLA_V)
        on = o * lax.rsqrt(ms + RMS_EPS)
        if h % 2 == 0:
            pair = on
        else:
            ps = slice((h // 2) * hs, (h // 2 + 1) * hs)
            o_ref[:, ps] = ((pair + on) * go_ref[:, ps]).astype(o_ref.dtype)


def _mla(u3, cos, sin, qn, wq, wqr, kvn, wk, wv, gq, gk, go, pp, pr, tq):
    bsz, seq, _ = u3.shape
    full2 = lambda a: _resident(a.shape, lambda b, i: (0, 0))
    seq_blk = lambda w, idx: pl.BlockSpec((None, seq, w), lambda b, i: (b, 0, idx))
    q_blk = lambda w, idx: pl.BlockSpec((None, tq, w), lambda b, i: (b, i, idx))
    return pl.pallas_call(
        _mla_kernel,
        grid=(bsz, seq // tq),
        in_specs=[
            q_blk(MLA_Q_LORA, U_QLAT // MLA_Q_LORA),
            seq_blk(MLA_KV_LORA, U_KVLAT // MLA_KV_LORA),
            seq_blk(LANES, U_MISC // LANES),
            seq_blk(LANES, 0), seq_blk(LANES, 0),
            q_blk(LANES, 0), q_blk(LANES, 0),
            full2(qn), full2(wq), full2(wqr), full2(kvn), full2(wk), full2(wv),
            full2(gq), full2(gk), full2(go), full2(pp), full2(pr),
        ],
        out_specs=pl.BlockSpec((None, tq, MLA_WIDTH), lambda b, i: (b, i, 0)),
        out_shape=jax.ShapeDtypeStruct((bsz, seq, MLA_WIDTH), BF16),
        scratch_shapes=[
            pltpu.VMEM((MLA_HEADS, seq, LANES), BF16),
            pltpu.VMEM((MLA_HEADS, seq, LANES), BF16),
        ],
        compiler_params=_cparams(("parallel", "arbitrary")),
        name="mla",
    )(u3, u3, u3, cos, sin, cos, sin, qn, wq, wqr, kvn, wk, wv, gq, gk, go, pp, pr)


def _gconv_kernel(h_ref, b_ref, c_ref, w_ref, g_ref, gm_ref, o_ref, pad):
    seq = o_ref.shape[0]
    half = CONV_K // 2
    gsz = CONV_WIDTH // CONV_GROUPS
    gm = gm_ref[...]
    rows = 256
    for j in range(GCONV_BLOCK // LANES):
        ls = slice(j * LANES, (j + 1) * LANES)
        zeros = jnp.zeros((SUBLANES, LANES), F32)
        pad[0:SUBLANES, :] = zeros
        pad[SUBLANES + seq:2 * SUBLANES + seq, :] = zeros
        pad[SUBLANES:SUBLANES + seq, :] = c_ref[:, ls] * h_ref[:, ls]
        w = w_ref[:, ls]
        for c in range(seq // rows):
            acc = jnp.zeros((rows, LANES), F32)
            for k in range(CONV_K):
                r0 = SUBLANES - half + k + c * rows
                acc = acc + w[k:k + 1, :] * pad[r0:r0 + rows, :]
            rs = slice(c * rows, (c + 1) * rows)
            y = acc * b_ref[rs, ls]
            ss = _dot_x01(y * y, gm, parts=2)
            o_ref[rs, ls] = (y * lax.rsqrt(ss * (1.0 / gsz) + RMS_EPS) * g_ref[:, ls]).astype(o_ref.dtype)


def _gconv(u3, w_ext, g, gm):
    bsz, seq, _ = u3.shape
    bw = GCONV_BLOCK
    nj = CONV_WIDTH // bw
    blk = lambda off: pl.BlockSpec((None, seq, bw), lambda b, j: (b, 0, off // bw + j))
    return pl.pallas_call(
        _gconv_kernel,
        grid=(bsz, nj),
        in_specs=[
            blk(U_CH), blk(U_CB), blk(U_CC),
            pl.BlockSpec((SUBLANES, bw), lambda b, j: (0, j)),
            pl.BlockSpec((1, bw), lambda b, j: (0, j)),
            pl.BlockSpec(gm.shape, lambda b, j: (0, 0)),
        ],
        out_specs=pl.BlockSpec((None, seq, bw), lambda b, j: (b, 0, j)),
        out_shape=jax.ShapeDtypeStruct((bsz, seq, CONV_WIDTH), BF16),
        scratch_shapes=[pltpu.VMEM((seq + 2 * SUBLANES, LANES), F32)],
        compiler_params=_cparams(("parallel", "parallel")),
        name="gconv",
    )(u3, u3, u3, w_ext, g, gm)


def _rot_half_perm(n):
    h = n // 2
    idx = np.array([(j + h) % n for j in range(n)])
    sign = np.array([-1.0 if j < h else 1.0 for j in range(n)], np.float32)
    return idx, sign


def _prep_w_in(w_in):
    depth = w_in.shape[0]
    o_dt = 2 * SSD_WIDTH + 2 * SSD_GROUPS * SSD_STATE
    o_q = o_dt + 2 * SSD_HEADS
    o_ch = o_q + MLA_Q_LORA + MLA_KV_LORA + MLA_ROPE
    mid = w_in[:, :, o_dt:o_ch].astype(BF16)
    dt_cols = [mid[:, :, d * SSD_HEADS + g * SSD_HPG:d * SSD_HEADS + (g + 1) * SSD_HPG]
               for g in range(SSD_GROUPS) for d in range(2)]
    pad = jnp.zeros((depth, D_MODEL, U_CH - U_MISC - MLA_ROPE - 2 * SSD_HEADS), BF16)
    w_mid = jnp.concatenate([mid[:, :, o_q - o_dt:]] + dt_cols + [pad], axis=-1)
    return w_in[:, :, :o_dt].astype(BF16), w_mid, w_in[:, :, o_ch:].astype(BF16)


def _prep_mla(w_uq, w_ukv, q_head_norm, k_head_norm):
    depth = w_uq.shape[0]
    ridx, rsign = _rot_half_perm(MLA_ROPE)
    wq3 = w_uq.astype(BF16).reshape(depth, MLA_Q_LORA, MLA_HEADS, MLA_QK)
    zq = jnp.zeros((depth, MLA_Q_LORA, MLA_HEADS, LANES - MLA_QK), BF16)
    wq = jnp.concatenate([wq3, zq], axis=-1).reshape(depth, MLA_Q_LORA, MLA_HEADS * LANES)
    half = MLA_ROPE // 2
    rot = jnp.concatenate([-wq3[..., MLA_NOPE + half:], wq3[..., MLA_NOPE:MLA_NOPE + half]], axis=-1)
    zn = jnp.zeros((depth, MLA_Q_LORA, MLA_HEADS, MLA_NOPE), BF16)
    wqr = jnp.concatenate([zn, rot, zq], axis=-1).reshape(depth, MLA_Q_LORA, MLA_HEADS * LANES)
    wkv3 = w_ukv.astype(BF16).reshape(depth, MLA_KV_LORA, MLA_HEADS, MLA_NOPE + MLA_V)
    zk = jnp.zeros((depth, MLA_KV_LORA, MLA_HEADS, LANES - MLA_NOPE), BF16)
    wk = jnp.concatenate([wkv3[..., :MLA_NOPE], zk], axis=-1).reshape(depth, MLA_KV_LORA, MLA_HEADS * LANES)
    v4 = wkv3[..., MLA_NOPE:].reshape(depth, MLA_KV_LORA, MLA_HEADS // 2, 2, MLA_V)
    zv = jnp.zeros_like(v4[:, :, :, 0])
    wv = jnp.stack([jnp.concatenate([v4[:, :, :, 0], zv], axis=-1),
                    jnp.concatenate([zv, v4[:, :, :, 1]], axis=-1)], axis=3)
    wv = wv.reshape(depth, MLA_KV_LORA, MLA_HEADS * LANES)

    def gains(g):
        zt = jnp.zeros((depth, LANES - MLA_QK), F32)
        plain = jnp.concatenate([g, zt], axis=-1)
        g_rope = g[:, MLA_NOPE:]
        rot_g = jnp.concatenate([jnp.zeros((depth, MLA_NOPE), F32), g_rope[:, half:], g_rope[:, :half], zt],
                                axis=-1)
        rest = jnp.zeros((depth, SUBLANES - 2, LANES), F32)
        return jnp.concatenate([plain[:, None], rot_g[:, None], rest], axis=1)

    return wq, wqr, wk, wv, gains(q_head_norm), gains(k_head_norm)


def _mla_consts():
    ridx, rsign = _rot_half_perm(MLA_ROPE)
    half = MLA_ROPE // 2
    inv = ROPE_BASE ** (-jnp.arange(half, dtype=F32) / half)
    invf = jnp.concatenate([jnp.zeros((MLA_NOPE,), F32), inv, inv, jnp.zeros((LANES - MLA_QK,), F32)])[None, :]
    src = np.arange(MLA_ROPE)
    pp = np.zeros((LANES, LANES), np.float32)
    pp[MISC_KPE + src, MLA_NOPE + src] = 1.0
    pr = np.zeros((LANES, LANES), np.float32)
    pr[MISC_KPE + ridx, MLA_NOPE + src] = rsign
    return invf, jnp.asarray(pp, BF16), jnp.asarray(pr, BF16)


def _prep_ssd(ssd_conv_w, ssd_conv_b, ssd_dt_bias, ssd_a_log, ssd_d):
    depth = ssd_conv_w.shape[0]
    cw_ext = jnp.concatenate([ssd_conv_w, ssd_conv_b[:, None, :],
                              jnp.zeros((depth, SUBLANES - SSD_CONV - 1, ssd_conv_w.shape[-1]), F32)], axis=1)

    def per_group(p):
        return p.reshape(depth, 2, SSD_GROUPS, SSD_HPG).transpose(0, 2, 1, 3).reshape(depth, SSD_GROUPS, 2 * SSD_HPG)

    bias_g, alog_g = per_group(ssd_dt_bias), per_group(ssd_a_log)
    n16 = 2 * SSD_HPG
    hps = []
    for g in range(SSD_GROUPS):
        lo = MISC_DT + g * n16
        lane_pad = ((0, 0), (lo, LANES - lo - n16))
        rows = [jnp.pad(bias_g[:, g], lane_pad), jnp.pad(alog_g[:, g], lane_pad),
                jnp.pad(jnp.ones((depth, n16), F32), lane_pad)]
        rows += [jnp.zeros((depth, LANES), F32)] * (SUBLANES - len(rows))
        hps.append(jnp.stack(rows, axis=1))
    hp = jnp.stack(hps, axis=1)
    hpt = jnp.stack([bias_g, alog_g] + [jnp.zeros_like(bias_g)] * (SUBLANES - 2), axis=-1)
    dx = jnp.repeat(ssd_d.reshape(depth, SSD_GROUPS, SSD_HPG), SSD_HEADDIM, axis=-1)[:, :, None, :]
    return cw_ext, hp, hpt, dx


def _ssd_expanders():
    out = []
    for g in range(SSD_GROUPS):
        per_dir = []
        for d in range(2):
            lo = MISC_DT + g * 2 * SSD_HPG + d * SSD_HPG
            e = np.zeros((LANES, SSD_GW), np.float32)
            e[lo + np.arange(SSD_GW) // SSD_HEADDIM, np.arange(SSD_GW)] = 1.0
            per_dir.append(jnp.asarray(e, BF16))
        out.append(per_dir)
    return out


def _prep_mixers(positions, ssd_conv_w, ssd_conv_b, ssd_dt_bias, ssd_a_log, ssd_d, ssd_norm,
                 mla_q_norm, mla_w_uq, mla_kv_norm, mla_w_ukv, mla_q_head_norm, mla_k_head_norm,
                 mla_out_norm, conv_w, conv_out_norm):
    depth = conv_w.shape[0]
    bsz, seq = positions.shape
    gsz = CONV_WIDTH // CONV_GROUPS
    li = np.arange(LANES)
    invf, pp, pr = _mla_consts()
    cos, sin = _rope_tables(positions.reshape(bsz, seq, 1), invf)
    return dict(
        mla_w=_prep_mla(mla_w_uq, mla_w_ukv, mla_q_head_norm, mla_k_head_norm),
        mla_c=(cos, sin, pp, pr),
        mla_n=(mla_q_norm, mla_kv_norm, mla_out_norm),
        ssd=_prep_ssd(ssd_conv_w, ssd_conv_b, ssd_dt_bias, ssd_a_log, ssd_d),
        ssd_norm=ssd_norm,
        expanders=_ssd_expanders(),
        gc_w=jnp.concatenate([conv_w, jnp.zeros((depth, SUBLANES - CONV_K, CONV_WIDTH), F32)], axis=1),
        gc_g=conv_out_norm,
        gm=jnp.asarray((li[:, None] // gsz == li[None, :] // gsz).astype(np.float32), BF16),
    )


def _mixers(u3, dtt, mp, l):
    bsz, seq, _ = u3.shape
    t = bsz * seq
    tq = 512 if seq % 512 == 0 else SSD_CHUNK
    cw_ext, hp, hpt, dx = mp["ssd"]
    ys = [_ssd_group(u3, dtt, cw_ext[l], hp[l, g], hpt[l, g], dx[l, g], mp["ssd_norm"][l][None, :],
                     *mp["expanders"][g], g).reshape(t, SSD_GW)
          for g in range(SSD_GROUPS)]
    wq, wqr, wk, wv, gq, gk = mp["mla_w"]
    cos, sin, pp, pr = mp["mla_c"]
    qn, kvn, go = mp["mla_n"]
    y_mla = _mla(u3, cos, sin, qn[l][None, :], wq[l], wqr[l], kvn[l][None, :], wk[l], wv[l],
                 gq[l], gk[l], go[l][None, :], pp, pr, tq).reshape(t, MLA_WIDTH)
    y_conv = _gconv(u3, mp["gc_w"][l], mp["gc_g"][l][None, :], mp["gm"]).reshape(t, CONV_WIDTH)
    return ys + [y_mla, y_conv]


def kernel(x, positions, ffn1_norm, ffn1_w_gate, ffn1_w_up, ffn1_w_down, mix_norm, w_in, ssd_conv_w, ssd_conv_b, ssd_dt_bias, ssd_a_log, ssd_d, ssd_norm, mla_q_norm, mla_w_uq, mla_kv_norm, mla_w_ukv, mla_q_head_norm, mla_k_head_norm, mla_out_norm, conv_w, conv_out_norm, w_out, ffn2_norm, ffn2_w_gate, ffn2_w_up, ffn2_w_down):
    bsz, seq, _ = x.shape
    depth = w_in.shape[0]
    t = bsz * seq
    assert seq % 256 == 0
    tm = 1024 if t % 1024 == 0 else 256
    ts = 512 if seq % 512 == 0 else 256

    row = lambda a: a[:, None, :]
    f1 = (row(ffn1_norm), ffn1_w_gate.astype(BF16), ffn1_w_up.astype(BF16), (0.5 * ffn1_w_down).astype(BF16))
    f2 = (row(ffn2_norm), ffn2_w_gate.astype(BF16), ffn2_w_up.astype(BF16), (0.5 * ffn2_w_down).astype(BF16))
    w_in_p = _prep_w_in(w_in)
    w_out_b = w_out.astype(BF16)
    mp = _prep_mixers(positions, ssd_conv_w, ssd_conv_b, ssd_dt_bias, ssd_a_log, ssd_d, ssd_norm,
                      mla_q_norm, mla_w_uq, mla_kv_norm, mla_w_ukv, mla_q_head_norm, mla_k_head_norm,
                      mla_out_norm, conv_w, conv_out_norm)

    xt = x.reshape(t, D_MODEL)
    for l in range(depth):
        xt = _ffn(xt, *f1, l, tm)
        u3, dtt = _inproj(xt.reshape(bsz, seq, D_MODEL), row(mix_norm), w_in_p, l, ts)
        ys = _mixers(u3, dtt, mp, l)
        xt = _outproj(xt, ys, w_out_b, l, tm)
        xt = _ffn(xt, *f2, l, tm)
    return xt.reshape(bsz, seq, D_MODEL)
```

```python
import functools
import math

import numpy as np
import jax
import jax.numpy as jnp
from jax import lax
from jax.experimental import pallas as pl
from jax.experimental.pallas import tpu as pltpu

F32 = jnp.float32
BF16 = jnp.bfloat16

D_MODEL = 1024
D_FF = 2816
RMS_EPS = 1e-6

SSD_WIDTH = 1024
SSD_HEADDIM = 64
SSD_HEADS = 16
SSD_GROUPS = 2
SSD_HPG = 8
SSD_STATE = 128
SSD_CONV = 5
SSD_CHUNK = 128
SSD_GW = SSD_HPG * SSD_HEADDIM

MLA_HEADS = 8
MLA_Q_LORA = 256
MLA_KV_LORA = 128
MLA_NOPE = 64
MLA_ROPE = 32
MLA_QK = 96
MLA_V = 64
MLA_WIDTH = 512
ROPE_BASE = 10000.0

CONV_WIDTH = 512
CONV_GROUPS = 8
CONV_K = 3

D_MIX = 2048
LANES = 128
SUBLANES = 8

U_Z = 0
U_XS = 1024
U_B = 2048
U_C = 2304
U_QLAT = 2560
U_KVLAT = 2816
U_MISC = 2944
U_CH = 3072
U_CB = 3584
U_CC = 4096
U_WIDTH = 4608
MISC_KPE = 0
MISC_DT = 32

NEG_BIG = -1e30
LOG2E = math.log2(math.e)
VMEM_LIMIT = 56 * 1024 * 1024
FFN_TF = 256
P1_GROUP = 4
P2_GROUP = 4
GCONV_BLOCK = 512


def _cparams(sem):
    return pltpu.CompilerParams(dimension_semantics=sem, vmem_limit_bytes=VMEM_LIMIT)


def _resident(block_shape, index_map):
    return pl.BlockSpec(block_shape, index_map, pipeline_mode=pl.Buffered(1))


def _dot(a, b):
    return jnp.dot(a, b, preferred_element_type=F32)


def _dot_nt(a, b):
    return lax.dot_general(a, b, (((1,), (1,)), ((), ())), preferred_element_type=F32)


def _split(v, parts):
    out = []
    r = v
    for i in range(parts):
        t = r.astype(BF16)
        out.append(t)
        if i + 1 < parts:
            r = r - t.astype(F32)
    return out


def _dot_x01(v, m01, parts=3):
    acc = None
    for t in _split(v, parts):
        d = _dot(t, m01)
        acc = d if acc is None else acc + d
    return acc


def _dot_01x(m01, v, parts=3):
    acc = None
    for t in _split(v, parts):
        d = _dot(m01, t)
        acc = d if acc is None else acc + d
    return acc


def _silu(x):
    return x * jax.nn.sigmoid(x)


def _softplus(x):
    return jnp.maximum(x, 0.0) + jnp.log1p(jnp.exp(-jnp.abs(x)))


def _rms(x, g, n=None):
    n = x.shape[-1] if n is None else n
    ms = jnp.sum(x * x, axis=-1, keepdims=True) * (1.0 / n)
    return x * lax.rsqrt(ms + RMS_EPS) * g


def _ffn_kernel(x_ref, g_ref, wg_ref, wu_ref, wd_ref, o_ref, xn_ref):
    x = x_ref[...]
    xn_ref[...] = _rms(x, g_ref[...]).astype(BF16)
    o_ref[...] = x

    def body(f, carry):
        cs = pl.ds(pl.multiple_of(f * FFN_TF, FFN_TF), FFN_TF)
        xn = xn_ref[...]
        h = _silu(_dot(xn, wg_ref[:, cs])) * _dot(xn, wu_ref[:, cs])
        o_ref[...] += _dot(h.astype(BF16), wd_ref[cs, :])
        return carry

    lax.fori_loop(0, D_FF // FFN_TF, body, 0, unroll=True)


def _ffn(x, norm, wg, wu, wd_half, layer, tm):
    t = x.shape[0]
    return pl.pallas_call(
        _ffn_kernel,
        grid=(t // tm,),
        in_specs=[
            pl.BlockSpec((tm, D_MODEL), lambda i: (i, 0)),
            _resident((None, 1, D_MODEL), lambda i: (layer, 0, 0)),
            _resident((None, D_MODEL, D_FF), lambda i: (layer, 0, 0)),
            _resident((None, D_MODEL, D_FF), lambda i: (layer, 0, 0)),
            _resident((None, D_FF, D_MODEL), lambda i: (layer, 0, 0)),
        ],
        out_specs=pl.BlockSpec((tm, D_MODEL), lambda i: (i, 0)),
        out_shape=jax.ShapeDtypeStruct((t, D_MODEL), F32),
        scratch_shapes=[pltpu.VMEM((tm, D_MODEL), BF16)],
        compiler_params=_cparams(("parallel",)),
        name="ffn",
    )(x, norm, wg, wu, wd_half)


def _inproj_kernel(x_ref, g_ref, wa_ref, wm_ref, wc_ref, u_ref, dtt_ref):
    xn = _rms(x_ref[...], g_ref[...]).astype(BF16)
    u_ref[:, :U_QLAT] = _dot(xn, wa_ref[...])
    u_ref[:, U_QLAT:U_CH] = _dot(xn, wm_ref[...])
    u_ref[:, U_CH:] = _dot(xn, wc_ref[...])
    misc_t = u_ref[:, U_MISC:U_MISC + LANES].T
    dtt_ref[...] = misc_t[MISC_DT:MISC_DT + 2 * SSD_HEADS, :]


def _inproj(x3, norm, w_parts, layer, ts):
    bsz, seq, _ = x3.shape
    w_specs = [_resident((None, D_MODEL, w.shape[-1]), lambda b, i: (layer, 0, 0)) for w in w_parts]
    return pl.pallas_call(
        _inproj_kernel,
        grid=(bsz, seq // ts),
        in_specs=[
            pl.BlockSpec((None, ts, D_MODEL), lambda b, i: (b, i, 0)),
            _resident((None, 1, D_MODEL), lambda b, i: (layer, 0, 0)),
        ] + w_specs,
        out_specs=[
            pl.BlockSpec((None, ts, U_WIDTH), lambda b, i: (b, i, 0)),
            pl.BlockSpec((None, 2 * SSD_HEADS, ts), lambda b, i: (b, 0, i)),
        ],
        out_shape=[
            jax.ShapeDtypeStruct((bsz, seq, U_WIDTH), F32),
            jax.ShapeDtypeStruct((bsz, 2 * SSD_HEADS, seq), F32),
        ],
        compiler_params=_cparams(("parallel", "parallel")),
        name="inproj",
    )(x3, norm, *w_parts)


def _outproj_kernel(x_ref, y0_ref, y1_ref, y2_ref, y3_ref, w0_ref, w1_ref, w2_ref, w3_ref, o_ref):
    acc = _dot(y0_ref[...], w0_ref[...])
    acc += _dot(y1_ref[...], w1_ref[...])
    acc += _dot(y2_ref[...], w2_ref[...])
    acc += _dot(y3_ref[...], w3_ref[...])
    o_ref[...] = x_ref[...] + acc


def _outproj(x, ys, w, layer, tm):
    t = x.shape[0]
    wd = D_MIX // len(ys)
    y_spec = pl.BlockSpec((tm, wd), lambda i: (i, 0))
    w_specs = [_resident((None, wd, D_MODEL), functools.partial(lambda k, i: (layer, k, 0), k))
               for k in range(len(ys))]
    return pl.pallas_call(
        _outproj_kernel,
        grid=(t // tm,),
        in_specs=[pl.BlockSpec((tm, D_MODEL), lambda i: (i, 0))] + [y_spec] * len(ys) + w_specs,
        out_specs=pl.BlockSpec((tm, D_MODEL), lambda i: (i, 0)),
        out_shape=jax.ShapeDtypeStruct((t, D_MODEL), F32),
        compiler_params=_cparams(("parallel",)),
        name="outproj",
    )(x, *ys, *([w] * len(ys)))


def _conv_silu_block(src_ref, w, c, ls, seq):
    rows = SSD_CHUNK
    half = SSD_CONV // 2
    r0 = pl.multiple_of(c * rows, rows)
    before = pl.multiple_of(jnp.maximum(r0 - SUBLANES, 0), SUBLANES)
    after = pl.multiple_of(jnp.minimum(r0 + rows, seq - SUBLANES), SUBLANES)
    prev8 = jnp.where(c > 0, src_ref[pl.ds(before, SUBLANES), ls], 0.0)
    next8 = jnp.where(c < seq // rows - 1, src_ref[pl.ds(after, SUBLANES), ls], 0.0)
    cat = jnp.concatenate([prev8, src_ref[pl.ds(r0, rows), ls], next8], axis=0)
    acc = jnp.broadcast_to(w[SSD_CONV:SSD_CONV + 1, ls], (rows, LANES))
    for k in range(SSD_CONV):
        shifted = cat if k == half else pltpu.roll(cat, (half - k) % cat.shape[0], axis=0)
        acc = acc + w[k:k + 1, ls] * shifted[SUBLANES:SUBLANES + rows]
    return _silu(acc)


def _ssd_kernel(dt_lane, z_ref, xs_ref, b_ref, c_ref, misc_ref, dtt_ref, cwx_ref, cwb_ref, cwc_ref,
                hp_ref, hpt_ref, dx_ref, ng_ref, ef_ref, eb_ref, o_ref,
                xc, bc, cc, bt, yacc, dts, dtat, ldtt, sf, sb, ecf, ecb, decf, decb, st):
    seq = o_ref.shape[0]
    ch = SSD_CHUNK
    nc = seq // ch
    hpg = SSD_HPG
    fl = dt_lane
    bl = dt_lane + hpg

    wx, wb, wc = cwx_ref[...], cwb_ref[...], cwc_ref[...]
    one_block = slice(0, LANES)

    def chunk_rows(c):
        return pl.ds(pl.multiple_of(c * ch, ch), ch)

    def conv_chunk(c):
        rows = chunk_rows(c)
        for j in range(SSD_GW // LANES):
            ls = slice(j * LANES, (j + 1) * LANES)
            xc[rows, ls] = _conv_silu_block(xs_ref, wx, c, ls, seq)
        bv = _conv_silu_block(b_ref, wb, c, one_block, seq)
        bc[rows, :] = bv.astype(BF16)
        bt[:, rows] = bv.T.astype(BF16)
        cc[rows, :] = _conv_silu_block(c_ref, wc, c, one_block, seq).astype(BF16)

    def conv_first(c, carry):
        conv_chunk(c)
        return carry

    lax.fori_loop(0, nc, conv_first, 0)

    hp = hp_ref[...]
    bias_row = hp[0:1, :]
    a_row = -jnp.exp(hp[1:2, :]) * hp[2:3, :] * LOG2E
    hpt = hpt_ref[...]
    dts[...] = _softplus(misc_ref[...] + bias_row)
    dtt = _softplus(dtt_ref[...] + hpt[:, 0:1])
    dtat[...] = dtt * (-jnp.exp(hpt[:, 1:2]) * LOG2E)
    ldtt[...] = jnp.log(dtt) * LOG2E
    st[...] = jnp.zeros_like(st)
    lane_lo = lax.broadcasted_iota(jnp.int32, (ch, SSD_GW), 1) % LANES < SSD_HEADDIM

    row_i = lax.broadcasted_iota(jnp.int32, (ch, ch), 0)
    col_i = lax.broadcasted_iota(jnp.int32, (ch, ch), 1)
    tri_le = col_i <= row_i
    tri_ge = col_i >= row_i
    t_le = jnp.where(tri_le, 1.0, 0.0).astype(BF16)
    t_ge = jnp.where(tri_ge, 1.0, 0.0).astype(BF16)
    ef = ef_ref[...]
    eb = eb_ref[...]

    g1 = math.gcd(nc, P1_GROUP)
    g2 = math.gcd(nc, P2_GROUP)

    def pass1(i, carry):
        cs = [i * g1 + k for k in range(g1)]
        rws = [chunk_rows(c) for c in cs]
        xs_ = [xc[r, :] for r in rws]
        bts = [bt[:, r] for r in rws]
        dts_ = [dts[r, :] for r in rws]
        dtas = [dt * a_row for dt in dts_]
        cum_fs = [_dot_01x(t_le, dta) for dta in dtas]
        cumt_fs = [_dot_x01(dtat[:, r], t_ge) for r in rws]
        cbs = [_dot_nt(cc[r, :], bc[r, :]) for r in rws]
        tots = [cf[ch - 1:ch, :] for cf in cum_fs]
        cum_bs = [tot - cf + dta for tot, cf, dta in zip(tots, cum_fs, dtas)]
        spreads = []
        for dt, tot, cf, cbw in zip(dts_, tots, cum_fs, cum_bs):
            spreads.append([_dot(jnp.concatenate([jnp.exp2(cum), dt * jnp.exp2(tot - cum)], axis=0).astype(BF16), e)
                            for cum, e in ((cf, ef), (cbw, eb))])
        for k, (c, r) in enumerate(zip(cs, rws)):
            x = xs_[k]
            bt_c = bts[k]
            dec8 = jnp.broadcast_to(jnp.exp2(tots[k]), (SUBLANES, LANES))
            for spread, e, ec_ref, s_ref, dec_ref in ((spreads[k][0], ef, ecf, sf, decf),
                                                      (spreads[k][1], eb, ecb, sb, decb)):
                ec_ref[c] = spread[0:ch].astype(BF16)
                s_ref[c] = _dot(bt_c, (x * spread[ch:]).astype(BF16)).astype(BF16)
                dec_ref[c] = _dot_x01(dec8, e, parts=2)
        for k, (c, r) in enumerate(zip(cs, rws)):
            x = xs_[k]
            dtat_c = dtat[:, r]
            ldt = ldtt[:, r]
            rf = cumt_fs[k] - ldt
            rb = cumt_fs[k][:, ch - 1:ch] - cumt_fs[k] + dtat_c - ldt
            zero = jnp.zeros_like(x)
            x2 = jnp.concatenate([jnp.where(lane_lo, x, zero), jnp.where(lane_lo, zero, x)],
                                 axis=0).astype(BF16)
            cb, cum_f, cum_b = cbs[k], cum_fs[k], cum_bs[k]
            for p in range(hpg // 2):
                ls = slice(p * LANES, (p + 1) * LANES)
                ms = []
                for h in (2 * p, 2 * p + 1):
                    seg_f = cum_f[:, fl + h:fl + h + 1] - rf[h:h + 1, :]
                    seg_b = cum_b[:, bl + h:bl + h + 1] - rb[hpg + h:hpg + h + 1, :]
                    ms.append(cb * (jnp.exp2(jnp.where(tri_le, seg_f, NEG_BIG))
                                    + jnp.exp2(jnp.where(tri_ge, seg_b, NEG_BIG))))
                m = jnp.concatenate(ms, axis=1).astype(BF16)
                yacc[r, ls] = _dot(m, x2[:, ls])
        return carry

    lax.fori_loop(0, nc // g1, pass1, 0)

    def pass2(i, carry):
        work = []
        for d, ec_ref, s_ref, dec_ref in ((0, ecf, sf, decf), (1, ecb, sb, decb)):
            prev = st[d]
            for k in range(g2):
                step = i * g2 + k
                c = step if d == 0 else nc - 1 - step
                work.append((c, ec_ref, prev.astype(BF16)))
                prev = prev * dec_ref[c][0:1, :] + s_ref[c].astype(F32)
            st[d] = prev
        offs = [_dot(cc[chunk_rows(c), :], prev_b) for c, _, prev_b in work]
        for (c, ec_ref, _), off in zip(work, offs):
            yacc[chunk_rows(c), :] += off * ec_ref[c].astype(F32)
        return carry

    lax.fori_loop(0, nc // g2, pass2, 0)

    d_x = dx_ref[...]
    gain = ng_ref[...]

    def pass3(c, carry):
        rows = chunk_rows(c)
        y = (yacc[rows, :] + xc[rows, :] * d_x) * _silu(z_ref[rows, :])
        o_ref[rows, :] = _rms(y, gain).astype(o_ref.dtype)
        return carry

    lax.fori_loop(0, nc, pass3, 0, unroll=2)


def _ssd_group(u3, dtt, cw_ext, hp, hpt, dx, ng, ef, eb, group):
    bsz, seq, _ = u3.shape
    gw = SSD_GW
    n = SSD_STATE
    dt_lane = MISC_DT + group * 2 * SSD_HPG
    nchunk = seq // SSD_CHUNK
    blk = lambda w, idx: pl.BlockSpec((None, seq, w), lambda b: (b, 0, idx))
    full2 = lambda a: _resident(a.shape, lambda b: (0, 0))
    return pl.pallas_call(
        functools.partial(_ssd_kernel, dt_lane),
        grid=(bsz,),
        in_specs=[
            blk(gw, U_Z // gw + group),
            blk(gw, U_XS // gw + group),
            blk(n, U_B // n + group),
            blk(n, U_C // n + group),
            blk(LANES, U_MISC // LANES),
            pl.BlockSpec((None, 2 * SSD_HPG, seq), lambda b: (b, group, 0)),
            _resident((SUBLANES, gw), lambda b: (0, group)),
            _resident((SUBLANES, n), lambda b: (0, SSD_WIDTH // n + group)),
            _resident((SUBLANES, n), lambda b: (0, SSD_WIDTH // n + SSD_GROUPS + group)),
            full2(hp), full2(hpt), full2(dx),
            _resident((1, gw), lambda b: (0, group)),
            full2(ef), full2(eb),
        ],
        out_specs=pl.BlockSpec((None, seq, gw), lambda b: (b, 0, 0)),
        out_shape=jax.ShapeDtypeStruct((bsz, seq, gw), BF16),
        scratch_shapes=[
            pltpu.VMEM((seq, gw), F32),
            pltpu.VMEM((seq, n), BF16),
            pltpu.VMEM((seq, n), BF16),
            pltpu.VMEM((n, seq), BF16),
            pltpu.VMEM((seq, gw), F32),
            pltpu.VMEM((seq, LANES), F32),
            pltpu.VMEM((2 * SSD_HPG, seq), F32),
            pltpu.VMEM((2 * SSD_HPG, seq), F32),
            pltpu.VMEM((nchunk, n, gw), BF16),
            pltpu.VMEM((nchunk, n, gw), BF16),
            pltpu.VMEM((nchunk, SSD_CHUNK, gw), BF16),
            pltpu.VMEM((nchunk, SSD_CHUNK, gw), BF16),
            pltpu.VMEM((nchunk, SUBLANES, gw), F32),
            pltpu.VMEM((nchunk, SUBLANES, gw), F32),
            pltpu.VMEM((2, n, gw), F32),
        ],
        compiler_params=_cparams(("parallel",)),
        name=f"ssd_g{group}",
    )(u3, u3, u3, u3, u3, dtt, cw_ext, cw_ext, cw_ext, hp, hpt, dx, ng, ef, eb)


def _rope_kernel(pos_ref, invf_ref, cos_ref, sin_ref):
    ang = pos_ref[...].astype(F32) * invf_ref[...]
    cos_ref[...] = jnp.cos(ang)
    sin_ref[...] = jnp.sin(ang)


def _rope_tables(pos3, invf):
    bsz, seq, _ = pos3.shape
    out = jax.ShapeDtypeStruct((bsz, seq, LANES), F32)
    spec = pl.BlockSpec((None, seq, LANES), lambda b: (b, 0, 0))
    return pl.pallas_call(
        _rope_kernel,
        grid=(bsz,),
        in_specs=[pl.BlockSpec((None, seq, 1), lambda b: (b, 0, 0)),
                  pl.BlockSpec(invf.shape, lambda b: (0, 0))],
        out_specs=[spec, spec],
        out_shape=[out, out],
        compiler_params=_cparams(("parallel",)),
        name="rope_tables",
    )(pos3, invf)


def _mla_kernel(qlat_ref, kvlat_ref, misc_ref, cosk_ref, sink_ref, cosq_ref, sinq_ref, qn_ref, wq_ref,
                wqr_ref, kvn_ref, wk_ref, wv_ref, gq_ref, gk_ref, go_ref, pp_ref, pr_ref, o_ref,
                k_scr, v_scr):
    hs = LANES
    lane = lax.broadcasted_iota(jnp.int32, (1, hs), 1)
    ones_lane = (MLA_V, 0)
    v_mask = (lane < MLA_V, lane >= MLA_V)

    @pl.when(pl.program_id(1) == 0)
    def _():
        kvn = _rms(kvlat_ref[...], kvn_ref[...]).astype(BF16)
        hi, mid = _split(misc_ref[...], 2)
        placed = jnp.concatenate([pp_ref[...], pr_ref[...]], axis=1)
        both = _dot(hi, placed) + _dot(mid, placed)
        pe = both[:, :hs]
        pe_rot = both[:, hs:]
        gk = gk_ref[...]
        kr = pe * gk[0:1, :] * cosk_ref[...] + pe_rot * gk[1:2, :] * sink_ref[...]
        ss_pe = jnp.sum(pe * pe, axis=-1, keepdims=True)
        for hp in range(MLA_HEADS // 2):
            ls2 = slice(2 * hp * hs, (2 * hp + 2) * hs)
            kp2 = _dot(kvn, wk_ref[:, ls2])
            v2 = _dot(kvn, wv_ref[:, ls2])
            for e in range(2):
                h = 2 * hp + e
                kp = kp2[:, e * hs:(e + 1) * hs]
                ss = jnp.sum(kp * kp, axis=-1, keepdims=True) + ss_pe
                rinv = lax.rsqrt(ss * (1.0 / MLA_QK) + RMS_EPS)
                k_scr[h] = ((kp * gk[0:1, :] + kr) * rinv).astype(BF16)
                ones_row = jnp.where(lane == ones_lane[e], 1.0, 0.0)
                v_scr[h] = (v2[:, e * hs:(e + 1) * hs] + ones_row).astype(BF16)

    qn = _rms(qlat_ref[...], qn_ref[...]).astype(BF16)
    gq = gq_ref[...]
    gcos = gq[0:1, :] * cosq_ref[...]
    gsin = gq[1:2, :] * sinq_ref[...]
    scale = MLA_QK ** -0.5 * LOG2E
    q_proj = {}

    def scores(h):
        if h % 2 == 0:
            ls2 = slice(h * hs, (h + 2) * hs)
            q_proj[h // 2] = (_dot(qn, wq_ref[:, ls2]), _dot(qn, wqr_ref[:, ls2]))
        ls = slice((h % 2) * hs, (h % 2 + 1) * hs)
        qp, qr = (a[:, ls] for a in q_proj[h // 2])
        ss = jnp.sum(qp * qp, axis=-1, keepdims=True)
        rinv = lax.rsqrt(ss * (1.0 / MLA_QK) + RMS_EPS) * scale
        q = ((qp * gcos + qr * gsin) * rinv).astype(BF16)
        return _dot_nt(q, k_scr[h])

    pair = None
    s_next = scores(0)
    for h in range(MLA_HEADS):
        s = s_next
        if h + 1 < MLA_HEADS:
            s_next = scores(h + 1)
        p = jnp.exp2(s - jnp.max(s, axis=-1, keepdims=True))
        o = _dot(p.astype(BF16), v_scr[h])
        ol = ones_lane[h % 2]
        o = jnp.where(v_mask[h % 2], o, 0.0) / o[:, ol:ol + 1]
        ms = jnp.sum(o * o, axis=-1, keepdims=True) * (1.0 / MLA_V)
        on = o * lax.rsqrt(ms + RMS_EPS)
        if h % 2 == 0:
            pair = on
        else:
            ps = slice((h // 2) * hs, (h // 2 + 1) * hs)
            o_ref[:, ps] = ((pair + on) * go_ref[:, ps]).astype(o_ref.dtype)


def _mla(u3, cos, sin, qn, wq, wqr, kvn, wk, wv, gq, gk, go, pp, pr, tq):
    bsz, seq, _ = u3.shape
    full2 = lambda a: _resident(a.shape, lambda b, i: (0, 0))
    seq_blk = lambda w, idx: pl.BlockSpec((None, seq, w), lambda b, i: (b, 0, idx))
    q_blk = lambda w, idx: pl.BlockSpec((None, tq, w), lambda b, i: (b, i, idx))
    return pl.pallas_call(
        _mla_kernel,
        grid=(bsz, seq // tq),
        in_specs=[
            q_blk(MLA_Q_LORA, U_QLAT // MLA_Q_LORA),
            seq_blk(MLA_KV_LORA, U_KVLAT // MLA_KV_LORA),
            seq_blk(LANES, U_MISC // LANES),
            seq_blk(LANES, 0), seq_blk(LANES, 0),
            q_blk(LANES, 0), q_blk(LANES, 0),
            full2(qn), full2(wq), full2(wqr), full2(kvn), full2(wk), full2(wv),
            full2(gq), full2(gk), full2(go), full2(pp), full2(pr),
        ],
        out_specs=pl.BlockSpec((None, tq, MLA_WIDTH), lambda b, i: (b, i, 0)),
        out_shape=jax.ShapeDtypeStruct((bsz, seq, MLA_WIDTH), BF16),
        scratch_shapes=[
            pltpu.VMEM((MLA_HEADS, seq, LANES), BF16),
            pltpu.VMEM((MLA_HEADS, seq, LANES), BF16),
        ],
        compiler_params=_cparams(("parallel", "arbitrary")),
        name="mla",
    )(u3, u3, u3, cos, sin, cos, sin, qn, wq, wqr, kvn, wk, wv, gq, gk, go, pp, pr)


def _gconv_kernel(h_ref, b_ref, c_ref, w_ref, g_ref, gm_ref, o_ref, pad):
    seq = o_ref.shape[0]
    half = CONV_K // 2
    gsz = CONV_WIDTH // CONV_GROUPS
    gm = gm_ref[...]
    rows = 256
    for j in range(GCONV_BLOCK // LANES):
        ls = slice(j * LANES, (j + 1) * LANES)
        zeros = jnp.zeros((SUBLANES, LANES), F32)
        pad[0:SUBLANES, :] = zeros
        pad[SUBLANES + seq:2 * SUBLANES + seq, :] = zeros
        pad[SUBLANES:SUBLANES + seq, :] = c_ref[:, ls] * h_ref[:, ls]
        w = w_ref[:, ls]
        for c in range(seq // rows):
            acc = jnp.zeros((rows, LANES), F32)
            for k in range(CONV_K):
                r0 = SUBLANES - half + k + c * rows
                acc = acc + w[k:k + 1, :] * pad[r0:r0 + rows, :]
            rs = slice(c * rows, (c + 1) * rows)
            y = acc * b_ref[rs, ls]
            ss = _dot_x01(y * y, gm, parts=2)
            o_ref[rs, ls] = (y * lax.rsqrt(ss * (1.0 / gsz) + RMS_EPS) * g_ref[:, ls]).astype(o_ref.dtype)


def _gconv(u3, w_ext, g, gm):
    bsz, seq, _ = u3.shape
    bw = GCONV_BLOCK
    nj = CONV_WIDTH // bw
    blk = lambda off: pl.BlockSpec((None, seq, bw), lambda b, j: (b, 0, off // bw + j))
    return pl.pallas_call(
        _gconv_kernel,
        grid=(bsz, nj),
        in_specs=[
            blk(U_CH), blk(U_CB), blk(U_CC),
            pl.BlockSpec((SUBLANES, bw), lambda b, j: (0, j)),
            pl.BlockSpec((1, bw), lambda b, j: (0, j)),
            pl.BlockSpec(gm.shape, lambda b, j: (0, 0)),
        ],
        out_specs=pl.BlockSpec((None, seq, bw), lambda b, j: (b, 0, j)),
        out_shape=jax.ShapeDtypeStruct((bsz, seq, CONV_WIDTH), BF16),
        scratch_shapes=[pltpu.VMEM((seq + 2 * SUBLANES, LANES), F32)],
        compiler_params=_cparams(("parallel", "parallel")),
        name="gconv",
    )(u3, u3, u3, w_ext, g, gm)


def _rot_half_perm(n):
    h = n // 2
    idx = np.array([(j + h) % n for j in range(n)])
    sign = np.array([-1.0 if j < h else 1.0 for j in range(n)], np.float32)
    return idx, sign


def _prep_w_in(w_in):
    depth = w_in.shape[0]
    o_dt = 2 * SSD_WIDTH + 2 * SSD_GROUPS * SSD_STATE
    o_q = o_dt + 2 * SSD_HEADS
    o_ch = o_q + MLA_Q_LORA + MLA_KV_LORA + MLA_ROPE
    mid = w_in[:, :, o_dt:o_ch].astype(BF16)
    dt_cols = [mid[:, :, d * SSD_HEADS + g * SSD_HPG:d * SSD_HEADS + (g + 1) * SSD_HPG]
               for g in range(SSD_GROUPS) for d in range(2)]
    pad = jnp.zeros((depth, D_MODEL, U_CH - U_MISC - MLA_ROPE - 2 * SSD_HEADS), BF16)
    w_mid = jnp.concatenate([mid[:, :, o_q - o_dt:]] + dt_cols + [pad], axis=-1)
    return w_in[:, :, :o_dt].astype(BF16), w_mid, w_in[:, :, o_ch:].astype(BF16)


def _prep_mla(w_uq, w_ukv, q_head_norm, k_head_norm):
    depth = w_uq.shape[0]
    ridx, rsign = _rot_half_perm(MLA_ROPE)
    wq3 = w_uq.astype(BF16).reshape(depth, MLA_Q_LORA, MLA_HEADS, MLA_QK)
    zq = jnp.zeros((depth, MLA_Q_LORA, MLA_HEADS, LANES - MLA_QK), BF16)
    wq = jnp.concatenate([wq3, zq], axis=-1).reshape(depth, MLA_Q_LORA, MLA_HEADS * LANES)
    half = MLA_ROPE // 2
    rot = jnp.concatenate([-wq3[..., MLA_NOPE + half:], wq3[..., MLA_NOPE:MLA_NOPE + half]], axis=-1)
    zn = jnp.zeros((depth, MLA_Q_LORA, MLA_HEADS, MLA_NOPE), BF16)
    wqr = jnp.concatenate([zn, rot, zq], axis=-1).reshape(depth, MLA_Q_LORA, MLA_HEADS * LANES)
    wkv3 = w_ukv.astype(BF16).reshape(depth, MLA_KV_LORA, MLA_HEADS, MLA_NOPE + MLA_V)
    zk = jnp.zeros((depth, MLA_KV_LORA, MLA_HEADS, LANES - MLA_NOPE), BF16)
    wk = jnp.concatenate([wkv3[..., :MLA_NOPE], zk], axis=-1).reshape(depth, MLA_KV_LORA, MLA_HEADS * LANES)
    v4 = wkv3[..., MLA_NOPE:].reshape(depth, MLA_KV_LORA, MLA_HEADS // 2, 2, MLA_V)
    zv = jnp.zeros_like(v4[:, :, :, 0])
    wv = jnp.stack([jnp.concatenate([v4[:, :, :, 0], zv], axis=-1),
                    jnp.concatenate([zv, v4[:, :, :, 1]], axis=-1)], axis=3)
    wv = wv.reshape(depth, MLA_KV_LORA, MLA_HEADS * LANES)

    def gains(g):
        zt = jnp.zeros((depth, LANES - MLA_QK), F32)
        plain = jnp.concatenate([g, zt], axis=-1)
        g_rope = g[:, MLA_NOPE:]
        rot_g = jnp.concatenate([jnp.zeros((depth, MLA_NOPE), F32), g_rope[:, half:], g_rope[:, :half], zt],
                                axis=-1)
        rest = jnp.zeros((depth, SUBLANES - 2, LANES), F32)
        return jnp.concatenate([plain[:, None], rot_g[:, None], rest], axis=1)

    return wq, wqr, wk, wv, gains(q_head_norm), gains(k_head_norm)


def _mla_consts():
    ridx, rsign = _rot_half_perm(MLA_ROPE)
    half = MLA_ROPE // 2
    inv = ROPE_BASE ** (-jnp.arange(half, dtype=F32) / half)
    invf = jnp.concatenate([jnp.zeros((MLA_NOPE,), F32), inv, inv, jnp.zeros((LANES - MLA_QK,), F32)])[None, :]
    src = np.arange(MLA_ROPE)
    pp = np.zeros((LANES, LANES), np.float32)
    pp[MISC_KPE + src, MLA_NOPE + src] = 1.0
    pr = np.zeros((LANES, LANES), np.float32)
    pr[MISC_KPE + ridx, MLA_NOPE + src] = rsign
    return invf, jnp.asarray(pp, BF16), jnp.asarray(pr, BF16)


def _prep_ssd(ssd_conv_w, ssd_conv_b, ssd_dt_bias, ssd_a_log, ssd_d):
    depth = ssd_conv_w.shape[0]
    cw_ext = jnp.concatenate([ssd_conv_w, ssd_conv_b[:, None, :],
                              jnp.zeros((depth, SUBLANES - SSD_CONV - 1, ssd_conv_w.shape[-1]), F32)], axis=1)

    def per_group(p):
        return p.reshape(depth, 2, SSD_GROUPS, SSD_HPG).transpose(0, 2, 1, 3).reshape(depth, SSD_GROUPS, 2 * SSD_HPG)

    bias_g, alog_g = per_group(ssd_dt_bias), per_group(ssd_a_log)
    n16 = 2 * SSD_HPG
    hps = []
    for g in range(SSD_GROUPS):
        lo = MISC_DT + g * n16
        lane_pad = ((0, 0), (lo, LANES - lo - n16))
        rows = [jnp.pad(bias_g[:, g], lane_pad), jnp.pad(alog_g[:, g], lane_pad),
                jnp.pad(jnp.ones((depth, n16), F32), lane_pad)]
        rows += [jnp.zeros((depth, LANES), F32)] * (SUBLANES - len(rows))
        hps.append(jnp.stack(rows, axis=1))
    hp = jnp.stack(hps, axis=1)
    hpt = jnp.stack([bias_g, alog_g] + [jnp.zeros_like(bias_g)] * (SUBLANES - 2), axis=-1)
    dx = jnp.repeat(ssd_d.reshape(depth, SSD_GROUPS, SSD_HPG), SSD_HEADDIM, axis=-1)[:, :, None, :]
    return cw_ext, hp, hpt, dx


def _ssd_expanders():
    out = []
    for g in range(SSD_GROUPS):
        per_dir = []
        for d in range(2):
            lo = MISC_DT + g * 2 * SSD_HPG + d * SSD_HPG
            e = np.zeros((LANES, SSD_GW), np.float32)
            e[lo + np.arange(SSD_GW) // SSD_HEADDIM, np.arange(SSD_GW)] = 1.0
            per_dir.append(jnp.asarray(e, BF16))
        out.append(per_dir)
    return out


def _prep_mixers(positions, ssd_conv_w, ssd_conv_b, ssd_dt_bias, ssd_a_log, ssd_d, ssd_norm,
                 mla_q_norm, mla_w_uq, mla_kv_norm, mla_w_ukv, mla_q_head_norm, mla_k_head_norm,
                 mla_out_norm, conv_w, conv_out_norm):
    depth = conv_w.shape[0]
    bsz, seq = positions.shape
    gsz = CONV_WIDTH // CONV_GROUPS
    li = np.arange(LANES)
    invf, pp, pr = _mla_consts()
    cos, sin = _rope_tables(positions.reshape(bsz, seq, 1), invf)
    return dict(
        mla_w=_prep_mla(mla_w_uq, mla_w_ukv, mla_q_head_norm, mla_k_head_norm),
        mla_c=(cos, sin, pp, pr),
        mla_n=(mla_q_norm, mla_kv_norm, mla_out_norm),
        ssd=_prep_ssd(ssd_conv_w, ssd_conv_b, ssd_dt_bias, ssd_a_log, ssd_d),
        ssd_norm=ssd_norm,
        expanders=_ssd_expanders(),
        gc_w=jnp.concatenate([conv_w, jnp.zeros((depth, SUBLANES - CONV_K, CONV_WIDTH), F32)], axis=1),
        gc_g=conv_out_norm,
        gm=jnp.asarray((li[:, None] // gsz == li[None, :] // gsz).astype(np.float32), BF16),
    )


def _mixers(u3, dtt, mp, l):
    bsz, seq, _ = u3.shape
    t = bsz * seq
    tq = 512 if seq % 512 == 0 else SSD_CHUNK
    cw_ext, hp, hpt, dx = mp["ssd"]
    ys = [_ssd_group(u3, dtt, cw_ext[l], hp[l, g], hpt[l, g], dx[l, g], mp["ssd_norm"][l][None, :],
                     *mp["expanders"][g], g).reshape(t, SSD_GW)
          for g in range(SSD_GROUPS)]
    wq, wqr, wk, wv, gq, gk = mp["mla_w"]
    cos, sin, pp, pr = mp["mla_c"]
    qn, kvn, go = mp["mla_n"]
    y_mla = _mla(u3, cos, sin, qn[l][None, :], wq[l], wqr[l], kvn[l][None, :], wk[l], wv[l],
                 gq[l], gk[l], go[l][None, :], pp, pr, tq).reshape(t, MLA_WIDTH)
    y_conv = _gconv(u3, mp["gc_w"][l], mp["gc_g"][l][None, :], mp["gm"]).reshape(t, CONV_WIDTH)
    return ys + [y_mla, y_conv]


def kernel(x, positions, ffn1_norm, ffn1_w_gate, ffn1_w_up, ffn1_w_down, mix_norm, w_in, ssd_conv_w, ssd_conv_b, ssd_dt_bias, ssd_a_log, ssd_d, ssd_norm, mla_q_norm, mla_w_uq, mla_kv_norm, mla_w_ukv, mla_q_head_norm, mla_k_head_norm, mla_out_norm, conv_w, conv_out_norm, w_out, ffn2_norm, ffn2_w_gate, ffn2_w_up, ffn2_w_down):
    bsz, seq, _ = x.shape
    depth = w_in.shape[0]
    t = bsz * seq
    assert seq % 256 == 0
    tm = 1024 if t % 1024 == 0 else 256
    ts = 512 if seq % 512 == 0 else 256

    row = lambda a: a[:, None, :]
    f1 = (row(ffn1_norm), ffn1_w_gate.astype(BF16), ffn1_w_up.astype(BF16), (0.5 * ffn1_w_down).astype(BF16))
    f2 = (row(ffn2_norm), ffn2_w_gate.astype(BF16), ffn2_w_up.astype(BF16), (0.5 * ffn2_w_down).astype(BF16))
    w_in_p = _prep_w_in(w_in)
    w_out_b = w_out.astype(BF16)
    mp = _prep_mixers(positions, ssd_conv_w, ssd_conv_b, ssd_dt_bias, ssd_a_log, ssd_d, ssd_norm,
                      mla_q_norm, mla_w_uq, mla_kv_norm, mla_w_ukv, mla_q_head_norm, mla_k_head_norm,
                      mla_out_norm, conv_w, conv_out_norm)

    xt = x.reshape(t, D_MODEL)
    for l in range(depth):
        xt = _ffn(xt, *f1, l, tm)
        u3, dtt = _inproj(xt.reshape(bsz, seq, D_MODEL), row(mix_norm), w_in_p, l, ts)
        ys = _mixers(u3, dtt, mp, l)
        xt = _outproj(xt, ys, w_out_b, l, tm)
        xt = _ffn(xt, *f2, l, tm)
    return xt.reshape(bsz, seq, D_MODEL)
```

```python
import functools
import math

import numpy as np
import jax
import jax.numpy as jnp
from jax import lax
from jax.experimental import pallas as pl
from jax.experimental.pallas import tpu as pltpu

F32 = jnp.float32
BF16 = jnp.bfloat16

D_MODEL = 1024
D_FF = 2816
RMS_EPS = 1e-6

SSD_WIDTH = 1024
SSD_HEADDIM = 64
SSD_HEADS = 16
SSD_GROUPS = 2
SSD_HPG = 8
SSD_STATE = 128
SSD_CONV = 5
SSD_CHUNK = 128
SSD_GW = SSD_HPG * SSD_HEADDIM

MLA_HEADS = 8
MLA_Q_LORA = 256
MLA_KV_LORA = 128
MLA_NOPE = 64
MLA_ROPE = 32
MLA_QK = 96
MLA_V = 64
MLA_WIDTH = 512
ROPE_BASE = 10000.0

CONV_WIDTH = 512
CONV_GROUPS = 8
CONV_K = 3

D_MIX = 2048
LANES = 128
SUBLANES = 8

U_Z = 0
U_XS = 1024
U_B = 2048
U_C = 2304
U_QLAT = 2560
U_KVLAT = 2816
U_MISC = 2944
U_CH = 3072
U_CB = 3584
U_CC = 4096
U_WIDTH = 4608
MISC_KPE = 0
MISC_DT = 32

NEG_BIG = -1e30
LOG2E = math.log2(math.e)
VMEM_LIMIT = 56 * 1024 * 1024
FFN_TF = 256
P1_GROUP = 4
P2_GROUP = 4
GCONV_BLOCK = 512


def _cparams(sem):
    return pltpu.CompilerParams(dimension_semantics=sem, vmem_limit_bytes=VMEM_LIMIT)


def _resident(block_shape, index_map):
    return pl.BlockSpec(block_shape, index_map, pipeline_mode=pl.Buffered(1))


def _dot(a, b):
    return jnp.dot(a, b, preferred_element_type=F32)


def _dot_nt(a, b):
    return lax.dot_general(a, b, (((1,), (1,)), ((), ())), preferred_element_type=F32)


def _split(v, parts):
    out = []
    r = v
    for i in range(parts):
        t = r.astype(BF16)
        out.append(t)
        if i + 1 < parts:
            r = r - t.astype(F32)
    return out


def _dot_x01(v, m01, parts=3):
    acc = None
    for t in _split(v, parts):
        d = _dot(t, m01)
        acc = d if acc is None else acc + d
    return acc


def _dot_01x(m01, v, parts=3):
    acc = None
    for t in _split(v, parts):
        d = _dot(m01, t)
        acc = d if acc is None else acc + d
    return acc


def _silu(x):
    return x * jax.nn.sigmoid(x)


def _softplus(x):
    return jnp.maximum(x, 0.0) + jnp.log1p(jnp.exp(-jnp.abs(x)))


def _rms(x, g, n=None):
    n = x.shape[-1] if n is None else n
    ms = jnp.sum(x * x, axis=-1, keepdims=True) * (1.0 / n)
    return x * lax.rsqrt(ms + RMS_EPS) * g


def _ffn_kernel(x_ref, g_ref, wg_ref, wu_ref, wd_ref, o_ref, xn_ref):
    x = x_ref[...]
    xn_ref[...] = _rms(x, g_ref[...]).astype(BF16)
    o_ref[...] = x

    def body(f, carry):
        cs = pl.ds(pl.multiple_of(f * FFN_TF, FFN_TF), FFN_TF)
        xn = xn_ref[...]
        h = _silu(_dot(xn, wg_ref[:, cs])) * _dot(xn, wu_ref[:, cs])
        o_ref[...] += _dot(h.astype(BF16), wd_ref[cs, :])
        return carry

    lax.fori_loop(0, D_FF // FFN_TF, body, 0, unroll=True)


def _ffn(x, norm, wg, wu, wd_half, layer, tm):
    t = x.shape[0]
    return pl.pallas_call(
        _ffn_kernel,
        grid=(t // tm,),
        in_specs=[
            pl.BlockSpec((tm, D_MODEL), lambda i: (i, 0)),
            _resident((None, 1, D_MODEL), lambda i: (layer, 0, 0)),
            _resident((None, D_MODEL, D_FF), lambda i: (layer, 0, 0)),
            _resident((None, D_MODEL, D_FF), lambda i: (layer, 0, 0)),
            _resident((None, D_FF, D_MODEL), lambda i: (layer, 0, 0)),
        ],
        out_specs=pl.BlockSpec((tm, D_MODEL), lambda i: (i, 0)),
        out_shape=jax.ShapeDtypeStruct((t, D_MODEL), F32),
        scratch_shapes=[pltpu.VMEM((tm, D_MODEL), BF16)],
        compiler_params=_cparams(("parallel",)),
        name="ffn",
    )(x, norm, wg, wu, wd_half)


def _inproj_kernel(x_ref, g_ref, wa_ref, wm_ref, wc_ref, u_ref, dtt_ref):
    xn = _rms(x_ref[...], g_ref[...]).astype(BF16)
    u_ref[:, :U_QLAT] = _dot(xn, wa_ref[...])
    u_ref[:, U_QLAT:U_CH] = _dot(xn, wm_ref[...])
    u_ref[:, U_CH:] = _dot(xn, wc_ref[...])
    misc_t = u_ref[:, U_MISC:U_MISC + LANES].T
    dtt_ref[...] = misc_t[MISC_DT:MISC_DT + 2 * SSD_HEADS, :]


def _inproj(x3, norm, w_parts, layer, ts):
    bsz, seq, _ = x3.shape
    w_specs = [_resident((None, D_MODEL, w.shape[-1]), lambda b, i: (layer, 0, 0)) for w in w_parts]
    return pl.pallas_call(
        _inproj_kernel,
        grid=(bsz, seq // ts),
        in_specs=[
            pl.BlockSpec((None, ts, D_MODEL), lambda b, i: (b, i, 0)),
            _resident((None, 1, D_MODEL), lambda b, i: (layer, 0, 0)),
        ] + w_specs,
        out_specs=[
            pl.BlockSpec((None, ts, U_WIDTH), lambda b, i: (b, i, 0)),
            pl.BlockSpec((None, 2 * SSD_HEADS, ts), lambda b, i: (b, 0, i)),
        ],
        out_shape=[
            jax.ShapeDtypeStruct((bsz, seq, U_WIDTH), F32),
            jax.ShapeDtypeStruct((bsz, 2 * SSD_HEADS, seq), F32),
        ],
        compiler_params=_cparams(("parallel", "parallel")),
        name="inproj",
    )(x3, norm, *w_parts)


def _outproj_kernel(x_ref, y0_ref, y1_ref, y2_ref, y3_ref, w0_ref, w1_ref, w2_ref, w3_ref, o_ref):
    acc = _dot(y0_ref[...], w0_ref[...])
    acc += _dot(y1_ref[...], w1_ref[...])
    acc += _dot(y2_ref[...], w2_ref[...])
    acc += _dot(y3_ref[...], w3_ref[...])
    o_ref[...] = x_ref[...] + acc


def _outproj(x, ys, w, layer, tm):
    t = x.shape[0]
    wd = D_MIX // len(ys)
    y_spec = pl.BlockSpec((tm, wd), lambda i: (i, 0))
    w_specs = [_resident((None, wd, D_MODEL), functools.partial(lambda k, i: (layer, k, 0), k))
               for k in range(len(ys))]
    return pl.pallas_call(
        _outproj_kernel,
        grid=(t // tm,),
        in_specs=[pl.BlockSpec((tm, D_MODEL), lambda i: (i, 0))] + [y_spec] * len(ys) + w_specs,
        out_specs=pl.BlockSpec((tm, D_MODEL), lambda i: (i, 0)),
        out_shape=jax.ShapeDtypeStruct((t, D_MODEL), F32),
        compiler_params=_cparams(("parallel",)),
        name="outproj",
    )(x, *ys, *([w] * len(ys)))


def _conv_silu_block(src_ref, w, c, ls, seq):
    rows = SSD_CHUNK
    half = SSD_CONV // 2
    r0 = pl.multiple_of(c * rows, rows)
    before = pl.multiple_of(jnp.maximum(r0 - SUBLANES, 0), SUBLANES)
    after = pl.multiple_of(jnp.minimum(r0 + rows, seq - SUBLANES), SUBLANES)
    prev8 = jnp.where(c > 0, src_ref[pl.ds(before, SUBLANES), ls], 0.0)
    next8 = jnp.where(c < seq // rows - 1, src_ref[pl.ds(after, SUBLANES), ls], 0.0)
    cat = jnp.concatenate([prev8, src_ref[pl.ds(r0, rows), ls], next8], axis=0)
    acc = jnp.broadcast_to(w[SSD_CONV:SSD_CONV + 1, ls], (rows, LANES))
    for k in range(SSD_CONV):
        shifted = cat if k == half else pltpu.roll(cat, (half - k) % cat.shape[0], axis=0)
        acc = acc + w[k:k + 1, ls] * shifted[SUBLANES:SUBLANES + rows]
    return _silu(acc)


def _ssd_kernel(dt_lane, z_ref, xs_ref, b_ref, c_ref, misc_ref, dtt_ref, cwx_ref, cwb_ref, cwc_ref,
                hp_ref, hpt_ref, dx_ref, ng_ref, ef_ref, eb_ref, o_ref,
                xc, bc, cc, bt, yacc, dts, dtat, ldtt, sf, sb, ecf, ecb, decf, decb, st):
    seq = o_ref.shape[0]
    ch = SSD_CHUNK
    nc = seq // ch
    hpg = SSD_HPG
    fl = dt_lane
    bl = dt_lane + hpg

    wx, wb, wc = cwx_ref[...], cwb_ref[...], cwc_ref[...]
    one_block = slice(0, LANES)

    def chunk_rows(c):
        return pl.ds(pl.multiple_of(c * ch, ch), ch)

    def conv_chunk(c):
        rows = chunk_rows(c)
        for j in range(SSD_GW // LANES):
            ls = slice(j * LANES, (j + 1) * LANES)
            xc[rows, ls] = _conv_silu_block(xs_ref, wx, c, ls, seq)
        bv = _conv_silu_block(b_ref, wb, c, one_block, seq)
        bc[rows, :] = bv.astype(BF16)
        bt[:, rows] = bv.T.astype(BF16)
        cc[rows, :] = _conv_silu_block(c_ref, wc, c, one_block, seq).astype(BF16)

    def conv_first(c, carry):
        conv_chunk(c)
        return carry

    lax.fori_loop(0, nc, conv_first, 0)

    hp = hp_ref[...]
    bias_row = hp[0:1, :]
    a_row = -jnp.exp(hp[1:2, :]) * hp[2:3, :] * LOG2E
    hpt = hpt_ref[...]
    dts[...] = _softplus(misc_ref[...] + bias_row)
    dtt = _softplus(dtt_ref[...] + hpt[:, 0:1])
    dtat[...] = dtt * (-jnp.exp(hpt[:, 1:2]) * LOG2E)
    ldtt[...] = jnp.log(dtt) * LOG2E
    st[...] = jnp.zeros_like(st)
    lane_lo = lax.broadcasted_iota(jnp.int32, (ch, SSD_GW), 1) % LANES < SSD_HEADDIM

    row_i = lax.broadcasted_iota(jnp.int32, (ch, ch), 0)
    col_i = lax.broadcasted_iota(jnp.int32, (ch, ch), 1)
    tri_le = col_i <= row_i
    tri_ge = col_i >= row_i
    t_le = jnp.where(tri_le, 1.0, 0.0).astype(BF16)
    t_ge = jnp.where(tri_ge, 1.0, 0.0).astype(BF16)
    ef = ef_ref[...]
    eb = eb_ref[...]

    g1 = math.gcd(nc, P1_GROUP)
    g2 = math.gcd(nc, P2_GROUP)

    def pass1(i, carry):
        cs = [i * g1 + k for k in range(g1)]
        rws = [chunk_rows(c) for c in cs]
        xs_ = [xc[r, :] for r in rws]
        bts = [bt[:, r] for r in rws]
        dts_ = [dts[r, :] for r in rws]
        dtas = [dt * a_row for dt in dts_]
        cum_fs = [_dot_01x(t_le, dta) for dta in dtas]
        cumt_fs = [_dot_x01(dtat[:, r], t_ge) for r in rws]
        cbs = [_dot_nt(cc[r, :], bc[r, :]) for r in rws]
        tots = [cf[ch - 1:ch, :] for cf in cum_fs]
        cum_bs = [tot - cf + dta for tot, cf, dta in zip(tots, cum_fs, dtas)]
        spreads = []
        for dt, tot, cf, cbw in zip(dts_, tots, cum_fs, cum_bs):
            spreads.append([_dot(jnp.concatenate([jnp.exp2(cum), dt * jnp.exp2(tot - cum)], axis=0).astype(BF16), e)
                            for cum, e in ((cf, ef), (cbw, eb))])
        for k, (c, r) in enumerate(zip(cs, rws)):
            x = xs_[k]
            bt_c = bts[k]
            dec8 = jnp.broadcast_to(jnp.exp2(tots[k]), (SUBLANES, LANES))
            for spread, e, ec_ref, s_ref, dec_ref in ((spreads[k][0], ef, ecf, sf, decf),
                                                      (spreads[k][1], eb, ecb, sb, decb)):
                ec_ref[c] = spread[0:ch].astype(BF16)
                s_ref[c] = _dot(bt_c, (x * spread[ch:]).astype(BF16)).astype(BF16)
                dec_ref[c] = _dot_x01(dec8, e, parts=2)
        for k, (c, r) in enumerate(zip(cs, rws)):
            x = xs_[k]
            dtat_c = dtat[:, r]
            ldt = ldtt[:, r]
            rf = cumt_fs[k] - ldt
            rb = cumt_fs[k][:, ch - 1:ch] - cumt_fs[k] + dtat_c - ldt
            zero = jnp.zeros_like(x)
            x2 = jnp.concatenate([jnp.where(lane_lo, x, zero), jnp.where(lane_lo, zero, x)],
                                 axis=0).astype(BF16)
            cb, cum_f, cum_b = cbs[k], cum_fs[k], cum_bs[k]
            for p in range(hpg // 2):
                ls = slice(p * LANES, (p + 1) * LANES)
                ms = []
                for h in (2 * p, 2 * p + 1):
                    seg_f = cum_f[:, fl + h:fl + h + 1] - rf[h:h + 1, :]
                    seg_b = cum_b[:, bl + h:bl + h + 1] - rb[hpg + h:hpg + h + 1, :]
                    ms.append(cb * (jnp.exp2(jnp.where(tri_le, seg_f, NEG_BIG))
                                    + jnp.exp2(jnp.where(tri_ge, seg_b, NEG_BIG))))
                m = jnp.concatenate(ms, axis=1).astype(BF16)
                yacc[r, ls] = _dot(m, x2[:, ls])
        return carry

    lax.fori_loop(0, nc // g1, pass1, 0)

    def pass2(i, carry):
        work = []
        for d, ec_ref, s_ref, dec_ref in ((0, ecf, sf, decf), (1, ecb, sb, decb)):
            prev = st[d]
            for k in range(g2):
                step = i * g2 + k
                c = step if d == 0 else nc - 1 - step
                work.append((c, ec_ref, prev.astype(BF16)))
                prev = prev * dec_ref[c][0:1, :] + s_ref[c].astype(F32)
            st[d] = prev
        offs = [_dot(cc[chunk_rows(c), :], prev_b) for c, _, prev_b in work]
        for (c, ec_ref, _), off in zip(work, offs):
            yacc[chunk_rows(c), :] += off * ec_ref[c].astype(F32)
        return carry

    lax.fori_loop(0, nc // g2, pass2, 0)

    d_x = dx_ref[...]
    gain = ng_ref[...]

    def pass3(c, carry):
        rows = chunk_rows(c)
        y = (yacc[rows, :] + xc[rows, :] * d_x) * _silu(z_ref[rows, :])
        o_ref[rows, :] = _rms(y, gain).astype(o_ref.dtype)
        return carry

    lax.fori_loop(0, nc, pass3, 0, unroll=2)


def _ssd_group(u3, dtt, cw_ext, hp, hpt, dx, ng, ef, eb, group):
    bsz, seq, _ = u3.shape
    gw = SSD_GW
    n = SSD_STATE
    dt_lane = MISC_DT + group * 2 * SSD_HPG
    nchunk = seq // SSD_CHUNK
    blk = lambda w, idx: pl.BlockSpec((None, seq, w), lambda b: (b, 0, idx))
    full2 = lambda a: _resident(a.shape, lambda b: (0, 0))
    return pl.pallas_call(
        functools.partial(_ssd_kernel, dt_lane),
        grid=(bsz,),
        in_specs=[
            blk(gw, U_Z // gw + group),
            blk(gw, U_XS // gw + group),
            blk(n, U_B // n + group),
            blk(n, U_C // n + group),
            blk(LANES, U_MISC // LANES),
            pl.BlockSpec((None, 2 * SSD_HPG, seq), lambda b: (b, group, 0)),
            _resident((SUBLANES, gw), lambda b: (0, group)),
            _resident((SUBLANES, n), lambda b: (0, SSD_WIDTH // n + group)),
            _resident((SUBLANES, n), lambda b: (0, SSD_WIDTH // n + SSD_GROUPS + group)),
            full2(hp), full2(hpt), full2(dx),
            _resident((1, gw), lambda b: (0, group)),
            full2(ef), full2(eb),
        ],
        out_specs=pl.BlockSpec((None, seq, gw), lambda b: (b, 0, 0)),
        out_shape=jax.ShapeDtypeStruct((bsz, seq, gw), BF16),
        scratch_shapes=[
            pltpu.VMEM((seq, gw), F32),
            pltpu.VMEM((seq, n), BF16),
            pltpu.VMEM((seq, n), BF16),
            pltpu.VMEM((n, seq), BF16),
            pltpu.VMEM((seq, gw), F32),
            pltpu.VMEM((seq, LANES), F32),
            pltpu.VMEM((2 * SSD_HPG, seq), F32),
            pltpu.VMEM((2 * SSD_HPG, seq), F32),
            pltpu.VMEM((nchunk, n, gw), BF16),
            pltpu.VMEM((nchunk, n, gw), BF16),
            pltpu.VMEM((nchunk, SSD_CHUNK, gw), BF16),
            pltpu.VMEM((nchunk, SSD_CHUNK, gw), BF16),
            pltpu.VMEM((nchunk, SUBLANES, gw), F32),
            pltpu.VMEM((nchunk, SUBLANES, gw), F32),
            pltpu.VMEM((2, n, gw), F32),
        ],
        compiler_params=_cparams(("parallel",)),
        name=f"ssd_g{group}",
    )(u3, u3, u3, u3, u3, dtt, cw_ext, cw_ext, cw_ext, hp, hpt, dx, ng, ef, eb)


def _rope_kernel(pos_ref, invf_ref, cos_ref, sin_ref):
    ang = pos_ref[...].astype(F32) * invf_ref[...]
    cos_ref[...] = jnp.cos(ang)
    sin_ref[...] = jnp.sin(ang)


def _rope_tables(pos3, invf):
    bsz, seq, _ = pos3.shape
    out = jax.ShapeDtypeStruct((bsz, seq, LANES), F32)
    spec = pl.BlockSpec((None, seq, LANES), lambda b: (b, 0, 0))
    return pl.pallas_call(
        _rope_kernel,
        grid=(bsz,),
        in_specs=[pl.BlockSpec((None, seq, 1), lambda b: (b, 0, 0)),
                  pl.BlockSpec(invf.shape, lambda b: (0, 0))],
        out_specs=[spec, spec],
        out_shape=[out, out],
        compiler_params=_cparams(("parallel",)),
        name="rope_tables",
    )(pos3, invf)


def _mla_kernel(qlat_ref, kvlat_ref, misc_ref, cosk_ref, sink_ref, cosq_ref, sinq_ref, qn_ref, wq_ref,
                wqr_ref, kvn_ref, wk_ref, wv_ref, gq_ref, gk_ref, go_ref, pp_ref, pr_ref, o_ref,
                k_scr, v_scr):
    hs = LANES
    lane = lax.broadcasted_iota(jnp.int32, (1, hs), 1)
    ones_lane = (MLA_V, 0)
    v_mask = (lane < MLA_V, lane >= MLA_V)

    @pl.when(pl.program_id(1) == 0)
    def _():
        kvn = _rms(kvlat_ref[...], kvn_ref[...]).astype(BF16)
        hi, mid = _split(misc_ref[...], 2)
        placed = jnp.concatenate([pp_ref[...], pr_ref[...]], axis=1)
        both = _dot(hi, placed) + _dot(mid, placed)
        pe = both[:, :hs]
        pe_rot = both[:, hs:]
        gk = gk_ref[...]
        kr = pe * gk[0:1, :] * cosk_ref[...] + pe_rot * gk[1:2, :] * sink_ref[...]
        ss_pe = jnp.sum(pe * pe, axis=-1, keepdims=True)
        for hp in range(MLA_HEADS // 2):
            ls2 = slice(2 * hp * hs, (2 * hp + 2) * hs)
            kp2 = _dot(kvn, wk_ref[:, ls2])
            v2 = _dot(kvn, wv_ref[:, ls2])
            for e in range(2):
                h = 2 * hp + e
                kp = kp2[:, e * hs:(e + 1) * hs]
                ss = jnp.sum(kp * kp, axis=-1, keepdims=True) + ss_pe
                rinv = lax.rsqrt(ss * (1.0 / MLA_QK) + RMS_EPS)
                k_scr[h] = ((kp * gk[0:1, :] + kr) * rinv).astype(BF16)
                ones_row = jnp.where(lane == ones_lane[e], 1.0, 0.0)
                v_scr[h] = (v2[:, e * hs:(e + 1) * hs] + ones_row).astype(BF16)

    qn = _rms(qlat_ref[...], qn_ref[...]).astype(BF16)
    gq = gq_ref[...]
    gcos = gq[0:1, :] * cosq_ref[...]
    gsin = gq[1:2, :] * sinq_ref[...]
    scale = MLA_QK ** -0.5 * LOG2E
    q_proj = {}

    def scores(h):
        if h % 2 == 0:
            ls2 = slice(h * hs, (h + 2) * hs)
            q_proj[h // 2] = (_dot(qn, wq_ref[:, ls2]), _dot(qn, wqr_ref[:, ls2]))
        ls = slice((h % 2) * hs, (h % 2 + 1) * hs)
        qp, qr = (a[:, ls] for a in q_proj[h // 2])
        ss = jnp.sum(qp * qp, axis=-1, keepdims=True)
        rinv = lax.rsqrt(ss * (1.0 / MLA_QK) + RMS_EPS) * scale
        q = ((qp * gcos + qr * gsin) * rinv).astype(BF16)
        return _dot_nt(q, k_scr[h])

    pair = None
    s_next = scores(0)
    for h in range(MLA_HEADS):
        s = s_next
        if h + 1 < MLA_HEADS:
            s_next = scores(h + 1)
        p = jnp.exp2(s - jnp.max(s, axis=-1, keepdims=True))
        o = _dot(p.astype(BF16), v_scr[h])
        ol = ones_lane[h % 2]
        o = jnp.where(v_mask[h % 2], o, 0.0) / o[:, ol:ol + 1]
        ms = jnp.sum(o * o, axis=-1, keepdims=True) * (1.0 / MLA_V)
        on = o * lax.rsqrt(ms + RMS_EPS)
        if h % 2 == 0:
            pair = on
        else:
            ps = slice((h // 2) * hs, (h // 2 + 1) * hs)
            o_ref[:, ps] = ((pair + on) * go_ref[:, ps]).astype(o_ref.dtype)


def _mla(u3, cos, sin, qn, wq, wqr, kvn, wk, wv, gq, gk, go, pp, pr, tq):
    bsz, seq, _ = u3.shape
    full2 = lambda a: _resident(a.shape, lambda b, i: (0, 0))
    seq_blk = lambda w, idx: pl.BlockSpec((None, seq, w), lambda b, i: (b, 0, idx))
    q_blk = lambda w, idx: pl.BlockSpec((None, tq, w), lambda b, i: (b, i, idx))
    return pl.pallas_call(
        _mla_kernel,
        grid=(bsz, seq // tq),
        in_specs=[
            q_blk(MLA_Q_LORA, U_QLAT // MLA_Q_LORA),
            seq_blk(MLA_KV_LORA, U_KVLAT // MLA_KV_LORA),
            seq_blk(LANES, U_MISC // LANES),
            seq_blk(LANES, 0), seq_blk(LANES, 0),
            q_blk(LANES, 0), q_blk(LANES, 0),
            full2(qn), full2(wq), full2(wqr), full2(kvn), full2(wk), full2(wv),
            full2(gq), full2(gk), full2(go), full2(pp), full2(pr),
        ],
        out_specs=pl.BlockSpec((None, tq, MLA_WIDTH), lambda b, i: (b, i, 0)),
        out_shape=jax.ShapeDtypeStruct((bsz, seq, MLA_WIDTH), BF16),
        scratch_shapes=[
            pltpu.VMEM((MLA_HEADS, seq, LANES), BF16),
            pltpu.VMEM((MLA_HEADS, seq, LANES), BF16),
        ],
        compiler_params=_cparams(("parallel", "arbitrary")),
        name="mla",
    )(u3, u3, u3, cos, sin, cos, sin, qn, wq, wqr, kvn, wk, wv, gq, gk, go, pp, pr)


def _gconv_kernel(h_ref, b_ref, c_ref, w_ref, g_ref, gm_ref, o_ref, pad):
    seq = o_ref.shape[0]
    half = CONV_K // 2
    gsz = CONV_WIDTH // CONV_GROUPS
    gm = gm_ref[...]
    rows = 256
    for j in range(GCONV_BLOCK // LANES):
        ls = slice(j * LANES, (j + 1) * LANES)
        zeros = jnp.zeros((SUBLANES, LANES), F32)
        pad[0:SUBLANES, :] = zeros
        pad[SUBLANES + seq:2 * SUBLANES + seq, :] = zeros
        pad[SUBLANES:SUBLANES + seq, :] = c_ref[:, ls] * h_ref[:, ls]
        w = w_ref[:, ls]
        for c in range(seq // rows):
            acc = jnp.zeros((rows, LANES), F32)
            for k in range(CONV_K):
                r0 = SUBLANES - half + k + c * rows
                acc = acc + w[k:k + 1, :] * pad[r0:r0 + rows, :]
            rs = slice(c * rows, (c + 1) * rows)
            y = acc * b_ref[rs, ls]
            ss = _dot_x01(y * y, gm, parts=2)
            o_ref[rs, ls] = (y * lax.rsqrt(ss * (1.0 / gsz) + RMS_EPS) * g_ref[:, ls]).astype(o_ref.dtype)


def _gconv(u3, w_ext, g, gm):
    bsz, seq, _ = u3.shape
    bw = GCONV_BLOCK
    nj = CONV_WIDTH // bw
    blk = lambda off: pl.BlockSpec((None, seq, bw), lambda b, j: (b, 0, off // bw + j))
    return pl.pallas_call(
        _gconv_kernel,
        grid=(bsz, nj),
        in_specs=[
            blk(U_CH), blk(U_CB), blk(U_CC),
            pl.BlockSpec((SUBLANES, bw), lambda b, j: (0, j)),
            pl.BlockSpec((1, bw), lambda b, j: (0, j)),
            pl.BlockSpec(gm.shape, lambda b, j: (0, 0)),
        ],
        out_specs=pl.BlockSpec((None, seq, bw), lambda b, j: (b, 0, j)),
        out_shape=jax.ShapeDtypeStruct((bsz, seq, CONV_WIDTH), BF16),
        scratch_shapes=[pltpu.VMEM((seq + 2 * SUBLANES, LANES), F32)],
        compiler_params=_cparams(("parallel", "parallel")),
        name="gconv",
    )(u3, u3, u3, w_ext, g, gm)


def _rot_half_perm(n):
    h = n // 2
    idx = np.array([(j + h) % n for j in range(n)])
    sign = np.array([-1.0 if j < h else 1.0 for j in range(n)], np.float32)
    return idx, sign


def _prep_w_in(w_in):
    depth = w_in.shape[0]
    o_dt = 2 * SSD_WIDTH + 2 * SSD_GROUPS * SSD_STATE
    o_q = o_dt + 2 * SSD_HEADS
    o_ch = o_q + MLA_Q_LORA + MLA_KV_LORA + MLA_ROPE
    mid = w_in[:, :, o_dt:o_ch].astype(BF16)
    dt_cols = [mid[:, :, d * SSD_HEADS + g * SSD_HPG:d * SSD_HEADS + (g + 1) * SSD_HPG]
               for g in range(SSD_GROUPS) for d in range(2)]
    pad = jnp.zeros((depth, D_MODEL, U_CH - U_MISC - MLA_ROPE - 2 * SSD_HEADS), BF16)
    w_mid = jnp.concatenate([mid[:, :, o_q - o_dt:]] + dt_cols + [pad], axis=-1)
    return w_in[:, :, :o_dt].astype(BF16), w_mid, w_in[:, :, o_ch:].astype(BF16)


def _prep_mla(w_uq, w_ukv, q_head_norm, k_head_norm):
    depth = w_uq.shape[0]
    ridx, rsign = _rot_half_perm(MLA_ROPE)
    wq3 = w_uq.astype(BF16).reshape(depth, MLA_Q_LORA, MLA_HEADS, MLA_QK)
    zq = jnp.zeros((depth, MLA_Q_LORA, MLA_HEADS, LANES - MLA_QK), BF16)
    wq = jnp.concatenate([wq3, zq], axis=-1).reshape(depth, MLA_Q_LORA, MLA_HEADS * LANES)
    half = MLA_ROPE // 2
    rot = jnp.concatenate([-wq3[..., MLA_NOPE + half:], wq3[..., MLA_NOPE:MLA_NOPE + half]], axis=-1)
    zn = jnp.zeros((depth, MLA_Q_LORA, MLA_HEADS, MLA_NOPE), BF16)
    wqr = jnp.concatenate([zn, rot, zq], axis=-1).reshape(depth, MLA_Q_LORA, MLA_HEADS * LANES)
    wkv3 = w_ukv.astype(BF16).reshape(depth, MLA_KV_LORA, MLA_HEADS, MLA_NOPE + MLA_V)
    zk = jnp.zeros((depth, MLA_KV_LORA, MLA_HEADS, LANES - MLA_NOPE), BF16)
    wk = jnp.concatenate([wkv3[..., :MLA_NOPE], zk], axis=-1).reshape(depth, MLA_KV_LORA, MLA_HEADS * LANES)
    v4 = wkv3[..., MLA_NOPE:].reshape(depth, MLA_KV_LORA, MLA_HEADS // 2, 2, MLA_V)
    zv = jnp.zeros_like(v4[:, :, :, 0])
    wv = jnp.stack([jnp.concatenate([v4[:, :, :, 0], zv], axis=-1),
                    jnp.concatenate([zv, v4[:, :, :, 1]], axis=-1)], axis=3)
    wv = wv.reshape(depth, MLA_KV_LORA, MLA_HEADS * LANES)

    def gains(g):
        zt = jnp.zeros((depth, LANES - MLA_QK), F32)
        plain = jnp.concatenate([g, zt], axis=-1)
        g_rope = g[:, MLA_NOPE:]
        rot_g = jnp.concatenate([jnp.zeros((depth, MLA_NOPE), F32), g_rope[:, half:], g_rope[:, :half], zt],
                                axis=-1)
        rest = jnp.zeros((depth, SUBLANES - 2, LANES), F32)
        return jnp.concatenate([plain[:, None], rot_g[:, None], rest], axis=1)

    return wq, wqr, wk, wv, gains(q_head_norm), gains(k_head_norm)


def _mla_consts():
    ridx, rsign = _rot_half_perm(MLA_ROPE)
    half = MLA_ROPE // 2
    inv = ROPE_BASE ** (-jnp.arange(half, dtype=F32) / half)
    invf = jnp.concatenate([jnp.zeros((MLA_NOPE,), F32), inv, inv, jnp.zeros((LANES - MLA_QK,), F32)])[None, :]
    src = np.arange(MLA_ROPE)
    pp = np.zeros((LANES, LANES), np.float32)
    pp[MISC_KPE + src, MLA_NOPE + src] = 1.0
    pr = np.zeros((LANES, LANES), np.float32)
    pr[MISC_KPE + ridx, MLA_NOPE + src] = rsign
    return invf, jnp.asarray(pp, BF16), jnp.asarray(pr, BF16)


def _prep_ssd(ssd_conv_w, ssd_conv_b, ssd_dt_bias, ssd_a_log, ssd_d):
    depth = ssd_conv_w.shape[0]
    cw_ext = jnp.concatenate([ssd_conv_w, ssd_conv_b[:, None, :],
                              jnp.zeros((depth, SUBLANES - SSD_CONV - 1, ssd_conv_w.shape[-1]), F32)], axis=1)

    def per_group(p):
        return p.reshape(depth, 2, SSD_GROUPS, SSD_HPG).transpose(0, 2, 1, 3).reshape(depth, SSD_GROUPS, 2 * SSD_HPG)

    bias_g, alog_g = per_group(ssd_dt_bias), per_group(ssd_a_log)
    n16 = 2 * SSD_HPG
    hps = []
    for g in range(SSD_GROUPS):
        lo = MISC_DT + g * n16
        lane_pad = ((0, 0), (lo, LANES - lo - n16))
        rows = [jnp.pad(bias_g[:, g], lane_pad), jnp.pad(alog_g[:, g], lane_pad),
                jnp.pad(jnp.ones((depth, n16), F32), lane_pad)]
        rows += [jnp.zeros((depth, LANES), F32)] * (SUBLANES - len(rows))
        hps.append(jnp.stack(rows, axis=1))
    hp = jnp.stack(hps, axis=1)
    hpt = jnp.stack([bias_g, alog_g] + [jnp.zeros_like(bias_g)] * (SUBLANES - 2), axis=-1)
    dx = jnp.repeat(ssd_d.reshape(depth, SSD_GROUPS, SSD_HPG), SSD_HEADDIM, axis=-1)[:, :, None, :]
    return cw_ext, hp, hpt, dx


def _ssd_expanders():
    out = []
    for g in range(SSD_GROUPS):
        per_dir = []
        for d in range(2):
            lo = MISC_DT + g * 2 * SSD_HPG + d * SSD_HPG
            e = np.zeros((LANES, SSD_GW), np.float32)
            e[lo + np.arange(SSD_GW) // SSD_HEADDIM, np.arange(SSD_GW)] = 1.0
            per_dir.append(jnp.asarray(e, BF16))
        out.append(per_dir)
    return out


def _prep_mixers(positions, ssd_conv_w, ssd_conv_b, ssd_dt_bias, ssd_a_log, ssd_d, ssd_norm,
                 mla_q_norm, mla_w_uq, mla_kv_norm, mla_w_ukv, mla_q_head_norm, mla_k_head_norm,
                 mla_out_norm, conv_w, conv_out_norm):
    depth = conv_w.shape[0]
    bsz, seq = positions.shape
    gsz = CONV_WIDTH // CONV_GROUPS
    li = np.arange(LANES)
    invf, pp, pr = _mla_consts()
    cos, sin = _rope_tables(positions.reshape(bsz, seq, 1), invf)
    return dict(
        mla_w=_prep_mla(mla_w_uq, mla_w_ukv, mla_q_head_norm, mla_k_head_norm),
        mla_c=(cos, sin, pp, pr),
        mla_n=(mla_q_norm, mla_kv_norm, mla_out_norm),
        ssd=_prep_ssd(ssd_conv_w, ssd_conv_b, ssd_dt_bias, ssd_a_log, ssd_d),
        ssd_norm=ssd_norm,
        expanders=_ssd_expanders(),
        gc_w=jnp.concatenate([conv_w, jnp.zeros((depth, SUBLANES - CONV_K, CONV_WIDTH), F32)], axis=1),
        gc_g=conv_out_norm,
        gm=jnp.asarray((li[:, None] // gsz == li[None, :] // gsz).astype(np.float32), BF16),
    )


def _mixers(u3, dtt, mp, l):
    bsz, seq, _ = u3.shape
    t = bsz * seq
    tq = 1024 if seq % 1024 == 0 else SSD_CHUNK
    cw_ext, hp, hpt, dx = mp["ssd"]
    ys = [_ssd_group(u3, dtt, cw_ext[l], hp[l, g], hpt[l, g], dx[l, g], mp["ssd_norm"][l][None, :],
                     *mp["expanders"][g], g).reshape(t, SSD_GW)
          for g in range(SSD_GROUPS)]
    wq, wqr, wk, wv, gq, gk = mp["mla_w"]
    cos, sin, pp, pr = mp["mla_c"]
    qn, kvn, go = mp["mla_n"]
    y_mla = _mla(u3, cos, sin, qn[l][None, :], wq[l], wqr[l], kvn[l][None, :], wk[l], wv[l],
                 gq[l], gk[l], go[l][None, :], pp, pr, tq).reshape(t, MLA_WIDTH)
    y_conv = _gconv(u3, mp["gc_w"][l], mp["gc_g"][l][None, :], mp["gm"]).reshape(t, CONV_WIDTH)
    return ys + [y_mla, y_conv]


def kernel(x, positions, ffn1_norm, ffn1_w_gate, ffn1_w_up, ffn1_w_down, mix_norm, w_in, ssd_conv_w, ssd_conv_b, ssd_dt_bias, ssd_a_log, ssd_d, ssd_norm, mla_q_norm, mla_w_uq, mla_kv_norm, mla_w_ukv, mla_q_head_norm, mla_k_head_norm, mla_out_norm, conv_w, conv_out_norm, w_out, ffn2_norm, ffn2_w_gate, ffn2_w_up, ffn2_w_down):
    bsz, seq, _ = x.shape
    depth = w_in.shape[0]
    t = bsz * seq
    assert seq % 256 == 0
    tm = 1024 if t % 1024 == 0 else 256
    ts = 512 if seq % 512 == 0 else 256

    row = lambda a: a[:, None, :]
    f1 = (row(ffn1_norm), ffn1_w_gate.astype(BF16), ffn1_w_up.astype(BF16), (0.5 * ffn1_w_down).astype(BF16))
    f2 = (row(ffn2_norm), ffn2_w_gate.astype(BF16), ffn2_w_up.astype(BF16), (0.5 * ffn2_w_down).astype(BF16))
    w_in_p = _prep_w_in(w_in)
    w_out_b = w_out.astype(BF16)
    mp = _prep_mixers(positions, ssd_conv_w, ssd_conv_b, ssd_dt_bias, ssd_a_log, ssd_d, ssd_norm,
                      mla_q_norm, mla_w_uq, mla_kv_norm, mla_w_ukv, mla_q_head_norm, mla_k_head_norm,
                      mla_out_norm, conv_w, conv_out_norm)

    xt = x.reshape(t, D_MODEL)
    for l in range(depth):
        xt = _ffn(xt, *f1, l, tm)
        u3, dtt = _inproj(xt.reshape(bsz, seq, D_MODEL), row(mix_norm), w_in_p, l, ts)
        ys = _mixers(u3, dtt, mp, l)
        xt = _outproj(xt, ys, w_out_b, l, tm)
        xt = _ffn(xt, *f2, l, tm)
    return xt.reshape(bsz, seq, D_MODEL)
```

```python
import functools
import math

import numpy as np
import jax
import jax.numpy as jnp
from jax import lax
from jax.experimental import pallas as pl
from jax.experimental.pallas import tpu as pltpu

F32 = jnp.float32
BF16 = jnp.bfloat16

D_MODEL = 1024
D_FF = 2816
RMS_EPS = 1e-6

SSD_WIDTH = 1024
SSD_HEADDIM = 64
SSD_HEADS = 16
SSD_GROUPS = 2
SSD_HPG = 8
SSD_STATE = 128
SSD_CONV = 5
SSD_CHUNK = 128
SSD_GW = SSD_HPG * SSD_HEADDIM

MLA_HEADS = 8
MLA_Q_LORA = 256
MLA_KV_LORA = 128
MLA_NOPE = 64
MLA_ROPE = 32
MLA_QK = 96
MLA_V = 64
MLA_WIDTH = 512
ROPE_BASE = 10000.0

CONV_WIDTH = 512
CONV_GROUPS = 8
CONV_K = 3

D_MIX = 2048
LANES = 128
SUBLANES = 8

U_Z = 0
U_XS = 1024
U_B = 2048
U_C = 2304
U_QLAT = 2560
U_KVLAT = 2816
U_MISC = 2944
U_CH = 3072
U_CB = 3584
U_CC = 4096
U_WIDTH = 4608
MISC_KPE = 0
MISC_DT = 32

NEG_BIG = -1e30
LOG2E = math.log2(math.e)
VMEM_LIMIT = 56 * 1024 * 1024
FFN_TF = 256
P1_GROUP = 4
P2_GROUP = 4
GCONV_BLOCK = 512


def _cparams(sem):
    return pltpu.CompilerParams(dimension_semantics=sem, vmem_limit_bytes=VMEM_LIMIT)


def _resident(block_shape, index_map):
    return pl.BlockSpec(block_shape, index_map, pipeline_mode=pl.Buffered(1))


def _dot(a, b):
    return jnp.dot(a, b, preferred_element_type=F32)


def _dot_nt(a, b):
    return lax.dot_general(a, b, (((1,), (1,)), ((), ())), preferred_element_type=F32)


def _split(v, parts):
    out = []
    r = v
    for i in range(parts):
        t = r.astype(BF16)
        out.append(t)
        if i + 1 < parts:
            r = r - t.astype(F32)
    return out


def _dot_x01(v, m01, parts=3):
    acc = None
    for t in _split(v, parts):
        d = _dot(t, m01)
        acc = d if acc is None else acc + d
    return acc


def _dot_01x(m01, v, parts=3):
    acc = None
    for t in _split(v, parts):
        d = _dot(m01, t)
        acc = d if acc is None else acc + d
    return acc


def _silu(x):
    return x * jax.nn.sigmoid(x)


def _softplus(x):
    return jnp.maximum(x, 0.0) + jnp.log1p(jnp.exp(-jnp.abs(x)))


def _rms(x, g, n=None):
    n = x.shape[-1] if n is None else n
    ms = jnp.sum(x * x, axis=-1, keepdims=True) * (1.0 / n)
    return x * lax.rsqrt(ms + RMS_EPS) * g


def _ffn_kernel(x_ref, g_ref, wg_ref, wu_ref, wd_ref, o_ref, xn_ref):
    x = x_ref[...]
    xn_ref[...] = _rms(x, g_ref[...]).astype(BF16)
    o_ref[...] = x

    def body(f, carry):
        cs = pl.ds(pl.multiple_of(f * FFN_TF, FFN_TF), FFN_TF)
        xn = xn_ref[...]
        h = _silu(_dot(xn, wg_ref[:, cs])) * _dot(xn, wu_ref[:, cs])
        o_ref[...] += _dot(h.astype(BF16), wd_ref[cs, :])
        return carry

    lax.fori_loop(0, D_FF // FFN_TF, body, 0, unroll=True)


def _ffn(x, norm, wg, wu, wd_half, layer, tm):
    t = x.shape[0]
    return pl.pallas_call(
        _ffn_kernel,
        grid=(t // tm,),
        in_specs=[
            pl.BlockSpec((tm, D_MODEL), lambda i: (i, 0)),
            _resident((None, 1, D_MODEL), lambda i: (layer, 0, 0)),
            _resident((None, D_MODEL, D_FF), lambda i: (layer, 0, 0)),
            _resident((None, D_MODEL, D_FF), lambda i: (layer, 0, 0)),
            _resident((None, D_FF, D_MODEL), lambda i: (layer, 0, 0)),
        ],
        out_specs=pl.BlockSpec((tm, D_MODEL), lambda i: (i, 0)),
        out_shape=jax.ShapeDtypeStruct((t, D_MODEL), F32),
        scratch_shapes=[pltpu.VMEM((tm, D_MODEL), BF16)],
        compiler_params=_cparams(("parallel",)),
        name="ffn",
    )(x, norm, wg, wu, wd_half)


def _outffn_kernel(x_ref, y0_ref, y1_ref, y2_ref, y3_ref, wo_ref, g_ref, wg_ref, wu_ref, wd_ref, o_ref, xn_ref):
    y = jnp.concatenate([y0_ref[...], y1_ref[...], y2_ref[...], y3_ref[...]], axis=1)
    x = x_ref[...] + _dot(y, wo_ref[...])
    xn_ref[...] = _rms(x, g_ref[...]).astype(BF16)
    o_ref[...] = x

    def body(f, carry):
        cs = pl.ds(pl.multiple_of(f * FFN_TF, FFN_TF), FFN_TF)
        xn = xn_ref[...]
        h = _silu(_dot(xn, wg_ref[:, cs])) * _dot(xn, wu_ref[:, cs])
        o_ref[...] += _dot(h.astype(BF16), wd_ref[cs, :])
        return carry

    lax.fori_loop(0, D_FF // FFN_TF, body, 0, unroll=True)


def _outffn(x, ys, w_out, norm, wg, wu, wd_half, layer, tm):
    t = x.shape[0]
    wd = D_MIX // len(ys)
    return pl.pallas_call(
        _outffn_kernel,
        grid=(t // tm,),
        in_specs=[pl.BlockSpec((tm, D_MODEL), lambda i: (i, 0))]
        + [pl.BlockSpec((tm, wd), lambda i: (i, 0))] * len(ys)
        + [
            _resident((None, D_MIX, D_MODEL), lambda i: (layer, 0, 0)),
            _resident((None, 1, D_MODEL), lambda i: (layer, 0, 0)),
            _resident((None, D_MODEL, D_FF), lambda i: (layer, 0, 0)),
            _resident((None, D_MODEL, D_FF), lambda i: (layer, 0, 0)),
            _resident((None, D_FF, D_MODEL), lambda i: (layer, 0, 0)),
        ],
        out_specs=pl.BlockSpec((tm, D_MODEL), lambda i: (i, 0)),
        out_shape=jax.ShapeDtypeStruct((t, D_MODEL), F32),
        scratch_shapes=[pltpu.VMEM((tm, D_MODEL), BF16)],
        compiler_params=_cparams(("parallel",)),
        name="outffn",
    )(x, *ys, w_out, norm, wg, wu, wd_half)


def _inproj_kernel(x_ref, g_ref, wa_ref, wm_ref, wc_ref, u_ref, dtt_ref):
    xn = _rms(x_ref[...], g_ref[...]).astype(BF16)
    u_ref[:, :U_QLAT] = _dot(xn, wa_ref[...])
    u_ref[:, U_QLAT:U_CH] = _dot(xn, wm_ref[...])
    u_ref[:, U_CH:] = _dot(xn, wc_ref[...])
    misc_t = u_ref[:, U_MISC:U_MISC + LANES].T
    dtt_ref[...] = misc_t[MISC_DT:MISC_DT + 2 * SSD_HEADS, :]


def _inproj(x3, norm, w_parts, layer, ts):
    bsz, seq, _ = x3.shape
    w_specs = [_resident((None, D_MODEL, w.shape[-1]), lambda b, i: (layer, 0, 0)) for w in w_parts]
    return pl.pallas_call(
        _inproj_kernel,
        grid=(bsz, seq // ts),
        in_specs=[
            pl.BlockSpec((None, ts, D_MODEL), lambda b, i: (b, i, 0)),
            _resident((None, 1, D_MODEL), lambda b, i: (layer, 0, 0)),
        ] + w_specs,
        out_specs=[
            pl.BlockSpec((None, ts, U_WIDTH), lambda b, i: (b, i, 0)),
            pl.BlockSpec((None, 2 * SSD_HEADS, ts), lambda b, i: (b, 0, i)),
        ],
        out_shape=[
            jax.ShapeDtypeStruct((bsz, seq, U_WIDTH), F32),
            jax.ShapeDtypeStruct((bsz, 2 * SSD_HEADS, seq), F32),
        ],
        compiler_params=_cparams(("parallel", "parallel")),
        name="inproj",
    )(x3, norm, *w_parts)


def _outproj_kernel(x_ref, y0_ref, y1_ref, y2_ref, y3_ref, w0_ref, w1_ref, w2_ref, w3_ref, o_ref):
    acc = _dot(y0_ref[...], w0_ref[...])
    acc += _dot(y1_ref[...], w1_ref[...])
    acc += _dot(y2_ref[...], w2_ref[...])
    acc += _dot(y3_ref[...], w3_ref[...])
    o_ref[...] = x_ref[...] + acc


def _outproj(x, ys, w, layer, tm):
    t = x.shape[0]
    wd = D_MIX // len(ys)
    y_spec = pl.BlockSpec((tm, wd), lambda i: (i, 0))
    w_specs = [_resident((None, wd, D_MODEL), functools.partial(lambda k, i: (layer, k, 0), k))
               for k in range(len(ys))]
    return pl.pallas_call(
        _outproj_kernel,
        grid=(t // tm,),
        in_specs=[pl.BlockSpec((tm, D_MODEL), lambda i: (i, 0))] + [y_spec] * len(ys) + w_specs,
        out_specs=pl.BlockSpec((tm, D_MODEL), lambda i: (i, 0)),
        out_shape=jax.ShapeDtypeStruct((t, D_MODEL), F32),
        compiler_params=_cparams(("parallel",)),
        name="outproj",
    )(x, *ys, *([w] * len(ys)))


def _conv_silu_block(src_ref, w, c, ls, seq):
    rows = SSD_CHUNK
    half = SSD_CONV // 2
    r0 = pl.multiple_of(c * rows, rows)
    before = pl.multiple_of(jnp.maximum(r0 - SUBLANES, 0), SUBLANES)
    after = pl.multiple_of(jnp.minimum(r0 + rows, seq - SUBLANES), SUBLANES)
    prev8 = jnp.where(c > 0, src_ref[pl.ds(before, SUBLANES), ls], 0.0)
    next8 = jnp.where(c < seq // rows - 1, src_ref[pl.ds(after, SUBLANES), ls], 0.0)
    cat = jnp.concatenate([prev8, src_ref[pl.ds(r0, rows), ls], next8], axis=0)
    acc = jnp.broadcast_to(w[SSD_CONV:SSD_CONV + 1, ls], (rows, LANES))
    for k in range(SSD_CONV):
        shifted = cat if k == half else pltpu.roll(cat, (half - k) % cat.shape[0], axis=0)
        acc = acc + w[k:k + 1, ls] * shifted[SUBLANES:SUBLANES + rows]
    return _silu(acc)


def _ssd_kernel(dt_lane, z_ref, xs_ref, b_ref, c_ref, misc_ref, dtt_ref, cwx_ref, cwb_ref, cwc_ref,
                hp_ref, hpt_ref, dx_ref, ng_ref, ef_ref, eb_ref, o_ref,
                xc, bc, cc, bt, yacc, dts, dtat, ldtt, sf, sb, ecf, ecb, decf, decb, st):
    seq = o_ref.shape[0]
    ch = SSD_CHUNK
    nc = seq // ch
    hpg = SSD_HPG
    fl = dt_lane
    bl = dt_lane + hpg

    wx, wb, wc = cwx_ref[...], cwb_ref[...], cwc_ref[...]
    one_block = slice(0, LANES)

    def chunk_rows(c):
        return pl.ds(pl.multiple_of(c * ch, ch), ch)

    def conv_chunk(c):
        rows = chunk_rows(c)
        for j in range(SSD_GW // LANES):
            ls = slice(j * LANES, (j + 1) * LANES)
            xc[rows, ls] = _conv_silu_block(xs_ref, wx, c, ls, seq)
        bv = _conv_silu_block(b_ref, wb, c, one_block, seq)
        bc[rows, :] = bv.astype(BF16)
        bt[:, rows] = bv.T.astype(BF16)
        cc[rows, :] = _conv_silu_block(c_ref, wc, c, one_block, seq).astype(BF16)

    def conv_first(c, carry):
        conv_chunk(c)
        return carry

    lax.fori_loop(0, nc, conv_first, 0)

    hp = hp_ref[...]
    bias_row = hp[0:1, :]
    a_row = -jnp.exp(hp[1:2, :]) * hp[2:3, :] * LOG2E
    hpt = hpt_ref[...]
    dts[...] = _softplus(misc_ref[...] + bias_row)
    dtt = _softplus(dtt_ref[...] + hpt[:, 0:1])
    dtat[...] = dtt * (-jnp.exp(hpt[:, 1:2]) * LOG2E)
    ldtt[...] = jnp.log(dtt) * LOG2E
    st[...] = jnp.zeros_like(st)
    lane_lo = lax.broadcasted_iota(jnp.int32, (ch, SSD_GW), 1) % LANES < SSD_HEADDIM

    row_i = lax.broadcasted_iota(jnp.int32, (ch, ch), 0)
    col_i = lax.broadcasted_iota(jnp.int32, (ch, ch), 1)
    tri_le = col_i <= row_i
    tri_ge = col_i >= row_i
    t_le = jnp.where(tri_le, 1.0, 0.0).astype(BF16)
    t_ge = jnp.where(tri_ge, 1.0, 0.0).astype(BF16)
    ef = ef_ref[...]
    eb = eb_ref[...]

    g1 = math.gcd(nc, P1_GROUP)
    g2 = math.gcd(nc, P2_GROUP)

    def pass1(i, carry):
        cs = [i * g1 + k for k in range(g1)]
        rws = [chunk_rows(c) for c in cs]
        xs_ = [xc[r, :] for r in rws]
        bts = [bt[:, r] for r in rws]
        dts_ = [dts[r, :] for r in rws]
        dtas = [dt * a_row for dt in dts_]
        cum_fs = [_dot_01x(t_le, dta) for dta in dtas]
        cumt_fs = [_dot_x01(dtat[:, r], t_ge) for r in rws]
        cbs = [_dot_nt(cc[r, :], bc[r, :]) for r in rws]
        tots = [cf[ch - 1:ch, :] for cf in cum_fs]
        cum_bs = [tot - cf + dta for tot, cf, dta in zip(tots, cum_fs, dtas)]
        spreads = []
        for dt, tot, cf, cbw in zip(dts_, tots, cum_fs, cum_bs):
            spreads.append([_dot(jnp.concatenate([jnp.exp2(cum), dt * jnp.exp2(tot - cum)], axis=0).astype(BF16), e)
                            for cum, e in ((cf, ef), (cbw, eb))])
        for k, (c, r) in enumerate(zip(cs, rws)):
            x = xs_[k]
            bt_c = bts[k]
            dec8 = jnp.broadcast_to(jnp.exp2(tots[k]), (SUBLANES, LANES))
            for spread, e, ec_ref, s_ref, dec_ref in ((spreads[k][0], ef, ecf, sf, decf),
                                                      (spreads[k][1], eb, ecb, sb, decb)):
                ec_ref[c] = spread[0:ch].astype(BF16)
                s_ref[c] = _dot(bt_c, (x * spread[ch:]).astype(BF16)).astype(BF16)
                dec_ref[c] = _dot_x01(dec8, e, parts=2)
        for k, (c, r) in enumerate(zip(cs, rws)):
            x = xs_[k]
            dtat_c = dtat[:, r]
            ldt = ldtt[:, r]
            rf = cumt_fs[k] - ldt
            rb = cumt_fs[k][:, ch - 1:ch] - cumt_fs[k] + dtat_c - ldt
            zero = jnp.zeros_like(x)
            x2 = jnp.concatenate([jnp.where(lane_lo, x, zero), jnp.where(lane_lo, zero, x)],
                                 axis=0).astype(BF16)
            cb, cum_f, cum_b = cbs[k], cum_fs[k], cum_bs[k]
            for p in range(hpg // 2):
                ls = slice(p * LANES, (p + 1) * LANES)
                ms = []
                for h in (2 * p, 2 * p + 1):
                    seg_f = cum_f[:, fl + h:fl + h + 1] - rf[h:h + 1, :]
                    seg_b = cum_b[:, bl + h:bl + h + 1] - rb[hpg + h:hpg + h + 1, :]
                    ms.append(cb * (jnp.exp2(jnp.where(tri_le, seg_f, NEG_BIG))
                                    + jnp.exp2(jnp.where(tri_ge, seg_b, NEG_BIG))))
                m = jnp.concatenate(ms, axis=1).astype(BF16)
                yacc[r, ls] = _dot(m, x2[:, ls])
        return carry

    lax.fori_loop(0, nc // g1, pass1, 0)

    def pass2(i, carry):
        work = []
        for d, ec_ref, s_ref, dec_ref in ((0, ecf, sf, decf), (1, ecb, sb, decb)):
            prev = st[d]
            for k in range(g2):
                step = i * g2 + k
                c = step if d == 0 else nc - 1 - step
                work.append((c, ec_ref, prev.astype(BF16)))
                prev = prev * dec_ref[c][0:1, :] + s_ref[c].astype(F32)
            st[d] = prev
        offs = [_dot(cc[chunk_rows(c), :], prev_b) for c, _, prev_b in work]
        for (c, ec_ref, _), off in zip(work, offs):
            yacc[chunk_rows(c), :] += off * ec_ref[c].astype(F32)
        return carry

    lax.fori_loop(0, nc // g2, pass2, 0)

    d_x = dx_ref[...]
    gain = ng_ref[...]

    def pass3(c, carry):
        rows = chunk_rows(c)
        y = (yacc[rows, :] + xc[rows, :] * d_x) * _silu(z_ref[rows, :])
        o_ref[rows, :] = _rms(y, gain).astype(o_ref.dtype)
        return carry

    lax.fori_loop(0, nc, pass3, 0, unroll=2)


def _ssd_group(u3, dtt, cw_ext, hp, hpt, dx, ng, ef, eb, group):
    bsz, seq, _ = u3.shape
    gw = SSD_GW
    n = SSD_STATE
    dt_lane = MISC_DT + group * 2 * SSD_HPG
    nchunk = seq // SSD_CHUNK
    blk = lambda w, idx: pl.BlockSpec((None, seq, w), lambda b: (b, 0, idx))
    full2 = lambda a: _resident(a.shape, lambda b: (0, 0))
    return pl.pallas_call(
        functools.partial(_ssd_kernel, dt_lane),
        grid=(bsz,),
        in_specs=[
            blk(gw, U_Z // gw + group),
            blk(gw, U_XS // gw + group),
            blk(n, U_B // n + group),
            blk(n, U_C // n + group),
            blk(LANES, U_MISC // LANES),
            pl.BlockSpec((None, 2 * SSD_HPG, seq), lambda b: (b, group, 0)),
            _resident((SUBLANES, gw), lambda b: (0, group)),
            _resident((SUBLANES, n), lambda b: (0, SSD_WIDTH // n + group)),
            _resident((SUBLANES, n), lambda b: (0, SSD_WIDTH // n + SSD_GROUPS + group)),
            full2(hp), full2(hpt), full2(dx),
            _resident((1, gw), lambda b: (0, group)),
            full2(ef), full2(eb),
        ],
        out_specs=pl.BlockSpec((None, seq, gw), lambda b: (b, 0, 0)),
        out_shape=jax.ShapeDtypeStruct((bsz, seq, gw), BF16),
        scratch_shapes=[
            pltpu.VMEM((seq, gw), F32),
            pltpu.VMEM((seq, n), BF16),
            pltpu.VMEM((seq, n), BF16),
            pltpu.VMEM((n, seq), BF16),
            pltpu.VMEM((seq, gw), F32),
            pltpu.VMEM((seq, LANES), F32),
            pltpu.VMEM((2 * SSD_HPG, seq), F32),
            pltpu.VMEM((2 * SSD_HPG, seq), F32),
            pltpu.VMEM((nchunk, n, gw), BF16),
            pltpu.VMEM((nchunk, n, gw), BF16),
            pltpu.VMEM((nchunk, SSD_CHUNK, gw), BF16),
            pltpu.VMEM((nchunk, SSD_CHUNK, gw), BF16),
            pltpu.VMEM((nchunk, SUBLANES, gw), F32),
            pltpu.VMEM((nchunk, SUBLANES, gw), F32),
            pltpu.VMEM((2, n, gw), F32),
        ],
        compiler_params=_cparams(("parallel",)),
        name=f"ssd_g{group}",
    )(u3, u3, u3, u3, u3, dtt, cw_ext, cw_ext, cw_ext, hp, hpt, dx, ng, ef, eb)


def _rope_kernel(pos_ref, invf_ref, cos_ref, sin_ref):
    ang = pos_ref[...].astype(F32) * invf_ref[...]
    cos_ref[...] = jnp.cos(ang)
    sin_ref[...] = jnp.sin(ang)


def _rope_tables(pos3, invf):
    bsz, seq, _ = pos3.shape
    out = jax.ShapeDtypeStruct((bsz, seq, LANES), F32)
    spec = pl.BlockSpec((None, seq, LANES), lambda b: (b, 0, 0))
    return pl.pallas_call(
        _rope_kernel,
        grid=(bsz,),
        in_specs=[pl.BlockSpec((None, seq, 1), lambda b: (b, 0, 0)),
                  pl.BlockSpec(invf.shape, lambda b: (0, 0))],
        out_specs=[spec, spec],
        out_shape=[out, out],
        compiler_params=_cparams(("parallel",)),
        name="rope_tables",
    )(pos3, invf)


def _mla_kernel(qlat_ref, kvlat_ref, misc_ref, cosk_ref, sink_ref, cosq_ref, sinq_ref, qn_ref, wq_ref,
                wqr_ref, kvn_ref, wk_ref, wv_ref, gq_ref, gk_ref, go_ref, pp_ref, pr_ref, o_ref,
                k_scr, v_scr):
    hs = LANES
    lane = lax.broadcasted_iota(jnp.int32, (1, hs), 1)
    ones_lane = (MLA_V, 0)
    v_mask = (lane < MLA_V, lane >= MLA_V)

    @pl.when(pl.program_id(1) == 0)
    def _():
        kvn = _rms(kvlat_ref[...], kvn_ref[...]).astype(BF16)
        hi, mid = _split(misc_ref[...], 2)
        placed = jnp.concatenate([pp_ref[...], pr_ref[...]], axis=1)
        both = _dot(hi, placed) + _dot(mid, placed)
        pe = both[:, :hs]
        pe_rot = both[:, hs:]
        gk = gk_ref[...]
        kr = pe * gk[0:1, :] * cosk_ref[...] + pe_rot * gk[1:2, :] * sink_ref[...]
        ss_pe = jnp.sum(pe * pe, axis=-1, keepdims=True)
        for hp in range(MLA_HEADS // 2):
            ls2 = slice(2 * hp * hs, (2 * hp + 2) * hs)
            kp2 = _dot(kvn, wk_ref[:, ls2])
            v2 = _dot(kvn, wv_ref[:, ls2])
            for e in range(2):
                h = 2 * hp + e
                kp = kp2[:, e * hs:(e + 1) * hs]
                ss = jnp.sum(kp * kp, axis=-1, keepdims=True) + ss_pe
                rinv = lax.rsqrt(ss * (1.0 / MLA_QK) + RMS_EPS)
                k_scr[h] = ((kp * gk[0:1, :] + kr) * rinv).astype(BF16)
                ones_row = jnp.where(lane == ones_lane[e], 1.0, 0.0)
                v_scr[h] = (v2[:, e * hs:(e + 1) * hs] + ones_row).astype(BF16)

    qn = _rms(qlat_ref[...], qn_ref[...]).astype(BF16)
    gq = gq_ref[...]
    gcos = gq[0:1, :] * cosq_ref[...]
    gsin = gq[1:2, :] * sinq_ref[...]
    scale = MLA_QK ** -0.5 * LOG2E
    q_proj = {}

    def scores(h):
        if h % 2 == 0:
            ls2 = slice(h * hs, (h + 2) * hs)
            q_proj[h // 2] = (_dot(qn, wq_ref[:, ls2]), _dot(qn, wqr_ref[:, ls2]))
        ls = slice((h % 2) * hs, (h % 2 + 1) * hs)
        qp, qr = (a[:, ls] for a in q_proj[h // 2])
        ss = jnp.sum(qp * qp, axis=-1, keepdims=True)
        rinv = lax.rsqrt(ss * (1.0 / MLA_QK) + RMS_EPS) * scale
        q = ((qp * gcos + qr * gsin) * rinv).astype(BF16)
        return _dot_nt(q, k_scr[h])

    pair = None
    s_next = scores(0)
    for h in range(MLA_HEADS):
        s = s_next
        if h + 1 < MLA_HEADS:
            s_next = scores(h + 1)
        p = jnp.exp2(s - jnp.max(s, axis=-1, keepdims=True))
        o = _dot(p.astype(BF16), v_scr[h])
        ol = ones_lane[h % 2]
        o = jnp.where(v_mask[h % 2], o, 0.0) / o[:, ol:ol + 1]
        ms = jnp.sum(o * o, axis=-1, keepdims=True) * (1.0 / MLA_V)
        on = o * lax.rsqrt(ms + RMS_EPS)
        if h % 2 == 0:
            pair = on
        else:
            ps = slice((h // 2) * hs, (h // 2 + 1) * hs)
            o_ref[:, ps] = ((pair + on) * go_ref[:, ps]).astype(o_ref.dtype)


def _mla(u3, cos, sin, qn, wq, wqr, kvn, wk, wv, gq, gk, go, pp, pr, tq):
    bsz, seq, _ = u3.shape
    full2 = lambda a: _resident(a.shape, lambda b, i: (0, 0))
    seq_blk = lambda w, idx: pl.BlockSpec((None, seq, w), lambda b, i: (b, 0, idx))
    q_blk = lambda w, idx: pl.BlockSpec((None, tq, w), lambda b, i: (b, i, idx))
    return pl.pallas_call(
        _mla_kernel,
        grid=(bsz, seq // tq),
        in_specs=[
            q_blk(MLA_Q_LORA, U_QLAT // MLA_Q_LORA),
            seq_blk(MLA_KV_LORA, U_KVLAT // MLA_KV_LORA),
            seq_blk(LANES, U_MISC // LANES),
            seq_blk(LANES, 0), seq_blk(LANES, 0),
            q_blk(LANES, 0), q_blk(LANES, 0),
            full2(qn), full2(wq), full2(wqr), full2(kvn), full2(wk), full2(wv),
            full2(gq), full2(gk), full2(go), full2(pp), full2(pr),
        ],
        out_specs=pl.BlockSpec((None, tq, MLA_WIDTH), lambda b, i: (b, i, 0)),
        out_shape=jax.ShapeDtypeStruct((bsz, seq, MLA_WIDTH), BF16),
        scratch_shapes=[
            pltpu.VMEM((MLA_HEADS, seq, LANES), BF16),
            pltpu.VMEM((MLA_HEADS, seq, LANES), BF16),
        ],
        compiler_params=_cparams(("parallel", "arbitrary")),
        name="mla",
    )(u3, u3, u3, cos, sin, cos, sin, qn, wq, wqr, kvn, wk, wv, gq, gk, go, pp, pr)


def _gconv_kernel(h_ref, b_ref, c_ref, w_ref, g_ref, gm_ref, o_ref, pad):
    seq = o_ref.shape[0]
    half = CONV_K // 2
    gsz = CONV_WIDTH // CONV_GROUPS
    gm = gm_ref[...]
    rows = 256
    for j in range(GCONV_BLOCK // LANES):
        ls = slice(j * LANES, (j + 1) * LANES)
        zeros = jnp.zeros((SUBLANES, LANES), F32)
        pad[0:SUBLANES, :] = zeros
        pad[SUBLANES + seq:2 * SUBLANES + seq, :] = zeros
        pad[SUBLANES:SUBLANES + seq, :] = c_ref[:, ls] * h_ref[:, ls]
        w = w_ref[:, ls]
        for c in range(seq // rows):
            acc = jnp.zeros((rows, LANES), F32)
            for k in range(CONV_K):
                r0 = SUBLANES - half + k + c * rows
                acc = acc + w[k:k + 1, :] * pad[r0:r0 + rows, :]
            rs = slice(c * rows, (c + 1) * rows)
            y = acc * b_ref[rs, ls]
            ss = _dot_x01(y * y, gm, parts=2)
            o_ref[rs, ls] = (y * lax.rsqrt(ss * (1.0 / gsz) + RMS_EPS) * g_ref[:, ls]).astype(o_ref.dtype)


def _gconv(u3, w_ext, g, gm):
    bsz, seq, _ = u3.shape
    bw = GCONV_BLOCK
    nj = CONV_WIDTH // bw
    blk = lambda off: pl.BlockSpec((None, seq, bw), lambda b, j: (b, 0, off // bw + j))
    return pl.pallas_call(
        _gconv_kernel,
        grid=(bsz, nj),
        in_specs=[
            blk(U_CH), blk(U_CB), blk(U_CC),
            pl.BlockSpec((SUBLANES, bw), lambda b, j: (0, j)),
            pl.BlockSpec((1, bw), lambda b, j: (0, j)),
            pl.BlockSpec(gm.shape, lambda b, j: (0, 0)),
        ],
        out_specs=pl.BlockSpec((None, seq, bw), lambda b, j: (b, 0, j)),
        out_shape=jax.ShapeDtypeStruct((bsz, seq, CONV_WIDTH), BF16),
        scratch_shapes=[pltpu.VMEM((seq + 2 * SUBLANES, LANES), F32)],
        compiler_params=_cparams(("parallel", "parallel")),
        name="gconv",
    )(u3, u3, u3, w_ext, g, gm)


def _rot_half_perm(n):
    h = n // 2
    idx = np.array([(j + h) % n for j in range(n)])
    sign = np.array([-1.0 if j < h else 1.0 for j in range(n)], np.float32)
    return idx, sign


def _prep_w_in(w_in):
    depth = w_in.shape[0]
    o_dt = 2 * SSD_WIDTH + 2 * SSD_GROUPS * SSD_STATE
    o_q = o_dt + 2 * SSD_HEADS
    o_ch = o_q + MLA_Q_LORA + MLA_KV_LORA + MLA_ROPE
    mid = w_in[:, :, o_dt:o_ch].astype(BF16)
    dt_cols = [mid[:, :, d * SSD_HEADS + g * SSD_HPG:d * SSD_HEADS + (g + 1) * SSD_HPG]
               for g in range(SSD_GROUPS) for d in range(2)]
    pad = jnp.zeros((depth, D_MODEL, U_CH - U_MISC - MLA_ROPE - 2 * SSD_HEADS), BF16)
    w_mid = jnp.concatenate([mid[:, :, o_q - o_dt:]] + dt_cols + [pad], axis=-1)
    return w_in[:, :, :o_dt].astype(BF16), w_mid, w_in[:, :, o_ch:].astype(BF16)


def _prep_mla(w_uq, w_ukv, q_head_norm, k_head_norm):
    depth = w_uq.shape[0]
    ridx, rsign = _rot_half_perm(MLA_ROPE)
    wq3 = w_uq.astype(BF16).reshape(depth, MLA_Q_LORA, MLA_HEADS, MLA_QK)
    zq = jnp.zeros((depth, MLA_Q_LORA, MLA_HEADS, LANES - MLA_QK), BF16)
    wq = jnp.concatenate([wq3, zq], axis=-1).reshape(depth, MLA_Q_LORA, MLA_HEADS * LANES)
    half = MLA_ROPE // 2
    rot = jnp.concatenate([-wq3[..., MLA_NOPE + half:], wq3[..., MLA_NOPE:MLA_NOPE + half]], axis=-1)
    zn = jnp.zeros((depth, MLA_Q_LORA, MLA_HEADS, MLA_NOPE), BF16)
    wqr = jnp.concatenate([zn, rot, zq], axis=-1).reshape(depth, MLA_Q_LORA, MLA_HEADS * LANES)
    wkv3 = w_ukv.astype(BF16).reshape(depth, MLA_KV_LORA, MLA_HEADS, MLA_NOPE + MLA_V)
    zk = jnp.zeros((depth, MLA_KV_LORA, MLA_HEADS, LANES - MLA_NOPE), BF16)
    wk = jnp.concatenate([wkv3[..., :MLA_NOPE], zk], axis=-1).reshape(depth, MLA_KV_LORA, MLA_HEADS * LANES)
    v4 = wkv3[..., MLA_NOPE:].reshape(depth, MLA_KV_LORA, MLA_HEADS // 2, 2, MLA_V)
    zv = jnp.zeros_like(v4[:, :, :, 0])
    wv = jnp.stack([jnp.concatenate([v4[:, :, :, 0], zv], axis=-1),
                    jnp.concatenate([zv, v4[:, :, :, 1]], axis=-1)], axis=3)
    wv = wv.reshape(depth, MLA_KV_LORA, MLA_HEADS * LANES)

    def gains(g):
        zt = jnp.zeros((depth, LANES - MLA_QK), F32)
        plain = jnp.concatenate([g, zt], axis=-1)
        g_rope = g[:, MLA_NOPE:]
        rot_g = jnp.concatenate([jnp.zeros((depth, MLA_NOPE), F32), g_rope[:, half:], g_rope[:, :half], zt],
                                axis=-1)
        rest = jnp.zeros((depth, SUBLANES - 2, LANES), F32)
        return jnp.concatenate([plain[:, None], rot_g[:, None], rest], axis=1)

    return wq, wqr, wk, wv, gains(q_head_norm), gains(k_head_norm)


def _mla_consts():
    ridx, rsign = _rot_half_perm(MLA_ROPE)
    half = MLA_ROPE // 2
    inv = ROPE_BASE ** (-jnp.arange(half, dtype=F32) / half)
    invf = jnp.concatenate([jnp.zeros((MLA_NOPE,), F32), inv, inv, jnp.zeros((LANES - MLA_QK,), F32)])[None, :]
    src = np.arange(MLA_ROPE)
    pp = np.zeros((LANES, LANES), np.float32)
    pp[MISC_KPE + src, MLA_NOPE + src] = 1.0
    pr = np.zeros((LANES, LANES), np.float32)
    pr[MISC_KPE + ridx, MLA_NOPE + src] = rsign
    return invf, jnp.asarray(pp, BF16), jnp.asarray(pr, BF16)


def _prep_ssd(ssd_conv_w, ssd_conv_b, ssd_dt_bias, ssd_a_log, ssd_d):
    depth = ssd_conv_w.shape[0]
    cw_ext = jnp.concatenate([ssd_conv_w, ssd_conv_b[:, None, :],
                              jnp.zeros((depth, SUBLANES - SSD_CONV - 1, ssd_conv_w.shape[-1]), F32)], axis=1)

    def per_group(p):
        return p.reshape(depth, 2, SSD_GROUPS, SSD_HPG).transpose(0, 2, 1, 3).reshape(depth, SSD_GROUPS, 2 * SSD_HPG)

    bias_g, alog_g = per_group(ssd_dt_bias), per_group(ssd_a_log)
    n16 = 2 * SSD_HPG
    hps = []
    for g in range(SSD_GROUPS):
        lo = MISC_DT + g * n16
        lane_pad = ((0, 0), (lo, LANES - lo - n16))
        rows = [jnp.pad(bias_g[:, g], lane_pad), jnp.pad(alog_g[:, g], lane_pad),
                jnp.pad(jnp.ones((depth, n16), F32), lane_pad)]
        rows += [jnp.zeros((depth, LANES), F32)] * (SUBLANES - len(rows))
        hps.append(jnp.stack(rows, axis=1))
    hp = jnp.stack(hps, axis=1)
    hpt = jnp.stack([bias_g, alog_g] + [jnp.zeros_like(bias_g)] * (SUBLANES - 2), axis=-1)
    dx = jnp.repeat(ssd_d.reshape(depth, SSD_GROUPS, SSD_HPG), SSD_HEADDIM, axis=-1)[:, :, None, :]
    return cw_ext, hp, hpt, dx


def _ssd_expanders():
    out = []
    for g in range(SSD_GROUPS):
        per_dir = []
        for d in range(2):
            lo = MISC_DT + g * 2 * SSD_HPG + d * SSD_HPG
            e = np.zeros((LANES, SSD_GW), np.float32)
            e[lo + np.arange(SSD_GW) // SSD_HEADDIM, np.arange(SSD_GW)] = 1.0
            per_dir.append(jnp.asarray(e, BF16))
        out.append(per_dir)
    return out


def _prep_mixers(positions, ssd_conv_w, ssd_conv_b, ssd_dt_bias, ssd_a_log, ssd_d, ssd_norm,
                 mla_q_norm, mla_w_uq, mla_kv_norm, mla_w_ukv, mla_q_head_norm, mla_k_head_norm,
                 mla_out_norm, conv_w, conv_out_norm):
    depth = conv_w.shape[0]
    bsz, seq = positions.shape
    gsz = CONV_WIDTH // CONV_GROUPS
    li = np.arange(LANES)
    invf, pp, pr = _mla_consts()
    cos, sin = _rope_tables(positions.reshape(bsz, seq, 1), invf)
    return dict(
        mla_w=_prep_mla(mla_w_uq, mla_w_ukv, mla_q_head_norm, mla_k_head_norm),
        mla_c=(cos, sin, pp, pr),
        mla_n=(mla_q_norm, mla_kv_norm, mla_out_norm),
        ssd=_prep_ssd(ssd_conv_w, ssd_conv_b, ssd_dt_bias, ssd_a_log, ssd_d),
        ssd_norm=ssd_norm,
        expanders=_ssd_expanders(),
        gc_w=jnp.concatenate([conv_w, jnp.zeros((depth, SUBLANES - CONV_K, CONV_WIDTH), F32)], axis=1),
        gc_g=conv_out_norm,
        gm=jnp.asarray((li[:, None] // gsz == li[None, :] // gsz).astype(np.float32), BF16),
    )


def _mixers(u3, dtt, mp, l):
    bsz, seq, _ = u3.shape
    t = bsz * seq
    tq = 512 if seq % 512 == 0 else SSD_CHUNK
    cw_ext, hp, hpt, dx = mp["ssd"]
    ys = [_ssd_group(u3, dtt, cw_ext[l], hp[l, g], hpt[l, g], dx[l, g], mp["ssd_norm"][l][None, :],
                     *mp["expanders"][g], g).reshape(t, SSD_GW)
          for g in range(SSD_GROUPS)]
    wq, wqr, wk, wv, gq, gk = mp["mla_w"]
    cos, sin, pp, pr = mp["mla_c"]
    qn, kvn, go = mp["mla_n"]
    y_mla = _mla(u3, cos, sin, qn[l][None, :], wq[l], wqr[l], kvn[l][None, :], wk[l], wv[l],
                 gq[l], gk[l], go[l][None, :], pp, pr, tq).reshape(t, MLA_WIDTH)
    y_conv = _gconv(u3, mp["gc_w"][l], mp["gc_g"][l][None, :], mp["gm"]).reshape(t, CONV_WIDTH)
    return ys + [y_mla, y_conv]


def kernel(x, positions, ffn1_norm, ffn1_w_gate, ffn1_w_up, ffn1_w_down, mix_norm, w_in, ssd_conv_w, ssd_conv_b, ssd_dt_bias, ssd_a_log, ssd_d, ssd_norm, mla_q_norm, mla_w_uq, mla_kv_norm, mla_w_ukv, mla_q_head_norm, mla_k_head_norm, mla_out_norm, conv_w, conv_out_norm, w_out, ffn2_norm, ffn2_w_gate, ffn2_w_up, ffn2_w_down):
    bsz, seq, _ = x.shape
    depth = w_in.shape[0]
    t = bsz * seq
    assert seq % 256 == 0
    tm = 1024 if t % 1024 == 0 else 256
    ts = 512 if seq % 512 == 0 else 256

    row = lambda a: a[:, None, :]
    f1 = (row(ffn1_norm), ffn1_w_gate.astype(BF16), ffn1_w_up.astype(BF16), (0.5 * ffn1_w_down).astype(BF16))
    f2 = (row(ffn2_norm), ffn2_w_gate.astype(BF16), ffn2_w_up.astype(BF16), (0.5 * ffn2_w_down).astype(BF16))
    w_in_p = _prep_w_in(w_in)
    w_out_b = w_out.astype(BF16)
    mp = _prep_mixers(positions, ssd_conv_w, ssd_conv_b, ssd_dt_bias, ssd_a_log, ssd_d, ssd_norm,
                      mla_q_norm, mla_w_uq, mla_kv_norm, mla_w_ukv, mla_q_head_norm, mla_k_head_norm,
                      mla_out_norm, conv_w, conv_out_norm)

    xt = x.reshape(t, D_MODEL)
    for l in range(depth):
        xt = _ffn(xt, *f1, l, tm)
        u3, dtt = _inproj(xt.reshape(bsz, seq, D_MODEL), row(mix_norm), w_in_p, l, ts)
        ys = _mixers(u3, dtt, mp, l)
        xt = _outffn(xt, ys, w_out_b, *f2, l, tm)
    return xt.reshape(bsz, seq, D_MODEL)
```
